```python
import math
import jax, jax.numpy as jnp
from jax import lax
import numpy as np

D_MODEL = 1024
BATCH = 2
SEQ = 16384
DEPTH = 1
DEC_BATCH = 4
DEC_SEQ = 8192
PAST_LEN = 128

MLA_HEADS = 8
QK_NOPE = 64
QK_ROPE = 32
QK_HEAD = QK_NOPE + QK_ROPE
V_HEAD = 64
Q_LORA = 384
KV_LORA = 256
ROPE_THETA = 10000.0
Q_BLOCK = 128
RW_HEADS = 8
RW_HEAD = 64
RW_WIDTH = RW_HEADS * RW_HEAD
DECAY_LORA = 64
A_LORA = 64
GATE_LORA = 128
LN_X_EPS = 6.4e-4
MLA_WIDTH = MLA_HEADS * V_HEAD
MIX_WIDTH = MLA_WIDTH + RW_WIDTH
MLA_COLS = Q_LORA + KV_LORA + QK_ROPE
RW_COLS = 3 * RW_WIDTH + 2 * DECAY_LORA + A_LORA + GATE_LORA
D_IN = MLA_COLS + RW_COLS
N_GROUPS = 4
EXPERTS_PER_GROUP = 8
N_EXPERTS = N_GROUPS * EXPERTS_PER_GROUP
EXPERT_TOP_K = 2
D_EXPERT = 256
MOE_BLOCK = 2048
RMS_EPS = 1e-6

kernel_name = 'hybrid_mla_rwkv7_hmoe_encoder'


def rmsnorm(x, g, eps=RMS_EPS):
    xf = x.astype(jnp.float32)
    y = xf * lax.rsqrt(jnp.mean(xf * xf, axis=-1, keepdims=True) + eps)
    return (y * g.astype(jnp.float32)).astype(x.dtype)


def apply_rope(x, seq_len):
    half = QK_ROPE // 2
    inv_freq = 1.0 / (ROPE_THETA ** (jnp.arange(half, dtype=jnp.float32) / half))
    ang = jnp.arange(seq_len, dtype=jnp.float32)[:, None] * inv_freq[None, :]
    cos = jnp.cos(ang)[None, :, None, :]
    sin = jnp.sin(ang)[None, :, None, :]
    xf = x.astype(jnp.float32)
    x1, x2 = xf[..., :half], xf[..., half:]
    return jnp.concatenate([x1 * cos - x2 * sin, x2 * cos + x1 * sin], axis=-1).astype(x.dtype)


def mla_group(c_q, c_kv, k_rope, q_a_norm_g, w_uq, kv_a_norm_g, w_ukv, q_norm_g, k_norm_g):
    b, s, _ = c_q.shape
    q = (rmsnorm(c_q, q_a_norm_g) @ w_uq).reshape(b, s, MLA_HEADS, QK_HEAD)
    kv = (rmsnorm(c_kv, kv_a_norm_g) @ w_ukv).reshape(b, s, MLA_HEADS, QK_NOPE + V_HEAD)
    k_nope, v = kv[..., :QK_NOPE], kv[..., QK_NOPE:]
    k_pe = jnp.broadcast_to(k_rope[:, :, None, :], (b, s, MLA_HEADS, QK_ROPE))
    k = jnp.concatenate([k_nope, k_pe], axis=-1)
    q = rmsnorm(q, q_norm_g)
    k = rmsnorm(k, k_norm_g)
    q = jnp.concatenate([q[..., :QK_NOPE], apply_rope(q[..., QK_NOPE:], s)], axis=-1)
    k = jnp.concatenate([k[..., :QK_NOPE], apply_rope(k[..., QK_NOPE:], s)], axis=-1)
    scale = QK_HEAD ** -0.5
    n_blk = s // Q_BLOCK
    q_blocks = q.reshape(b, n_blk, Q_BLOCK, MLA_HEADS, QK_HEAD).transpose(1, 0, 2, 3, 4)

    def attend(q_blk):
        scores = jnp.einsum('bqhd,bkhd->bhqk', q_blk, k, preferred_element_type=jnp.float32) * scale
        probs = jax.nn.softmax(scores, axis=-1).astype(v.dtype)
        return jnp.einsum('bhqk,bkhd->bqhd', probs, v)

    o = lax.map(attend, q_blocks)
    return o.transpose(1, 0, 2, 3, 4).reshape(b, s, MLA_WIDTH)


def centered_shift(z):
    prev = jnp.pad(z[:, :-1], ((0, 0), (1, 0), (0, 0)))
    nxt = jnp.pad(z[:, 1:], ((0, 0), (0, 1), (0, 0)))
    return 0.5 * (prev + nxt) - z


def rwkv7_group(u, mu_shift, w0_f, w_up_f, w0_b, w_up_b, a0, a_up, g_up, k_k, k_a, r_k, ln_x_g, ln_x_b):
    b, s, _ = u.shape
    u = u + centered_shift(u) * mu_shift
    cuts = [RW_WIDTH, 2 * RW_WIDTH, 3 * RW_WIDTH, 3 * RW_WIDTH + DECAY_LORA,
            3 * RW_WIDTH + 2 * DECAY_LORA, 3 * RW_WIDTH + 2 * DECAY_LORA + A_LORA]
    r, k, v, xw_f, xw_b, xa, xg = jnp.split(u, cuts, axis=-1)
    w_f = -jax.nn.softplus(-(w0_f + jnp.tanh(xw_f) @ w_up_f)) - 0.5
    w_b = -jax.nn.softplus(-(w0_b + jnp.tanh(xw_b) @ w_up_b)) - 0.5
    dec_f = jnp.exp(-jnp.exp(w_f.astype(jnp.float32)))
    dec_b = jnp.exp(-jnp.exp(w_b.astype(jnp.float32)))
    a = jax.nn.sigmoid(a0 + xa @ a_up)
    g = jax.nn.sigmoid(xg) @ g_up

    def heads(z):
        return z.reshape(b, s, RW_HEADS, RW_HEAD)

    kk = heads(k * k_k).astype(jnp.float32)
    kk = kk / jnp.maximum(jnp.sqrt(jnp.sum(kk * kk, axis=-1, keepdims=True)), 1e-12)
    k = k * (1.0 + (a - 1.0) * k_a)
    rh, kh, vh, ah = heads(r), heads(k), heads(v), heads(a).astype(jnp.float32)

    def time_major(z):
        return jnp.moveaxis(z.astype(jnp.float32), 1, 0)

    def both_dirs(zf, zb):
        return jnp.stack([time_major(zf), jnp.flip(time_major(zb), 0)], axis=1)

    xs = (both_dirs(rh, rh), both_dirs(kh, kh), both_dirs(vh, vh),
          both_dirs(-kk, -kk), both_dirs(kk * ah, kk * ah),
          both_dirs(heads(dec_f), heads(dec_b)))

    def step(state, inp):
        r_t, k_t, v_t, a_t, b_t, w_t = inp
        sa = jnp.einsum('dbhvk,dbhk->dbhv', state, a_t)
        state = (state * w_t[..., None, :] + sa[..., None] * b_t[..., None, :]
                 + v_t[..., None] * k_t[..., None, :])
        y_t = jnp.einsum('dbhvk,dbhk->dbhv', state, r_t)
        return state, y_t

    state0 = jnp.zeros((2, b, RW_HEADS, RW_HEAD, RW_HEAD), jnp.float32)
    _, ys = lax.scan(step, state0, xs)
    y = ys[:, 0] + jnp.flip(ys[:, 1], 0)
    y = jnp.moveaxis(y, 0, 1)
    mean = jnp.mean(y, axis=-1, keepdims=True)
    var = jnp.mean(jnp.square(y - mean), axis=-1, keepdims=True)
    y = ((y - mean) * lax.rsqrt(var + LN_X_EPS)).reshape(b, s, RW_WIDTH)
    y = (y * ln_x_g.astype(jnp.float32) + ln_x_b.astype(jnp.float32)).astype(u.dtype)
    bonus = (jnp.sum(rh * kh * r_k, axis=-1, keepdims=True) * vh).reshape(b, s, RW_WIDTH)
    return (y + bonus) * g


def hierarchical_moe(h, w_router_group, b_router_group, w_router_expert, b_router_expert,
                     w_expert_gate, w_expert_up, w_expert_down):
    b, s, d = h.shape
    n_tok = b * s
    blk = math.gcd(n_tok, MOE_BLOCK)
    tokens = h.reshape(n_tok // blk, blk, d)

    def block(tb):
        gl = (tb @ w_router_group).astype(jnp.float32) + b_router_group.astype(jnp.float32)
        gp = jax.nn.softmax(gl, axis=-1)
        _, g_idx = lax.top_k(gl, 1)
        g_onehot = jax.nn.one_hot(g_idx[:, 0], N_GROUPS, dtype=jnp.float32)
        p_g = jnp.sum(gp * g_onehot, axis=-1, keepdims=True)
        el = ((tb @ w_router_expert).astype(jnp.float32) + b_router_expert.astype(jnp.float32))
        el = el.reshape(-1, N_GROUPS, EXPERTS_PER_GROUP)
        el_sel = jnp.einsum('nge,ng->ne', el, g_onehot)
        top_v, top_i = lax.top_k(el_sel, EXPERT_TOP_K)
        top_w = jax.nn.softmax(top_v, axis=-1) * p_g
        gate_e = jnp.sum(jax.nn.one_hot(top_i, EXPERTS_PER_GROUP, dtype=jnp.float32) * top_w[..., None], axis=1)
        gate = (g_onehot[:, :, None] * gate_e[:, None, :]).reshape(-1, N_EXPERTS).astype(tb.dtype)
        hg = jnp.einsum('nd,edf->nef', tb, w_expert_gate)
        hu = jnp.einsum('nd,edf->nef', tb, w_expert_up)
        act = jax.nn.silu(hg) * hu * gate[..., None]
        return jnp.einsum('nef,efd->nd', act, w_expert_down)

    return lax.map(block, tokens).reshape(b, s, d)


def encoder_layer(x, norm1_g, w_in, q_a_norm_g, w_uq, kv_a_norm_g, w_ukv, q_norm_g, k_norm_g,
                  mu_shift, w0_f, w_up_f, w0_b, w_up_b, a0, a_up, g_up, k_k, k_a, r_k, ln_x_g, ln_x_b,
                  w_out, norm2_g, w_router_group, b_router_group, w_router_expert, b_router_expert,
                  w_expert_gate, w_expert_up, w_expert_down):
    h = rmsnorm(x, norm1_g)
    z = h @ w_in
    c_q, c_kv, k_rope, u = jnp.split(z, [Q_LORA, Q_LORA + KV_LORA, MLA_COLS], axis=-1)
    o_a = mla_group(c_q, c_kv, k_rope, q_a_norm_g, w_uq, kv_a_norm_g, w_ukv, q_norm_g, k_norm_g)
    o_b = rwkv7_group(u, mu_shift, w0_f, w_up_f, w0_b, w_up_b, a0, a_up, g_up, k_k, k_a, r_k, ln_x_g, ln_x_b)
    x = x + jnp.concatenate([o_a, o_b], axis=-1) @ w_out
    x = x + hierarchical_moe(rmsnorm(x, norm2_g), w_router_group, b_router_group, w_router_expert,
                             b_router_expert, w_expert_gate, w_expert_up, w_expert_down)
    return x


def setup_inputs(seed: int = 0) -> dict:
    key = jax.random.key(seed)
    ks = iter(jax.random.split(key, 40))

    def nrm(shape, scale, loc=0.0):
        return loc + scale * jax.random.normal(next(ks), (DEPTH,) + shape, jnp.float32)

    return {
        'x_prompt': jax.random.normal(next(ks), (BATCH, SEQ, D_MODEL), jnp.float32),
        'x_sample': jax.random.normal(next(ks), (DEC_BATCH, DEC_SEQ, D_MODEL), jnp.float32),
        'norm1_g': nrm((D_MODEL,), 0.02, 1.0),
        'w_in': nrm((D_MODEL, D_IN), D_MODEL ** -0.5),
        'q_a_norm_g': nrm((Q_LORA,), 0.02, 1.0),
        'w_uq': nrm((Q_LORA, MLA_HEADS * QK_HEAD), Q_LORA ** -0.5),
        'kv_a_norm_g': nrm((KV_LORA,), 0.02, 1.0),
        'w_ukv': nrm((KV_LORA, MLA_HEADS * (QK_NOPE + V_HEAD)), KV_LORA ** -0.5),
        'q_norm_g': nrm((QK_HEAD,), 0.02, 1.0),
        'k_norm_g': nrm((QK_HEAD,), 0.02, 1.0),
        'mu_shift': jax.random.uniform(next(ks), (DEPTH, RW_COLS), jnp.float32),
        'w0_f': nrm((RW_WIDTH,), 0.5, -1.0),
        'w_up_f': nrm((DECAY_LORA, RW_WIDTH), 0.1 * DECAY_LORA ** -0.5),
        'w0_b': nrm((RW_WIDTH,), 0.5, -1.0),
        'w_up_b': nrm((DECAY_LORA, RW_WIDTH), 0.1 * DECAY_LORA ** -0.5),
        'a0': nrm((RW_WIDTH,), 0.1),
        'a_up': nrm((A_LORA, RW_WIDTH), 0.1 * A_LORA ** -0.5),
        'g_up': nrm((GATE_LORA, RW_WIDTH), GATE_LORA ** -0.5),
        'k_k': nrm((RW_WIDTH,), 0.05, 0.85),
        'k_a': nrm((RW_WIDTH,), 0.05, 1.0),
        'r_k': nrm((RW_HEADS, RW_HEAD), 0.1),
        'ln_x_g': nrm((RW_WIDTH,), 0.02, 1.0),
        'ln_x_b': nrm((RW_WIDTH,), 0.02),
        'w_out': nrm((MIX_WIDTH, D_MODEL), MIX_WIDTH ** -0.5),
        'norm2_g': nrm((D_MODEL,), 0.02, 1.0),
        'w_router_group': nrm((D_MODEL, N_GROUPS), D_MODEL ** -0.5),
        'b_router_group': nrm((N_GROUPS,), 0.01),
        'w_router_expert': nrm((D_MODEL, N_EXPERTS), D_MODEL ** -0.5),
        'b_router_expert': nrm((N_EXPERTS,), 0.01),
        'w_expert_gate': nrm((N_EXPERTS, D_MODEL, D_EXPERT), D_MODEL ** -0.5),
        'w_expert_up': nrm((N_EXPERTS, D_MODEL, D_EXPERT), D_MODEL ** -0.5),
        'w_expert_down': nrm((N_EXPERTS, D_EXPERT, D_MODEL), D_EXPERT ** -0.5),
    }


def reference(x_prompt, x_sample, norm1_g, w_in, q_a_norm_g, w_uq, kv_a_norm_g, w_ukv, q_norm_g, k_norm_g,
              mu_shift, w0_f, w_up_f, w0_b, w_up_b, a0, a_up, g_up, k_k, k_a, r_k, ln_x_g, ln_x_b,
              w_out, norm2_g, w_router_group, b_router_group, w_router_expert, b_router_expert,
              w_expert_gate, w_expert_up, w_expert_down):
    layer_params = (norm1_g, w_in, q_a_norm_g, w_uq, kv_a_norm_g, w_ukv, q_norm_g, k_norm_g,
                    mu_shift, w0_f, w_up_f, w0_b, w_up_b, a0, a_up, g_up, k_k, k_a, r_k, ln_x_g, ln_x_b,
                    w_out, norm2_g, w_router_group, b_router_group, w_router_expert, b_router_expert,
                    w_expert_gate, w_expert_up, w_expert_down)
    y_prompt = x_prompt
    y_sample = x_sample
    for layer in range(DEPTH):
        p = [w[layer] for w in layer_params]
        y_prompt = encoder_layer(y_prompt, *p)
        y_sample = encoder_layer(y_sample, *p)
    return (y_prompt, y_sample)
```

```python
import functools
import math

import jax
import jax.numpy as jnp
from jax import lax
from jax.experimental import pallas as pl
from jax.experimental.pallas import tpu as pltpu

D_MODEL = 1024
MLA_HEADS = 8
QK_NOPE = 64
QK_ROPE = 32
QK_HEAD = QK_NOPE + QK_ROPE
V_HEAD = 64
Q_LORA = 384
KV_LORA = 256
ROPE_THETA = 10000.0
RW_HEADS = 8
RW_HEAD = 64
RW_WIDTH = RW_HEADS * RW_HEAD
DECAY_LORA = 64
A_LORA = 64
GATE_LORA = 128
LN_X_EPS = 6.4e-4
MLA_WIDTH = MLA_HEADS * V_HEAD
N_GROUPS = 4
EXPERTS_PER_GROUP = 8
N_EXPERTS = N_GROUPS * EXPERTS_PER_GROUP
D_EXPERT = 256
RMS_EPS = 1e-6

LANE = 128
HEAD_PAD = LANE
MLA_PAD = MLA_HEADS * HEAD_PAD
RW_COLS_PAD = 3 * RW_WIDTH + 3 * LANE
MLA_COLS_PAD = Q_LORA + KV_LORA + 2 * LANE
D_IN_PAD = MLA_COLS_PAD + RW_COLS_PAD
CHUNK = 64
HG = 4
HGW = HG * RW_HEAD
VMEM_LIMIT = 56 * 1024 * 1024

F32 = jnp.float32
BF16 = jnp.bfloat16


def _dot(a, b):
    return jnp.dot(a, b, preferred_element_type=F32)


def _split_dot(a, b_exact):
    hi = a.astype(BF16)
    lo = (a - hi.astype(F32)).astype(BF16)
    return _dot(hi, b_exact) + _dot(lo, b_exact)


def _cparams(sem):
    return pltpu.CompilerParams(dimension_semantics=sem, vmem_limit_bytes=VMEM_LIMIT)


def _proj_kernel(x_ref, g1_ref, win_ref, qag_ref, wuq_ref, kvag_ref, wukv_ref, gq_ref, gk_ref,
                 ct_ref, st_ref, q_out, k_out, v_out, u_out):
    x = x_ref[...]
    h = x * lax.rsqrt(jnp.mean(x * x, axis=-1, keepdims=True) + RMS_EPS) * g1_ref[...]
    z = _dot(h.astype(BF16), win_ref[...])
    u_out[...] = z[:, MLA_COLS_PAD:]

    cq = z[:, :Q_LORA]
    cqn = cq * lax.rsqrt(jnp.mean(cq * cq, axis=-1, keepdims=True) + RMS_EPS) * qag_ref[...]
    q2 = _dot(cqn.astype(BF16), wuq_ref[...])
    ckv = z[:, Q_LORA:Q_LORA + KV_LORA]
    ckvn = ckv * lax.rsqrt(jnp.mean(ckv * ckv, axis=-1, keepdims=True) + RMS_EPS) * kvag_ref[...]
    kv2 = _dot(ckvn.astype(BF16), wukv_ref[...])
    kr_main = z[:, Q_LORA + KV_LORA:Q_LORA + KV_LORA + LANE]
    kr_swap = z[:, Q_LORA + KV_LORA + LANE:MLA_COLS_PAD]

    ct = ct_ref[...]
    st = st_ref[...]
    gq = gq_ref[...]
    gk = gk_ref[...]
    scale = QK_HEAD ** -0.5
    lane = lax.broadcasted_iota(jnp.int32, (1, LANE), 1)
    ones_col = jnp.where(lane == V_HEAD, 1.0, 0.0).astype(F32)
    kr_ssq = jnp.sum(kr_main * kr_main, axis=-1, keepdims=True)
    for hd in range(MLA_HEADS):
        sl = slice(hd * HEAD_PAD, (hd + 1) * HEAD_PAD)
        qm = q2[:, sl]
        qs = q2[:, MLA_PAD + hd * HEAD_PAD:MLA_PAD + (hd + 1) * HEAD_PAD]
        rinv = lax.rsqrt(jnp.sum(qm * qm, axis=-1, keepdims=True) * (1.0 / QK_HEAD) + RMS_EPS)
        qh = (qm * gq[0:1] * ct + qs * gq[1:2] * st) * (rinv * scale)
        q_out[:, sl] = qh.astype(BF16)
        kn = kv2[:, sl]
        rinv_k = lax.rsqrt((jnp.sum(kn * kn, axis=-1, keepdims=True) + kr_ssq) * (1.0 / QK_HEAD) + RMS_EPS)
        kh = ((kn + kr_main) * gk[0:1] * ct + kr_swap * gk[1:2] * st) * rinv_k
        k_out[:, sl] = kh.astype(BF16)
        v_out[:, sl] = (kv2[:, MLA_PAD + hd * HEAD_PAD:MLA_PAD + (hd + 1) * HEAD_PAD] + ones_col).astype(BF16)


def _proj(x2, seq, p, tm):
    n = x2.shape[0]
    tm = min(tm, seq)
    nseq_t = seq // tm
    row = lambda i: (i, 0)
    fixed = lambda i: (0, 0)
    pos = lambda i: (i % nseq_t, 0)
    full = lambda a: pl.BlockSpec(a.shape, fixed)
    return pl.pallas_call(
        _proj_kernel,
        grid=(n // tm,),
        in_specs=[pl.BlockSpec((tm, D_MODEL), row), full(p['g1']), full(p['w_in']), full(p['qag']),
                  full(p['w_uq']), full(p['kvag']), full(p['w_ukv']), full(p['gq']), full(p['gk']),
                  pl.BlockSpec((tm, LANE), pos), pl.BlockSpec((tm, LANE), pos)],
        out_specs=[pl.BlockSpec((tm, MLA_PAD), row), pl.BlockSpec((tm, MLA_PAD), row),
                   pl.BlockSpec((tm, MLA_PAD), row), pl.BlockSpec((tm, RW_COLS_PAD), row)],
        out_shape=[jax.ShapeDtypeStruct((n, MLA_PAD), BF16), jax.ShapeDtypeStruct((n, MLA_PAD), BF16),
                   jax.ShapeDtypeStruct((n, MLA_PAD), BF16), jax.ShapeDtypeStruct((n, RW_COLS_PAD), F32)],
        compiler_params=_cparams(("parallel",)),
        name="proj",
    )(x2, p['g1'], p['w_in'], p['qag'], p['w_uq'], p['kvag'], p['w_ukv'], p['gq'], p['gk'],
      p['ct'][:seq], p['st'][:seq])


def _attn_kernel(q_ref, k_ref, v_ref, o_ref, m_ref, acc_ref):
    j = pl.program_id(3)

    @pl.when(j == 0)
    def _():
        m_ref[...] = jnp.full(m_ref.shape, -jnp.inf, F32)
        acc_ref[...] = jnp.zeros(acc_ref.shape, F32)

    for hh in range(2):
        sl = slice(hh * HEAD_PAD, (hh + 1) * HEAD_PAD)
        s = lax.dot_general(k_ref[:, sl], q_ref[:, sl], (((1,), (1,)), ((), ())),
                            preferred_element_type=F32)
        m_old = m_ref[hh]
        m_new = jnp.maximum(m_old, jnp.max(s, axis=0, keepdims=True))
        p = jnp.exp(s - m_new)
        alpha = jnp.exp(m_old - m_new)
        pv = lax.dot_general(v_ref[:, sl], p.astype(BF16), (((0,), (0,)), ((), ())),
                             preferred_element_type=F32)
        acc_ref[hh] = alpha * acc_ref[hh] + pv
        m_ref[hh] = m_new

    @pl.when(j == pl.num_programs(3) - 1)
    def _():
        outs = []
        for hh in range(2):
            a = acc_ref[hh]
            outs.append(a[:V_HEAD] / a[V_HEAD:V_HEAD + 1])
        o_ref[...] = jnp.concatenate(outs, axis=0).T.astype(BF16)


def _attn(q, k, v, bsz, seq, tq, tk):
    tq = min(tq, seq)
    tk = min(tk, seq)
    nq, nk = seq // tq, seq // tk
    n = q.shape[0]
    return pl.pallas_call(
        _attn_kernel,
        grid=(bsz, MLA_HEADS // 2, nq, nk),
        in_specs=[pl.BlockSpec((tq, 2 * HEAD_PAD), lambda b, h, i, j: (b * nq + i, h)),
                  pl.BlockSpec((tk, 2 * HEAD_PAD), lambda b, h, i, j: (b * nk + j, h)),
                  pl.BlockSpec((tk, 2 * HEAD_PAD), lambda b, h, i, j: (b * nk + j, h))],
        out_specs=pl.BlockSpec((tq, 2 * V_HEAD), lambda b, h, i, j: (b * nq + i, h)),
        out_shape=jax.ShapeDtypeStruct((n, MLA_WIDTH), BF16),
        scratch_shapes=[pltpu.VMEM((2, 1, tq), F32), pltpu.VMEM((2, HEAD_PAD, tq), F32)],
        compiler_params=_cparams(("parallel", "parallel", "parallel", "arbitrary")),
        name="attn",
    )(q, k, v)


def _rwprep_kernel(u_ref, up_ref, un_ref, mu_ref, wdec_ref, w0_ref, aup_ref, a0_ref, gup_ref,
                   kk_ref, ka_ref, rk_ref, seg_ref,
                   r_out, k_out, v_out, kkn_out, kka_out, lwf_out, lwb_out, g_out, bonus_out,
                   *, tm, seq):
    i = pl.program_id(0)
    u = u_ref[...]
    first = (i * tm) % seq == 0
    last = ((i + 1) * tm) % seq == 0
    prev_row = jnp.where(first, 0.0, up_ref[0, 7:8, :])
    next_row = jnp.where(last, 0.0, un_ref[0, 0:1, :])
    rows = lax.broadcasted_iota(jnp.int32, (tm, 1), 0)
    prev = jnp.where(rows == 0, prev_row, pltpu.roll(u, 1, 0))
    nxt = jnp.where(rows == tm - 1, next_row, pltpu.roll(u, tm - 1, 0))
    us = u + (0.5 * (prev + nxt) - u) * mu_ref[...]

    r = us[:, :RW_WIDTH]
    k = us[:, RW_WIDTH:2 * RW_WIDTH]
    v = us[:, 2 * RW_WIDTH:3 * RW_WIDTH]
    xw = us[:, 3 * RW_WIDTH:3 * RW_WIDTH + LANE]
    xa = us[:, 3 * RW_WIDTH + LANE:3 * RW_WIDTH + 2 * LANE]
    xg = us[:, 3 * RW_WIDTH + 2 * LANE:]

    wpre = w0_ref[...] + _dot(jnp.tanh(xw).astype(BF16), wdec_ref[...])
    wlog = -(jnp.maximum(-wpre, 0.0) + jnp.log1p(jnp.exp(-jnp.abs(wpre)))) - 0.5
    lw = -jnp.exp(wlog)
    lwf_out[...] = lw[:, :RW_WIDTH]
    lwb_out[...] = lw[:, RW_WIDTH:]
    a = jax.nn.sigmoid(a0_ref[...] + _dot(xa.astype(BF16), aup_ref[...]))
    g_out[...] = _dot(jax.nn.sigmoid(xg).astype(BF16), gup_ref[...])

    seg = seg_ref[...]
    kk = k * kk_ref[...]
    kkn = kk / jnp.maximum(jnp.sqrt(_split_dot(kk * kk, seg)), 1e-12)
    k2 = k * (1.0 + (a - 1.0) * ka_ref[...])
    r_out[...] = r
    k_out[...] = k2
    v_out[...] = v
    kkn_out[...] = kkn
    kka_out[...] = kkn * a
    bonus_out[...] = _split_dot(r * k2 * rk_ref[...], seg) * v


def _rwprep(u, seq, p, tm):
    n = u.shape[0]
    tm = min(tm, seq)
    g8 = tm // 8
    u3 = u.reshape(n // 8, 8, RW_COLS_PAD)
    ngrp = n // 8
    row = lambda i: (i, 0)
    fixed = lambda i: (0, 0)
    full = lambda a: pl.BlockSpec(a.shape, fixed)
    out_sds = jax.ShapeDtypeStruct((n, RW_WIDTH), F32)
    out_spec = pl.BlockSpec((tm, RW_WIDTH), row)
    return pl.pallas_call(
        functools.partial(_rwprep_kernel, tm=tm, seq=seq),
        grid=(n // tm,),
        in_specs=[pl.BlockSpec((tm, RW_COLS_PAD), row),
                  pl.BlockSpec((1, 8, RW_COLS_PAD), lambda i: (jnp.maximum(i * g8 - 1, 0), 0, 0)),
                  pl.BlockSpec((1, 8, RW_COLS_PAD), lambda i: (jnp.minimum((i + 1) * g8, ngrp - 1), 0, 0)),
                  full(p['mu']), full(p['w_dec']), full(p['w0']), full(p['a_up']), full(p['a0']),
                  full(p['g_up']), full(p['k_k']), full(p['k_a']), full(p['r_k']), full(p['seg'])],
        out_specs=[out_spec] * 9,
        out_shape=[out_sds] * 9,
        compiler_params=_cparams(("parallel",)),
        name="rwprep",
    )(u, u3, u3, p['mu'], p['w_dec'], p['w0'], p['a_up'], p['a0'], p['g_up'], p['k_k'], p['k_a'],
      p['r_k'], p['seg'])


def _bd(x, mask):
    xb = x.astype(BF16)
    return jnp.where(mask, jnp.concatenate([xb] * HG, axis=0), jnp.zeros((), BF16))


def _scan_chunk(r, k, v, kkn, kka, lw, s_bd, reverse, consts):
    tri, strict_m, incl_m, eye_p, bd_m = consts
    lg = _split_dot3(tri, lw)
    lgx = lg - lw
    tot = lg[0:1] if reverse else lg[CHUNK - 1:CHUNK]
    gi = jnp.exp(lg)
    ginv = jnp.exp(-lg)
    gend = jnp.exp(tot - lg)
    at = -kkn * jnp.exp(lgx)
    rt = r * gi
    bt = kka * ginv
    kt = k * ginv
    bh = kka * gend
    kh = k * gend

    lhs = jnp.concatenate([at, rt], axis=0).astype(BF16)
    rhs = jnp.concatenate([_bd(bt, bd_m), _bd(kt, bd_m)], axis=0)
    a_all = lax.dot_general(lhs, rhs, (((1,), (1,)), ((), ())), preferred_element_type=F32)
    zero = jnp.zeros((), F32)
    n_ab = jnp.where(strict_m, a_all[:CHUNK, :HGW], zero)
    a_ak = jnp.where(strict_m, a_all[:CHUNK, HGW:], zero)
    a_rb = jnp.where(incl_m, a_all[CHUNK:, :HGW], zero)
    a_rk = jnp.where(incl_m, a_all[CHUNK:, HGW:], zero)

    t = eye_p + n_ab
    nk = _dot(n_ab.astype(BF16), _bd(n_ab, bd_m))
    for _ in range(4):
        both = _dot(jnp.concatenate([nk, t], axis=0).astype(BF16), _bd(nk, bd_m))
        nk = both[:CHUNK]
        t = t + both[CHUNK:]
    t = t + _dot(t.astype(BF16), _bd(nk, bd_m))

    tb = t.astype(BF16)
    vbd = _bd(v, bd_m)
    w = _dot(tb, _bd(at, bd_m))
    akv = _dot(a_ak.astype(BF16), vbd)
    uv = _dot(tb, _bd(akv, bd_m))
    arb = a_rb.astype(BF16)
    qp = rt + _dot(arb, _bd(w, bd_m))
    y_in = _dot(jnp.concatenate([arb, a_rk.astype(BF16)], axis=1),
                jnp.concatenate([_bd(uv, bd_m), vbd], axis=0))

    lhs_t = jnp.concatenate([bh, kh], axis=0).astype(BF16)
    rhs_t = jnp.concatenate([jnp.concatenate([w, uv], axis=1),
                             jnp.concatenate([jnp.zeros_like(v), v], axis=1)], axis=0).astype(BF16)
    mc = lax.dot_general(lhs_t, rhs_t, (((0,), (0,)), ((), ())), preferred_element_type=F32)
    rows = lax.broadcasted_iota(jnp.int32, (HGW, HGW), 0)
    cols = lax.broadcasted_iota(jnp.int32, (HGW, HGW), 1)
    gtot = jnp.exp(tot)
    m_bd = jnp.where(bd_m, mc[:, :HGW], zero) + jnp.where(rows == cols, gtot, zero)
    c_bd = jnp.where(bd_m, mc[:, HGW:], zero)

    sb = s_bd.astype(BF16)
    y = y_in + _dot(qp.astype(BF16), sb)
    s_new = _dot(m_bd.astype(BF16), sb) + c_bd
    return y, s_new


def _split_dot3(b_exact, a):
    hi = a.astype(BF16)
    r1 = a - hi.astype(F32)
    mid = r1.astype(BF16)
    lo = (r1 - mid.astype(F32)).astype(BF16)
    return _dot(b_exact, hi) + _dot(b_exact, mid) + _dot(b_exact, lo)


def _rwscan_kernel(rf, kf, vf, nf, af, lf, rb, kb, vb, nb, ab, lb, yf_out, yb_out, s_ref):
    c = pl.program_id(1)

    @pl.when(c == 0)
    def _():
        s_ref[...] = jnp.zeros(s_ref.shape, F32)

    ti = lax.broadcasted_iota(jnp.int32, (CHUNK, CHUNK), 0)
    tj = lax.broadcasted_iota(jnp.int32, (CHUNK, CHUNK), 1)
    pt = lax.broadcasted_iota(jnp.int32, (CHUNK, HGW), 0)
    ps = lax.broadcasted_iota(jnp.int32, (CHUNK, HGW), 1) & (CHUNK - 1)
    br = lax.broadcasted_iota(jnp.int32, (HGW, HGW), 0) >> 6
    bc = lax.broadcasted_iota(jnp.int32, (HGW, HGW), 1) >> 6
    bd_m = br == bc
    eye_p = jnp.where(pt == ps, 1.0, 0.0).astype(F32)
    for d, refs, y_out in ((0, (rf, kf, vf, nf, af, lf), yf_out), (1, (rb, kb, vb, nb, ab, lb), yb_out)):
        reverse = d == 1
        if reverse:
            consts = (jnp.where(tj >= ti, 1.0, 0.0).astype(BF16), ps > pt, ps >= pt, eye_p, bd_m)
        else:
            consts = (jnp.where(tj <= ti, 1.0, 0.0).astype(BF16), ps < pt, ps <= pt, eye_p, bd_m)
        for g in range(RW_HEADS // HG):
            sl = slice(g * HGW, (g + 1) * HGW)
            y, s_new = _scan_chunk(*[x[:, sl] for x in refs], s_ref[d, g], reverse, consts)
            y_out[:, sl] = y
            s_ref[d, g] = s_new


def _rwscan(r, k, v, kkn, kka, lwf, lwb, bsz, seq):
    n = r.shape[0]
    nc = seq // CHUNK
    fwd = pl.BlockSpec((CHUNK, RW_WIDTH), lambda b, c: (b * nc + c, 0))
    bwd = pl.BlockSpec((CHUNK, RW_WIDTH), lambda b, c: (b * nc + nc - 1 - c, 0))
    sds = jax.ShapeDtypeStruct((n, RW_WIDTH), F32)
    return pl.pallas_call(
        _rwscan_kernel,
        grid=(bsz, nc),
        in_specs=[fwd] * 6 + [bwd] * 6,
        out_specs=[fwd, bwd],
        out_shape=[sds, sds],
        scratch_shapes=[pltpu.VMEM((2, RW_HEADS // HG, HGW, HGW), F32)],
        compiler_params=_cparams(("parallel", "arbitrary")),
        name="rwscan",
    )(r, k, v, kkn, kka, lwf, r, k, v, kkn, kka, lwb)


def _post_kernel(x_ref, oa_ref, yf_ref, yb_ref, bonus_ref, g_ref, lng_ref, lnb_ref, seg_ref, wout_ref,
                 g2_ref, wr_ref, br_ref, x1_out, h2_out, gate_out):
    seg = seg_ref[...]
    y = yf_ref[...] + yb_ref[...]
    mean = _split_dot(y, seg) * (1.0 / RW_HEAD)
    dlt = y - mean
    var = _split_dot(dlt * dlt, seg) * (1.0 / RW_HEAD)
    yn = dlt * lax.rsqrt(var + LN_X_EPS) * lng_ref[...] + lnb_ref[...]
    ob = (yn + bonus_ref[...]) * g_ref[...]
    x1 = (x_ref[...] + _dot(oa_ref[...], wout_ref[:MLA_WIDTH, :])
          + _dot(ob.astype(BF16), wout_ref[MLA_WIDTH:, :]))
    x1_out[...] = x1
    h2 = x1 * lax.rsqrt(jnp.mean(x1 * x1, axis=-1, keepdims=True) + RMS_EPS) * g2_ref[...]
    h2b = h2.astype(BF16)
    h2_out[...] = h2b

    logits = _dot(h2b, wr_ref[...]) + br_ref[...]
    lane_i = lax.broadcasted_iota(jnp.int32, logits.shape, 1)
    lane = lane_i.astype(F32)
    neg = jnp.float32(-jnp.inf)
    big = jnp.float32(1e9)
    is_g = (lane_i >= N_EXPERTS) & (lane_i < N_EXPERTS + N_GROUPS)
    gl = jnp.where(is_g, logits, neg)
    gmax = jnp.max(gl, axis=-1, keepdims=True)
    gidx = jnp.min(jnp.where(gl == gmax, lane, big), axis=-1, keepdims=True) - N_EXPERTS
    p_g = 1.0 / jnp.sum(jnp.where(is_g, jnp.exp(gl - gmax), 0.0), axis=-1, keepdims=True)
    in_grp = (lane_i < N_EXPERTS) & ((lane_i >> 3).astype(F32) == gidx)
    el = jnp.where(in_grp, logits, neg)
    v1 = jnp.max(el, axis=-1, keepdims=True)
    i1 = jnp.min(jnp.where(el == v1, lane, big), axis=-1, keepdims=True)
    el2 = jnp.where(lane == i1, neg, el)
    v2 = jnp.max(el2, axis=-1, keepdims=True)
    i2 = jnp.min(jnp.where(el2 == v2, lane, big), axis=-1, keepdims=True)
    e2 = jnp.exp(v2 - v1)
    den = 1.0 + e2
    gate_out[...] = (jnp.where(lane == i1, (1.0 / den) * p_g, 0.0)
                     + jnp.where(lane == i2, (e2 / den) * p_g, 0.0))


def _post(x2, oa, yf, yb, bonus, g, p, tm):
    n = x2.shape[0]
    tm = min(tm, n)
    row = lambda i: (i, 0)
    fixed = lambda i: (0, 0)
    full = lambda a: pl.BlockSpec(a.shape, fixed)
    rw = pl.BlockSpec((tm, RW_WIDTH), row)
    return pl.pallas_call(
        _post_kernel,
        grid=(n // tm,),
        in_specs=[pl.BlockSpec((tm, D_MODEL), row), pl.BlockSpec((tm, MLA_WIDTH), row), rw, rw, rw, rw,
                  full(p['ln_g']), full(p['ln_b']), full(p['seg']), full(p['w_out']), full(p['g2']),
                  full(p['w_r']), full(p['b_r'])],
        out_specs=[pl.BlockSpec((tm, D_MODEL), row), pl.BlockSpec((tm, D_MODEL), row),
                   pl.BlockSpec((tm, LANE), row)],
        out_shape=[jax.ShapeDtypeStruct((n, D_MODEL), F32), jax.ShapeDtypeStruct((n, D_MODEL), BF16),
                   jax.ShapeDtypeStruct((n, LANE), F32)],
        compiler_params=_cparams(("parallel",)),
        name="post",
    )(x2, oa, yf, yb, bonus, g, p['ln_g'], p['ln_b'], p['seg'], p['w_out'], p['g2'], p['w_r'], p['b_r'])


def _moe_kernel(x1_ref, h2_ref, gate_ref, exp_ref, wg_ref, wu_ref, wd_ref, o_ref, acc_ref):
    e = pl.program_id(1)

    @pl.when(e == 0)
    def _():
        acc_ref[...] = x1_ref[...]

    h2 = h2_ref[...]
    hg = _dot(h2, wg_ref[...])
    hu = _dot(h2, wu_ref[...])
    gexp = _split_dot(gate_ref[...], exp_ref[0])
    act = hg * jax.nn.sigmoid(hg) * hu * gexp
    acc_ref[...] += _dot(act.astype(BF16), wd_ref[...])

    @pl.when(e == pl.num_programs(1) - 1)
    def _():
        o_ref[...] = acc_ref[...]


def _moe(x1, h2, gate, p, tm, th):
    n = x1.shape[0]
    tm = min(tm, n)
    nh = (N_EXPERTS * D_EXPERT) // th
    return pl.pallas_call(
        _moe_kernel,
        grid=(n // tm, nh),
        in_specs=[pl.BlockSpec((tm, D_MODEL), lambda i, e: (i, 0)),
                  pl.BlockSpec((tm, D_MODEL), lambda i, e: (i, 0)),
                  pl.BlockSpec((tm, LANE), lambda i, e: (i, 0)),
                  pl.BlockSpec((1, LANE, th), lambda i, e: (e, 0, 0)),
                  pl.BlockSpec((D_MODEL, th), lambda i, e: (0, e)),
                  pl.BlockSpec((D_MODEL, th), lambda i, e: (0, e)),
                  pl.BlockSpec((th, D_MODEL), lambda i, e: (e, 0))],
        out_specs=pl.BlockSpec((tm, D_MODEL), lambda i, e: (i, 0)),
        out_shape=jax.ShapeDtypeStruct((n, D_MODEL), F32),
        scratch_shapes=[pltpu.VMEM((tm, D_MODEL), F32)],
        compiler_params=_cparams(("parallel", "arbitrary")),
        name="moe",
    )(x1, h2, gate, p['expand'], p['wg'], p['wu'], p['wd'])


def _head_pad_cols(w, head_w, lo, hi, dst, width=HEAD_PAD, perm=None):
    rows = w.shape[0]
    wh = w.reshape(rows, -1, head_w)[:, :, lo:hi]
    if perm is not None:
        wh = wh[:, :, perm]
    out = jnp.zeros((rows, wh.shape[1], width), w.dtype)
    return out.at[:, :, dst:dst + (hi - lo)].set(wh).reshape(rows, -1)


def _pack_params(lp, max_seq, th):
    (norm1_g, w_in, q_a_norm_g, w_uq, kv_a_norm_g, w_ukv, q_norm_g, k_norm_g, mu_shift, w0_f, w_up_f,
     w0_b, w_up_b, a0, a_up, g_up, k_k, k_a, r_k, ln_x_g, ln_x_b, w_out, norm2_g, w_router_group,
     b_router_group, w_router_expert, b_router_expert, w_expert_gate, w_expert_up, w_expert_down) = lp
    half = QK_ROPE // 2
    swap = jnp.concatenate([jnp.arange(half, QK_ROPE), jnp.arange(half)])
    zcol = lambda rows, n: jnp.zeros((rows, n), F32)
    p = {}
    c0 = Q_LORA + KV_LORA
    w_kr = w_in[:, c0:c0 + QK_ROPE]
    u0 = c0 + QK_ROPE
    wu_ = w_in[:, u0:]
    d = D_MODEL
    w_in_p = jnp.concatenate([
        w_in[:, :c0],
        zcol(d, QK_NOPE), w_kr, zcol(d, LANE - QK_HEAD),
        zcol(d, QK_NOPE), w_kr[:, swap], zcol(d, LANE - QK_HEAD),
        wu_[:, :3 * RW_WIDTH + 2 * DECAY_LORA + A_LORA], zcol(d, LANE - A_LORA),
        wu_[:, 3 * RW_WIDTH + 2 * DECAY_LORA + A_LORA:]], axis=1)
    p['w_in'] = w_in_p.astype(BF16)
    mu = mu_shift[None, :]
    p['mu'] = jnp.concatenate([mu[:, :3 * RW_WIDTH + 2 * DECAY_LORA + A_LORA], zcol(1, LANE - A_LORA),
                               mu[:, 3 * RW_WIDTH + 2 * DECAY_LORA + A_LORA:]], axis=1)
    p['g1'] = norm1_g[None, :]
    p['qag'] = q_a_norm_g[None, :]
    p['kvag'] = kv_a_norm_g[None, :]
    p['w_uq'] = jnp.concatenate([
        _head_pad_cols(w_uq, QK_HEAD, 0, QK_HEAD, 0),
        _head_pad_cols(w_uq, QK_HEAD, QK_NOPE, QK_HEAD, QK_NOPE, perm=swap)], axis=1).astype(BF16)
    p['w_ukv'] = jnp.concatenate([
        _head_pad_cols(w_ukv, QK_NOPE + V_HEAD, 0, QK_NOPE, 0),
        _head_pad_cols(w_ukv, QK_NOPE + V_HEAD, QK_NOPE, QK_NOPE + V_HEAD, 0)], axis=1).astype(BF16)

    def gain_rows(g):
        main = jnp.concatenate([g, jnp.zeros((LANE - QK_HEAD,), F32)])
        swp = jnp.concatenate([jnp.zeros((QK_NOPE,), F32), g[QK_NOPE:][swap], jnp.zeros((LANE - QK_HEAD,), F32)])
        return jnp.stack([main, swp])
    p['gq'] = gain_rows(q_norm_g)
    p['gk'] = gain_rows(k_norm_g)
    inv_freq = 1.0 / (ROPE_THETA ** (jnp.arange(half, dtype=F32) / half))
    ang = jnp.arange(max_seq, dtype=F32)[:, None] * inv_freq[None, :]
    cos, sin = jnp.cos(ang), jnp.sin(ang)
    zpad = jnp.zeros((max_seq, LANE - QK_HEAD), F32)
    p['ct'] = jnp.concatenate([jnp.ones((max_seq, QK_NOPE), F32), cos, cos, zpad], axis=1)
    p['st'] = jnp.concatenate([jnp.zeros((max_seq, QK_NOPE), F32), -sin, sin, zpad], axis=1)

    zl = jnp.zeros((DECAY_LORA, RW_WIDTH), F32)
    p['w_dec'] = jnp.concatenate([jnp.concatenate([w_up_f, zl], axis=1),
                                  jnp.concatenate([zl, w_up_b], axis=1)], axis=0).astype(BF16)
    p['w0'] = jnp.concatenate([w0_f, w0_b])[None, :]
    p['a_up'] = jnp.concatenate([a_up, jnp.zeros((LANE - A_LORA, RW_WIDTH), F32)], axis=0).astype(BF16)
    p['a0'] = a0[None, :]
    p['g_up'] = g_up.astype(BF16)
    p['k_k'] = k_k[None, :]
    p['k_a'] = k_a[None, :]
    p['r_k'] = r_k.reshape(1, RW_WIDTH)
    hid = jnp.arange(RW_WIDTH) // RW_HEAD
    p['seg'] = (hid[:, None] == hid[None, :]).astype(BF16)
    p['ln_g'] = ln_x_g[None, :]
    p['ln_b'] = ln_x_b[None, :]
    p['w_out'] = w_out.astype(BF16)
    p['g2'] = norm2_g[None, :]
    p['w_r'] = jnp.concatenate([w_router_expert, w_router_group,
                                zcol(d, LANE - N_EXPERTS - N_GROUPS)], axis=1).astype(BF16)
    p['b_r'] = jnp.concatenate([b_router_expert, b_router_group,
                                jnp.zeros((LANE - N_EXPERTS - N_GROUPS,), F32)])[None, :]
    hidden = N_EXPERTS * D_EXPERT
    p['wg'] = jnp.transpose(w_expert_gate, (1, 0, 2)).reshape(d, hidden).astype(BF16)
    p['wu'] = jnp.transpose(w_expert_up, (1, 0, 2)).reshape(d, hidden).astype(BF16)
    p['wd'] = w_expert_down.reshape(hidden, d).astype(BF16)
    unit_e = jnp.arange(hidden) // D_EXPERT
    expand = (jnp.arange(LANE)[:, None] == unit_e[None, :]).astype(BF16)
    p['expand'] = expand.reshape(LANE, hidden // th, th).transpose(1, 0, 2)
    return p


TM_PROJ = 512
TQ = 1024
TK = 1024
TM_PREP = 512
TM_POST = 512
TM_MOE = 1024
TH_MOE = 1024


def _layer(x, p):
    bsz, seq, d = x.shape
    x2 = x.reshape(bsz * seq, d)
    q, k, v, u = _proj(x2, seq, p, TM_PROJ)
    oa = _attn(q, k, v, bsz, seq, TQ, TK)
    r, k2, vv, kkn, kka, lwf, lwb, g, bonus = _rwprep(u, seq, p, TM_PREP)
    yf, yb = _rwscan(r, k2, vv, kkn, kka, lwf, lwb, bsz, seq)
    x1, h2, gate = _post(x2, oa, yf, yb, bonus, g, p, TM_POST)
    out = _moe(x1, h2, gate, p, TM_MOE, TH_MOE)
    return out.reshape(bsz, seq, d)


def kernel(x_prompt, x_sample, norm1_g, w_in, q_a_norm_g, w_uq, kv_a_norm_g, w_ukv, q_norm_g, k_norm_g, mu_shift, w0_f, w_up_f, w0_b, w_up_b, a0, a_up, g_up, k_k, k_a, r_k, ln_x_g, ln_x_b, w_out, norm2_g, w_router_group, b_router_group, w_router_expert, b_router_expert, w_expert_gate, w_expert_up, w_expert_down):
    layer_params = (norm1_g, w_in, q_a_norm_g, w_uq, kv_a_norm_g, w_ukv, q_norm_g, k_norm_g, mu_shift,
                    w0_f, w_up_f, w0_b, w_up_b, a0, a_up, g_up, k_k, k_a, r_k, ln_x_g, ln_x_b, w_out,
                    norm2_g, w_router_group, b_router_group, w_router_expert, b_router_expert,
                    w_expert_gate, w_expert_up, w_expert_down)
    y_prompt, y_sample = x_prompt, x_sample
    max_seq = max(x_prompt.shape[1], x_sample.shape[1])
    for layer in range(norm1_g.shape[0]):
        p = _pack_params([w[layer] for w in layer_params], max_seq, TH_MOE)
        y_prompt = _layer(y_prompt, p)
        y_sample = _layer(y_sample, p)
    return (y_prompt, y_sample)
```

```python
import functools
import math

import jax
import jax.numpy as jnp
from jax import lax
from jax.experimental import pallas as pl
from jax.experimental.pallas import tpu as pltpu

D_MODEL = 1024
MLA_HEADS = 8
QK_NOPE = 64
QK_ROPE = 32
QK_HEAD = QK_NOPE + QK_ROPE
V_HEAD = 64
Q_LORA = 384
KV_LORA = 256
ROPE_THETA = 10000.0
RW_HEADS = 8
RW_HEAD = 64
RW_WIDTH = RW_HEADS * RW_HEAD
DECAY_LORA = 64
A_LORA = 64
GATE_LORA = 128
LN_X_EPS = 6.4e-4
MLA_WIDTH = MLA_HEADS * V_HEAD
N_GROUPS = 4
EXPERTS_PER_GROUP = 8
N_EXPERTS = N_GROUPS * EXPERTS_PER_GROUP
D_EXPERT = 256
RMS_EPS = 1e-6

LANE = 128
HEAD_PAD = LANE
MLA_PAD = MLA_HEADS * HEAD_PAD
RW_COLS_PAD = 3 * RW_WIDTH + 3 * LANE
MLA_COLS_PAD = Q_LORA + KV_LORA + 2 * LANE
D_IN_PAD = MLA_COLS_PAD + RW_COLS_PAD
CHUNK = 64
HG = 4
HGW = HG * RW_HEAD
VMEM_LIMIT = 56 * 1024 * 1024
LOG2E = 1.4426950408889634
NCH = 4
FAST_SOFTMAX_MAX_SHIFT = 100.0

F32 = jnp.float32
BF16 = jnp.bfloat16


def _dot(a, b):
    return jnp.dot(a, b, preferred_element_type=F32)


def _split_dot(a, b_exact):
    hi = a.astype(BF16)
    lo = (a - hi.astype(F32)).astype(BF16)
    return _dot(hi, b_exact) + _dot(lo, b_exact)


def _cparams(sem):
    return pltpu.CompilerParams(dimension_semantics=sem, vmem_limit_bytes=VMEM_LIMIT)


def _proj_kernel(x_ref, g1_ref, win_ref, qag_ref, wuq_ref, kvag_ref, wukv_ref, gq_ref, gk_ref,
                 ct_ref, st_ref, q_out, k_out, v_out, u_out):
    x = x_ref[...]
    h = x * lax.rsqrt(jnp.mean(x * x, axis=-1, keepdims=True) + RMS_EPS) * g1_ref[...]
    z = _dot(h.astype(BF16), win_ref[...])
    u_out[...] = z[:, MLA_COLS_PAD:]

    cq = z[:, :Q_LORA]
    cqn = cq * lax.rsqrt(jnp.mean(cq * cq, axis=-1, keepdims=True) + RMS_EPS) * qag_ref[...]
    q2 = _dot(cqn.astype(BF16), wuq_ref[...])
    ckv = z[:, Q_LORA:Q_LORA + KV_LORA]
    ckvn = ckv * lax.rsqrt(jnp.mean(ckv * ckv, axis=-1, keepdims=True) + RMS_EPS) * kvag_ref[...]
    kv2 = _dot(ckvn.astype(BF16), wukv_ref[...])
    kr_main = z[:, Q_LORA + KV_LORA:Q_LORA + KV_LORA + LANE]
    kr_swap = z[:, Q_LORA + KV_LORA + LANE:MLA_COLS_PAD]

    ct = ct_ref[...]
    st = st_ref[...]
    gq = gq_ref[...]
    gk = gk_ref[...]
    scale = QK_HEAD ** -0.5 * LOG2E
    lane = lax.broadcasted_iota(jnp.int32, (1, LANE), 1)
    ones_col = jnp.where(lane == V_HEAD, 1.0, 0.0).astype(F32)
    shift_col = jnp.where(lane == QK_HEAD, 1.0, 0.0).astype(F32)
    q_shift = shift_col * gq[2:3, 0:1]
    kr_ssq = jnp.sum(kr_main * kr_main, axis=-1, keepdims=True)
    for hd in range(MLA_HEADS):
        sl = slice(hd * HEAD_PAD, (hd + 1) * HEAD_PAD)
        qm = q2[:, sl]
        qs = q2[:, MLA_PAD + hd * HEAD_PAD:MLA_PAD + (hd + 1) * HEAD_PAD]
        rinv = lax.rsqrt(jnp.sum(qm * qm, axis=-1, keepdims=True) * (1.0 / QK_HEAD) + RMS_EPS)
        qh = (qm * gq[0:1] * ct + qs * gq[1:2] * st) * (rinv * scale)
        q_out[:, sl] = (qh - q_shift).astype(BF16)
        kn = kv2[:, sl]
        rinv_k = lax.rsqrt((jnp.sum(kn * kn, axis=-1, keepdims=True) + kr_ssq) * (1.0 / QK_HEAD) + RMS_EPS)
        kh = ((kn + kr_main) * gk[0:1] * ct + kr_swap * gk[1:2] * st) * rinv_k
        k_out[:, sl] = (kh + shift_col).astype(BF16)
        v_out[:, sl] = (kv2[:, MLA_PAD + hd * HEAD_PAD:MLA_PAD + (hd + 1) * HEAD_PAD] + ones_col).astype(BF16)


def _proj(x2, seq, p, tm):
    n = x2.shape[0]
    tm = min(tm, seq)
    nseq_t = seq // tm
    row = lambda i: (i, 0)
    fixed = lambda i: (0, 0)
    pos = lambda i: (i % nseq_t, 0)
    full = lambda a: pl.BlockSpec(a.shape, fixed)
    return pl.pallas_call(
        _proj_kernel,
        grid=(n // tm,),
        in_specs=[pl.BlockSpec((tm, D_MODEL), row), full(p['g1']), full(p['w_in']), full(p['qag']),
                  full(p['w_uq']), full(p['kvag']), full(p['w_ukv']), full(p['gq']), full(p['gk']),
                  pl.BlockSpec((tm, LANE), pos), pl.BlockSpec((tm, LANE), pos)],
        out_specs=[pl.BlockSpec((tm, MLA_PAD), row), pl.BlockSpec((tm, MLA_PAD), row),
                   pl.BlockSpec((tm, MLA_PAD), row), pl.BlockSpec((tm, RW_COLS_PAD), row)],
        out_shape=[jax.ShapeDtypeStruct((n, MLA_PAD), BF16), jax.ShapeDtypeStruct((n, MLA_PAD), BF16),
                   jax.ShapeDtypeStruct((n, MLA_PAD), BF16), jax.ShapeDtypeStruct((n, RW_COLS_PAD), F32)],
        compiler_params=_cparams(("parallel",)),
        name="proj",
    )(x2, p['g1'], p['w_in'], p['qag'], p['w_uq'], p['kvag'], p['w_ukv'], p['gq'], p['gk'],
      p['ct'][:seq], p['st'][:seq])


def _attn_kernel(q_ref, k_ref, v_ref, o_ref, m_ref, acc_ref, *, running_max):
    j = pl.program_id(3)

    @pl.when(j == 0)
    def _():
        if running_max:
            m_ref[...] = jnp.full(m_ref.shape, -jnp.inf, F32)
        acc_ref[...] = jnp.zeros(acc_ref.shape, F32)

    heads = [slice(hh * HEAD_PAD, (hh + 1) * HEAD_PAD) for hh in range(2)]
    scores = [lax.dot_general(k_ref[:, sl], q_ref[:, sl], (((1,), (1,)), ((), ())),
                              preferred_element_type=F32) for sl in heads]
    for hh, sl in enumerate(heads):
        s = scores[hh]
        if running_max:
            m_old = m_ref[hh]
            m_new = jnp.maximum(m_old, jnp.max(s, axis=0, keepdims=True))
            p = jnp.exp2(s - m_new)
            m_ref[hh] = m_new
        else:
            p = jnp.exp2(s)
        pv = lax.dot_general(v_ref[:, sl], p.astype(BF16), (((0,), (0,)), ((), ())),
                             preferred_element_type=F32)
        if running_max:
            acc_ref[hh] = jnp.exp2(m_old - m_new) * acc_ref[hh] + pv
        else:
            acc_ref[hh] += pv

    @pl.when(j == pl.num_programs(3) - 1)
    def _():
        outs = []
        for hh in range(2):
            a = acc_ref[hh]
            outs.append(a[:V_HEAD] / a[V_HEAD:V_HEAD + 1])
        o_ref[...] = jnp.concatenate(outs, axis=0).T.astype(BF16)


def _attn(q, k, v, bsz, seq, fast_softmax):
    return lax.cond(fast_softmax,
                    lambda q, k, v: _attn_call(q, k, v, bsz, seq, TQ, TK, False),
                    lambda q, k, v: _attn_call(q, k, v, bsz, seq, TQ_MAX, TK_MAX, True),
                    q, k, v)


def _attn_call(q, k, v, bsz, seq, tq, tk, running_max):
    tq = min(tq, seq)
    tk = min(tk, seq)
    nq, nk = seq // tq, seq // tk
    n = q.shape[0]
    return pl.pallas_call(
        functools.partial(_attn_kernel, running_max=running_max),
        grid=(bsz, MLA_HEADS // 2, nq, nk),
        in_specs=[pl.BlockSpec((tq, 2 * HEAD_PAD), lambda b, h, i, j: (b * nq + i, h)),
                  pl.BlockSpec((tk, 2 * HEAD_PAD), lambda b, h, i, j: (b * nk + j, h)),
                  pl.BlockSpec((tk, 2 * HEAD_PAD), lambda b, h, i, j: (b * nk + j, h))],
        out_specs=pl.BlockSpec((tq, 2 * V_HEAD), lambda b, h, i, j: (b * nq + i, h)),
        out_shape=jax.ShapeDtypeStruct((n, MLA_WIDTH), BF16),
        scratch_shapes=[pltpu.VMEM((2, 1, tq), F32), pltpu.VMEM((2, HEAD_PAD, tq), F32)],
        compiler_params=_cparams(("parallel", "parallel", "parallel", "arbitrary")),
        name="attn_max" if running_max else "attn",
    )(q, k, v)


def _rwprep_kernel(u_ref, up_ref, un_ref, mu_ref, wdec_ref, w0_ref, aup_ref, a0_ref, gup_ref,
                   kk_ref, ka_ref, rk_ref, seg_ref,
                   r_out, k_out, v_out, kkn_out, kka_out, lwf_out, lwb_out, g_out, bonus_out,
                   *, tm, seq):
    i = pl.program_id(0)
    u = u_ref[...]
    first = (i * tm) % seq == 0
    last = ((i + 1) * tm) % seq == 0
    prev_row = jnp.where(first, 0.0, up_ref[0, 7:8, :])
    next_row = jnp.where(last, 0.0, un_ref[0, 0:1, :])
    rows = lax.broadcasted_iota(jnp.int32, (tm, 1), 0)
    prev = jnp.where(rows == 0, prev_row, pltpu.roll(u, 1, 0))
    nxt = jnp.where(rows == tm - 1, next_row, pltpu.roll(u, tm - 1, 0))
    us = u + (0.5 * (prev + nxt) - u) * mu_ref[...]

    r = us[:, :RW_WIDTH]
    k = us[:, RW_WIDTH:2 * RW_WIDTH]
    v = us[:, 2 * RW_WIDTH:3 * RW_WIDTH]
    xw = us[:, 3 * RW_WIDTH:3 * RW_WIDTH + LANE]
    xa = us[:, 3 * RW_WIDTH + LANE:3 * RW_WIDTH + 2 * LANE]
    xg = us[:, 3 * RW_WIDTH + 2 * LANE:]

    wpre = w0_ref[...] + _dot(jnp.tanh(xw).astype(BF16), wdec_ref[...])
    wlog = -(jnp.maximum(-wpre, 0.0) + jnp.log1p(jnp.exp(-jnp.abs(wpre)))) - 0.5
    lw = -jnp.exp(wlog)
    lwf_out[...] = lw[:, :RW_WIDTH]
    lwb_out[...] = lw[:, RW_WIDTH:]
    a = jax.nn.sigmoid(a0_ref[...] + _dot(xa.astype(BF16), aup_ref[...]))
    g_out[...] = _dot(jax.nn.sigmoid(xg).astype(BF16), gup_ref[...])

    seg = seg_ref[...]
    kk = k * kk_ref[...]
    kkn = kk / jnp.maximum(jnp.sqrt(_split_dot(kk * kk, seg)), 1e-12)
    k2 = k * (1.0 + (a - 1.0) * ka_ref[...])
    r_out[...] = r
    k_out[...] = k2
    v_out[...] = v
    kkn_out[...] = kkn
    kka_out[...] = kkn * a
    bonus_out[...] = _split_dot(r * k2 * rk_ref[...], seg) * v


def _rwprep(u, seq, p, tm):
    n = u.shape[0]
    tm = min(tm, seq)
    g8 = tm // 8
    u3 = u.reshape(n // 8, 8, RW_COLS_PAD)
    ngrp = n // 8
    row = lambda i: (i, 0)
    fixed = lambda i: (0, 0)
    full = lambda a: pl.BlockSpec(a.shape, fixed)
    out_sds = jax.ShapeDtypeStruct((n, RW_WIDTH), F32)
    out_spec = pl.BlockSpec((tm, RW_WIDTH), row)
    return pl.pallas_call(
        functools.partial(_rwprep_kernel, tm=tm, seq=seq),
        grid=(n // tm,),
        in_specs=[pl.BlockSpec((tm, RW_COLS_PAD), row),
                  pl.BlockSpec((1, 8, RW_COLS_PAD), lambda i: (jnp.maximum(i * g8 - 1, 0), 0, 0)),
                  pl.BlockSpec((1, 8, RW_COLS_PAD), lambda i: (jnp.minimum((i + 1) * g8, ngrp - 1), 0, 0)),
                  full(p['mu']), full(p['w_dec']), full(p['w0']), full(p['a_up']), full(p['a0']),
                  full(p['g_up']), full(p['k_k']), full(p['k_a']), full(p['r_k']), full(p['seg'])],
        out_specs=[out_spec] * 9,
        out_shape=[out_sds] * 9,
        compiler_params=_cparams(("parallel",)),
        name="rwprep",
    )(u, u3, u3, p['mu'], p['w_dec'], p['w0'], p['a_up'], p['a0'], p['g_up'], p['k_k'], p['k_a'],
      p['r_k'], p['seg'])


def _bd(x, bd16):
    return jnp.concatenate([x.astype(BF16)] * HG, axis=0) * bd16


def _chunk_intra(r, k, v, kkn, kka, lw, reverse, consts):
    tri, strict_m, incl_m, eye_p, bd16, bd32, eye32 = consts
    lg = _split_dot3(tri, lw)
    yield
    lgx = lg - lw
    tot = lg[0:1] if reverse else lg[CHUNK - 1:CHUNK]
    gi = jnp.exp(lg)
    ginv = jnp.exp(-lg)
    gend = jnp.exp(tot - lg)
    at = -kkn * jnp.exp(lgx)
    rt = r * gi
    bt = kka * ginv
    kt = k * ginv
    bh = kka * gend
    kh = k * gend

    lhs = jnp.concatenate([at, rt], axis=0).astype(BF16)
    rhs = jnp.concatenate([_bd(bt, bd16), _bd(kt, bd16)], axis=0)
    a_all = lax.dot_general(lhs, rhs, (((1,), (1,)), ((), ())), preferred_element_type=F32)
    yield
    n_ab = a_all[:CHUNK, :HGW] * strict_m
    a_ak = a_all[:CHUNK, HGW:] * strict_m
    a_rb = a_all[CHUNK:, :HGW] * incl_m
    a_rk = a_all[CHUNK:, HGW:] * incl_m

    t = eye_p + n_ab
    nk = _dot(n_ab.astype(BF16), _bd(n_ab, bd16))
    yield
    for _ in range(4):
        both = _dot(jnp.concatenate([nk, t], axis=0).astype(BF16), _bd(nk, bd16))
        yield
        nk = both[:CHUNK]
        t = t + both[CHUNK:]
    tn = _dot(t.astype(BF16), _bd(nk, bd16))
    yield
    t = t + tn

    tb = t.astype(BF16)
    vbd = _bd(v, bd16)
    w = _dot(tb, _bd(at, bd16))
    akv = _dot(a_ak.astype(BF16), vbd)
    yield
    uv = _dot(tb, _bd(akv, bd16))
    arb = a_rb.astype(BF16)
    qpd = _dot(arb, _bd(w, bd16))
    yield
    y_in = _dot(jnp.concatenate([arb, a_rk.astype(BF16)], axis=1),
                jnp.concatenate([_bd(uv, bd16), vbd], axis=0))
    lhs_t = jnp.concatenate([bh, kh], axis=0).astype(BF16)
    rhs_t = jnp.concatenate([jnp.concatenate([w, uv], axis=1),
                             jnp.concatenate([jnp.zeros_like(v), v], axis=1)], axis=0).astype(BF16)
    mc = lax.dot_general(lhs_t, rhs_t, (((0,), (0,)), ((), ())), preferred_element_type=F32)
    yield
    m_bd = mc[:, :HGW] * bd32 + eye32 * jnp.exp(tot)
    c_bd = mc[:, HGW:] * bd32
    return y_in, (rt + qpd).astype(BF16), m_bd.astype(BF16), c_bd


def _interleave(gens):
    results = [None] * len(gens)
    active = list(range(len(gens)))
    while active:
        still = []
        for i in active:
            try:
                next(gens[i])
                still.append(i)
            except StopIteration as stop:
                results[i] = stop.value
        active = still
    return results


def _split_dot3(b_exact, a):
    hi = a.astype(BF16)
    r1 = a - hi.astype(F32)
    mid = r1.astype(BF16)
    lo = (r1 - mid.astype(F32)).astype(BF16)
    return _dot(b_exact, hi) + _dot(b_exact, mid) + _dot(b_exact, lo)


def _rwscan_kernel(rf, kf, vf, nf, af, lf, rb, kb, vb, nb, ab, lb, tri_ref, m64_ref, bd16_ref, bd32_ref,
                   yf_out, yb_out, s_ref):
    c = pl.program_id(1)

    @pl.when(c == 0)
    def _():
        s_ref[...] = jnp.zeros(s_ref.shape, F32)

    nch = rf.shape[0] // CHUNK
    bd16 = bd16_ref[...]
    bd32 = bd32_ref[0]
    eye32 = bd32_ref[1]
    eye_p = m64_ref[4]
    ngrp = RW_HEADS // HG
    chains = []
    gens = []
    for d, refs, y_out in ((0, (rf, kf, vf, nf, af, lf), yf_out), (1, (rb, kb, vb, nb, ab, lb), yb_out)):
        reverse = d == 1
        consts = (tri_ref[d], m64_ref[2 * d], m64_ref[2 * d + 1], eye_p, bd16, bd32, eye32)
        order = list(range(nch - 1, -1, -1) if reverse else range(nch))
        for g in range(ngrp):
            chains.append((d, g, y_out, order))
            for j in order:
                ops = [x[j * CHUNK:(j + 1) * CHUNK, g * HGW:(g + 1) * HGW] for x in refs]
                gens.append(_chunk_intra(*ops, reverse, consts))
    parts = _interleave(gens)

    states = [s_ref[d, g] for d, g, _, _ in chains]
    for step in range(nch):
        for ci, (d, g, y_out, order) in enumerate(chains):
            j = order[step]
            y_in, qp, m_bd, c_bd = parts[ci * nch + step]
            sb = states[ci].astype(BF16)
            y_out[j * CHUNK:(j + 1) * CHUNK, g * HGW:(g + 1) * HGW] = y_in + _dot(qp, sb)
            states[ci] = _dot(m_bd, sb) + c_bd
    for ci, (d, g, _, _) in enumerate(chains):
        s_ref[d, g] = states[ci]


def _rwscan(r, k, v, kkn, kka, lwf, lwb, bsz, seq, p):
    n = r.shape[0]
    rows = min(NCH * CHUNK, seq)
    nb = seq // rows
    fwd = pl.BlockSpec((rows, RW_WIDTH), lambda b, c: (b * nb + c, 0))
    bwd = pl.BlockSpec((rows, RW_WIDTH), lambda b, c: (b * nb + nb - 1 - c, 0))
    full = lambda a: pl.BlockSpec(a.shape, lambda b, c: (0,) * a.ndim)
    sds = jax.ShapeDtypeStruct((n, RW_WIDTH), F32)
    return pl.pallas_call(
        _rwscan_kernel,
        grid=(bsz, nb),
        in_specs=[fwd] * 6 + [bwd] * 6 + [full(p['sc_tri']), full(p['sc_m64']), full(p['sc_bd16']),
                                           full(p['sc_bd32'])],
        out_specs=[fwd, bwd],
        out_shape=[sds, sds],
        scratch_shapes=[pltpu.VMEM((2, RW_HEADS // HG, HGW, HGW), F32)],
        compiler_params=_cparams(("parallel", "arbitrary")),
        name="rwscan",
    )(r, k, v, kkn, kka, lwf, r, k, v, kkn, kka, lwb, p['sc_tri'], p['sc_m64'], p['sc_bd16'], p['sc_bd32'])


def _post_kernel(x_ref, oa_ref, yf_ref, yb_ref, bonus_ref, g_ref, lng_ref, lnb_ref, seg_ref, wout_ref,
                 g2_ref, wr_ref, br_ref, x1_out, h2_out, gate_out):
    seg = seg_ref[...]
    y = yf_ref[...] + yb_ref[...]
    mean = _split_dot(y, seg) * (1.0 / RW_HEAD)
    dlt = y - mean
    var = _split_dot(dlt * dlt, seg) * (1.0 / RW_HEAD)
    yn = dlt * lax.rsqrt(var + LN_X_EPS) * lng_ref[...] + lnb_ref[...]
    ob = (yn + bonus_ref[...]) * g_ref[...]
    x1 = (x_ref[...] + _dot(oa_ref[...], wout_ref[:MLA_WIDTH, :])
          + _dot(ob.astype(BF16), wout_ref[MLA_WIDTH:, :]))
    x1_out[...] = x1
    h2 = x1 * lax.rsqrt(jnp.mean(x1 * x1, axis=-1, keepdims=True) + RMS_EPS) * g2_ref[...]
    h2b = h2.astype(BF16)
    h2_out[...] = h2b

    logits = _dot(h2b, wr_ref[...]) + br_ref[...]
    lane_i = lax.broadcasted_iota(jnp.int32, logits.shape, 1)
    lane = lane_i.astype(F32)
    neg = jnp.float32(-jnp.inf)
    big = jnp.float32(1e9)
    is_g = (lane_i >= N_EXPERTS) & (lane_i < N_EXPERTS + N_GROUPS)
    gl = jnp.where(is_g, logits, neg)
    gmax = jnp.max(gl, axis=-1, keepdims=True)
    gidx = jnp.min(jnp.where(gl == gmax, lane, big), axis=-1, keepdims=True) - N_EXPERTS
    p_g = 1.0 / jnp.sum(jnp.where(is_g, jnp.exp(gl - gmax), 0.0), axis=-1, keepdims=True)
    in_grp = (lane_i < N_EXPERTS) & ((lane_i >> 3).astype(F32) == gidx)
    el = jnp.where(in_grp, logits, neg)
    v1 = jnp.max(el, axis=-1, keepdims=True)
    i1 = jnp.min(jnp.where(el == v1, lane, big), axis=-1, keepdims=True)
    el2 = jnp.where(lane == i1, neg, el)
    v2 = jnp.max(el2, axis=-1, keepdims=True)
    i2 = jnp.min(jnp.where(el2 == v2, lane, big), axis=-1, keepdims=True)
    e2 = jnp.exp(v2 - v1)
    den = 1.0 + e2
    gate_out[...] = (jnp.where(lane == i1, (1.0 / den) * p_g, 0.0)
                     + jnp.where(lane == i2, (e2 / den) * p_g, 0.0))


def _post(x2, oa, yf, yb, bonus, g, p, tm):
    n = x2.shape[0]
    tm = min(tm, n)
    row = lambda i: (i, 0)
    fixed = lambda i: (0, 0)
    full = lambda a: pl.BlockSpec(a.shape, fixed)
    rw = pl.BlockSpec((tm, RW_WIDTH), row)
    return pl.pallas_call(
        _post_kernel,
        grid=(n // tm,),
        in_specs=[pl.BlockSpec((tm, D_MODEL), row), pl.BlockSpec((tm, MLA_WIDTH), row), rw, rw, rw, rw,
                  full(p['ln_g']), full(p['ln_b']), full(p['seg']), full(p['w_out']), full(p['g2']),
                  full(p['w_r']), full(p['b_r'])],
        out_specs=[pl.BlockSpec((tm, D_MODEL), row), pl.BlockSpec((tm, D_MODEL), row),
                   pl.BlockSpec((tm, LANE), row)],
        out_shape=[jax.ShapeDtypeStruct((n, D_MODEL), F32), jax.ShapeDtypeStruct((n, D_MODEL), BF16),
                   jax.ShapeDtypeStruct((n, LANE), F32)],
        compiler_params=_cparams(("parallel",)),
        name="post",
    )(x2, oa, yf, yb, bonus, g, p['ln_g'], p['ln_b'], p['seg'], p['w_out'], p['g2'], p['w_r'], p['b_r'])


def _moe_kernel(x1_ref, h2_ref, gate_ref, exp_ref, wg_ref, wu_ref, wd_ref, o_ref, acc_ref):
    e = pl.program_id(1)

    @pl.when(e == 0)
    def _():
        acc_ref[...] = x1_ref[...]

    h2 = h2_ref[...]
    hg = _dot(h2, wg_ref[...])
    hu = _dot(h2, wu_ref[...])
    gexp = _split_dot(gate_ref[...], exp_ref[0])
    act = hg * jax.nn.sigmoid(hg) * hu * gexp
    acc_ref[...] += _dot(act.astype(BF16), wd_ref[...])

    @pl.when(e == pl.num_programs(1) - 1)
    def _():
        o_ref[...] = acc_ref[...]


def _moe(x1, h2, gate, p, tm, th):
    n = x1.shape[0]
    tm = min(tm, n)
    nh = (N_EXPERTS * D_EXPERT) // th
    return pl.pallas_call(
        _moe_kernel,
        grid=(n // tm, nh),
        in_specs=[pl.BlockSpec((tm, D_MODEL), lambda i, e: (i, 0)),
                  pl.BlockSpec((tm, D_MODEL), lambda i, e: (i, 0)),
                  pl.BlockSpec((tm, LANE), lambda i, e: (i, 0)),
                  pl.BlockSpec((1, LANE, th), lambda i, e: (e, 0, 0)),
                  pl.BlockSpec((D_MODEL, th), lambda i, e: (0, e)),
                  pl.BlockSpec((D_MODEL, th), lambda i, e: (0, e)),
                  pl.BlockSpec((th, D_MODEL), lambda i, e: (e, 0))],
        out_specs=pl.BlockSpec((tm, D_MODEL), lambda i, e: (i, 0)),
        out_shape=jax.ShapeDtypeStruct((n, D_MODEL), F32),
        scratch_shapes=[pltpu.VMEM((tm, D_MODEL), F32)],
        compiler_params=_cparams(("parallel", "arbitrary")),
        name="moe",
    )(x1, h2, gate, p['expand'], p['wg'], p['wu'], p['wd'])


def _head_pad_cols(w, head_w, lo, hi, dst, width=HEAD_PAD, perm=None):
    rows = w.shape[0]
    wh = w.reshape(rows, -1, head_w)[:, :, lo:hi]
    if perm is not None:
        wh = wh[:, :, perm]
    out = jnp.zeros((rows, wh.shape[1], width), w.dtype)
    return out.at[:, :, dst:dst + (hi - lo)].set(wh).reshape(rows, -1)


def _pack_params(lp, max_seq, th):
    (norm1_g, w_in, q_a_norm_g, w_uq, kv_a_norm_g, w_ukv, q_norm_g, k_norm_g, mu_shift, w0_f, w_up_f,
     w0_b, w_up_b, a0, a_up, g_up, k_k, k_a, r_k, ln_x_g, ln_x_b, w_out, norm2_g, w_router_group,
     b_router_group, w_router_expert, b_router_expert, w_expert_gate, w_expert_up, w_expert_down) = lp
    half = QK_ROPE // 2
    swap = jnp.concatenate([jnp.arange(half, QK_ROPE), jnp.arange(half)])
    zcol = lambda rows, n: jnp.zeros((rows, n), F32)
    p = {}
    c0 = Q_LORA + KV_LORA
    w_kr = w_in[:, c0:c0 + QK_ROPE]
    u0 = c0 + QK_ROPE
    wu_ = w_in[:, u0:]
    d = D_MODEL
    w_in_p = jnp.concatenate([
        w_in[:, :c0],
        zcol(d, QK_NOPE), w_kr, zcol(d, LANE - QK_HEAD),
        zcol(d, QK_NOPE), w_kr[:, swap], zcol(d, LANE - QK_HEAD),
        wu_[:, :3 * RW_WIDTH + 2 * DECAY_LORA + A_LORA], zcol(d, LANE - A_LORA),
        wu_[:, 3 * RW_WIDTH + 2 * DECAY_LORA + A_LORA:]], axis=1)
    p['w_in'] = w_in_p.astype(BF16)
    mu = mu_shift[None, :]
    p['mu'] = jnp.concatenate([mu[:, :3 * RW_WIDTH + 2 * DECAY_LORA + A_LORA], zcol(1, LANE - A_LORA),
                               mu[:, 3 * RW_WIDTH + 2 * DECAY_LORA + A_LORA:]], axis=1)
    p['g1'] = norm1_g[None, :]
    p['qag'] = q_a_norm_g[None, :]
    p['kvag'] = kv_a_norm_g[None, :]
    p['w_uq'] = jnp.concatenate([
        _head_pad_cols(w_uq, QK_HEAD, 0, QK_HEAD, 0),
        _head_pad_cols(w_uq, QK_HEAD, QK_NOPE, QK_HEAD, QK_NOPE, perm=swap)], axis=1).astype(BF16)
    p['w_ukv'] = jnp.concatenate([
        _head_pad_cols(w_ukv, QK_NOPE + V_HEAD, 0, QK_NOPE, 0),
        _head_pad_cols(w_ukv, QK_NOPE + V_HEAD, QK_NOPE, QK_NOPE + V_HEAD, 0)], axis=1).astype(BF16)

    def gain_rows(g):
        main = jnp.concatenate([g, jnp.zeros((LANE - QK_HEAD,), F32)])
        swp = jnp.concatenate([jnp.zeros((QK_NOPE,), F32), g[QK_NOPE:][swap], jnp.zeros((LANE - QK_HEAD,), F32)])
        return jnp.stack([main, swp])
    kbound = math.sqrt(QK_HEAD) * jnp.max(jnp.abs(k_norm_g))
    qbound = jnp.max(jnp.abs(q_norm_g)) * LOG2E
    p['gq'] = jnp.concatenate([gain_rows(q_norm_g), jnp.full((1, LANE), qbound * kbound, F32)], axis=0)
    p['gk'] = gain_rows(k_norm_g)
    p['fast_softmax'] = 2.0 * qbound * kbound <= FAST_SOFTMAX_MAX_SHIFT
    inv_freq = 1.0 / (ROPE_THETA ** (jnp.arange(half, dtype=F32) / half))
    ang = jnp.arange(max_seq, dtype=F32)[:, None] * inv_freq[None, :]
    cos, sin = jnp.cos(ang), jnp.sin(ang)
    zpad = jnp.zeros((max_seq, LANE - QK_HEAD), F32)
    p['ct'] = jnp.concatenate([jnp.ones((max_seq, QK_NOPE), F32), cos, cos, zpad], axis=1)
    p['st'] = jnp.concatenate([jnp.zeros((max_seq, QK_NOPE), F32), -sin, sin, zpad], axis=1)

    zl = jnp.zeros((DECAY_LORA, RW_WIDTH), F32)
    p['w_dec'] = jnp.concatenate([jnp.concatenate([w_up_f, zl], axis=1),
                                  jnp.concatenate([zl, w_up_b], axis=1)], axis=0).astype(BF16)
    p['w0'] = jnp.concatenate([w0_f, w0_b])[None, :]
    p['a_up'] = jnp.concatenate([a_up, jnp.zeros((LANE - A_LORA, RW_WIDTH), F32)], axis=0).astype(BF16)
    p['a0'] = a0[None, :]
    p['g_up'] = g_up.astype(BF16)
    p['k_k'] = k_k[None, :]
    p['k_a'] = k_a[None, :]
    p['r_k'] = r_k.reshape(1, RW_WIDTH)
    hid = jnp.arange(RW_WIDTH) // RW_HEAD
    p['seg'] = (hid[:, None] == hid[None, :]).astype(BF16)
    ti = jnp.arange(CHUNK)[:, None]
    tj = jnp.arange(CHUNK)[None, :]
    p['sc_tri'] = jnp.stack([tj <= ti, tj >= ti]).astype(BF16)
    ps = (jnp.arange(HGW) % CHUNK)[None, :]
    p['sc_m64'] = jnp.stack([ps < ti, ps <= ti, ps > ti, ps >= ti, ps == ti]).astype(F32)
    blk = jnp.arange(HGW) // RW_HEAD
    same = blk[:, None] == blk[None, :]
    p['sc_bd16'] = same.astype(BF16)
    p['sc_bd32'] = jnp.stack([same, jnp.eye(HGW, dtype=bool)]).astype(F32)
    p['ln_g'] = ln_x_g[None, :]
    p['ln_b'] = ln_x_b[None, :]
    p['w_out'] = w_out.astype(BF16)
    p['g2'] = norm2_g[None, :]
    p['w_r'] = jnp.concatenate([w_router_expert, w_router_group,
                                zcol(d, LANE - N_EXPERTS - N_GROUPS)], axis=1).astype(BF16)
    p['b_r'] = jnp.concatenate([b_router_expert, b_router_group,
                                jnp.zeros((LANE - N_EXPERTS - N_GROUPS,), F32)])[None, :]
    hidden = N_EXPERTS * D_EXPERT
    p['wg'] = jnp.transpose(w_expert_gate, (1, 0, 2)).reshape(d, hidden).astype(BF16)
    p['wu'] = jnp.transpose(w_expert_up, (1, 0, 2)).reshape(d, hidden).astype(BF16)
    p['wd'] = w_expert_down.reshape(hidden, d).astype(BF16)
    unit_e = jnp.arange(hidden) // D_EXPERT
    expand = (jnp.arange(LANE)[:, None] == unit_e[None, :]).astype(BF16)
    p['expand'] = expand.reshape(LANE, hidden // th, th).transpose(1, 0, 2)
    return p


TM_PROJ = 512
TQ = 1024
TK = 2048
TQ_MAX = 1024
TK_MAX = 1024
TM_PREP = 512
TM_POST = 512
TM_MOE = 1024
TH_MOE = 1024


def _layer(x, p):
    bsz, seq, d = x.shape
    x2 = x.reshape(bsz * seq, d)
    q, k, v, u = _proj(x2, seq, p, TM_PROJ)
    oa = _attn(q, k, v, bsz, seq, p['fast_softmax'])
    r, k2, vv, kkn, kka, lwf, lwb, g, bonus = _rwprep(u, seq, p, TM_PREP)
    yf, yb = _rwscan(r, k2, vv, kkn, kka, lwf, lwb, bsz, seq, p)
    x1, h2, gate = _post(x2, oa, yf, yb, bonus, g, p, TM_POST)
    out = _moe(x1, h2, gate, p, TM_MOE, TH_MOE)
    return out.reshape(bsz, seq, d)


def kernel(x_prompt, x_sample, norm1_g, w_in, q_a_norm_g, w_uq, kv_a_norm_g, w_ukv, q_norm_g, k_norm_g, mu_shift, w0_f, w_up_f, w0_b, w_up_b, a0, a_up, g_up, k_k, k_a, r_k, ln_x_g, ln_x_b, w_out, norm2_g, w_router_group, b_router_group, w_router_expert, b_router_expert, w_expert_gate, w_expert_up, w_expert_down):
    layer_params = (norm1_g, w_in, q_a_norm_g, w_uq, kv_a_norm_g, w_ukv, q_norm_g, k_norm_g, mu_shift,
                    w0_f, w_up_f, w0_b, w_up_b, a0, a_up, g_up, k_k, k_a, r_k, ln_x_g, ln_x_b, w_out,
                    norm2_g, w_router_group, b_router_group, w_router_expert, b_router_expert,
                    w_expert_gate, w_expert_up, w_expert_down)
    y_prompt, y_sample = x_prompt, x_sample
    max_seq = max(x_prompt.shape[1], x_sample.shape[1])
    for layer in range(norm1_g.shape[0]):
        p = _pack_params([w[layer] for w in layer_params], max_seq, TH_MOE)
        y_prompt = _layer(y_prompt, p)
        y_sample = _layer(y_sample, p)
    return (y_prompt, y_sample)
```

```python
import functools
import math

import jax
import jax.numpy as jnp
from jax import lax
from jax.experimental import pallas as pl
from jax.experimental.pallas import tpu as pltpu

D_MODEL = 1024
MLA_HEADS = 8
QK_NOPE = 64
QK_ROPE = 32
QK_HEAD = QK_NOPE + QK_ROPE
V_HEAD = 64
Q_LORA = 384
KV_LORA = 256
ROPE_THETA = 10000.0
RW_HEADS = 8
RW_HEAD = 64
RW_WIDTH = RW_HEADS * RW_HEAD
DECAY_LORA = 64
A_LORA = 64
GATE_LORA = 128
LN_X_EPS = 6.4e-4
MLA_WIDTH = MLA_HEADS * V_HEAD
N_GROUPS = 4
EXPERTS_PER_GROUP = 8
N_EXPERTS = N_GROUPS * EXPERTS_PER_GROUP
D_EXPERT = 256
GROUP_HIDDEN = EXPERTS_PER_GROUP * D_EXPERT
MOE_CHUNK = 512
EXPERTS_PER_CHUNK = MOE_CHUNK // D_EXPERT
RMS_EPS = 1e-6

LANE = 128
HEAD_PAD = LANE
MLA_PAD = MLA_HEADS * HEAD_PAD
RW_COLS_PAD = 3 * RW_WIDTH + 3 * LANE
MLA_COLS_PAD = Q_LORA + KV_LORA + 2 * LANE
D_IN_PAD = MLA_COLS_PAD + RW_COLS_PAD
CHUNK = 64
HG = 4
HGW = HG * RW_HEAD
VMEM_LIMIT = 56 * 1024 * 1024
LOG2E = 1.4426950408889634
NCH = 4
FAST_SOFTMAX_MAX_SHIFT = 100.0

F32 = jnp.float32
BF16 = jnp.bfloat16


def _dot(a, b):
    return jnp.dot(a, b, preferred_element_type=F32)


def _split_dot(a, b_exact):
    hi = a.astype(BF16)
    lo = (a - hi.astype(F32)).astype(BF16)
    return _dot(hi, b_exact) + _dot(lo, b_exact)


def _cparams(sem):
    return pltpu.CompilerParams(dimension_semantics=sem, vmem_limit_bytes=VMEM_LIMIT)


def _proj_kernel(x_ref, g1_ref, win_ref, qag_ref, wuq_ref, kvag_ref, wukv_ref, gq_ref, gk_ref,
                 ct_ref, st_ref, q_out, k_out, v_out, u_out):
    x = x_ref[...]
    h = x * lax.rsqrt(jnp.mean(x * x, axis=-1, keepdims=True) + RMS_EPS) * g1_ref[...]
    z = _dot(h.astype(BF16), win_ref[...])
    u_out[...] = z[:, MLA_COLS_PAD:]

    cq = z[:, :Q_LORA]
    cqn = cq * lax.rsqrt(jnp.mean(cq * cq, axis=-1, keepdims=True) + RMS_EPS) * qag_ref[...]
    q2 = _dot(cqn.astype(BF16), wuq_ref[...])
    ckv = z[:, Q_LORA:Q_LORA + KV_LORA]
    ckvn = ckv * lax.rsqrt(jnp.mean(ckv * ckv, axis=-1, keepdims=True) + RMS_EPS) * kvag_ref[...]
    kv2 = _dot(ckvn.astype(BF16), wukv_ref[...])
    kr_main = z[:, Q_LORA + KV_LORA:Q_LORA + KV_LORA + LANE]
    kr_swap = z[:, Q_LORA + KV_LORA + LANE:MLA_COLS_PAD]

    ct = ct_ref[...]
    st = st_ref[...]
    gq = gq_ref[...]
    gk = gk_ref[...]
    scale = QK_HEAD ** -0.5 * LOG2E
    lane = lax.broadcasted_iota(jnp.int32, (1, LANE), 1)
    ones_col = jnp.where(lane == V_HEAD, 1.0, 0.0).astype(F32)
    shift_col = jnp.where(lane == QK_HEAD, 1.0, 0.0).astype(F32)
    q_shift = shift_col * gq[2:3, 0:1]
    kr_ssq = jnp.sum(kr_main * kr_main, axis=-1, keepdims=True)
    for hd in range(MLA_HEADS):
        sl = slice(hd * HEAD_PAD, (hd + 1) * HEAD_PAD)
        qm = q2[:, sl]
        qs = q2[:, MLA_PAD + hd * HEAD_PAD:MLA_PAD + (hd + 1) * HEAD_PAD]
        rinv = lax.rsqrt(jnp.sum(qm * qm, axis=-1, keepdims=True) * (1.0 / QK_HEAD) + RMS_EPS)
        qh = (qm * gq[0:1] * ct + qs * gq[1:2] * st) * (rinv * scale)
        q_out[:, sl] = (qh - q_shift).astype(BF16)
        kn = kv2[:, sl]
        rinv_k = lax.rsqrt((jnp.sum(kn * kn, axis=-1, keepdims=True) + kr_ssq) * (1.0 / QK_HEAD) + RMS_EPS)
        kh = ((kn + kr_main) * gk[0:1] * ct + kr_swap * gk[1:2] * st) * rinv_k
        k_out[:, sl] = (kh + shift_col).astype(BF16)
        v_out[:, sl] = (kv2[:, MLA_PAD + hd * HEAD_PAD:MLA_PAD + (hd + 1) * HEAD_PAD] + ones_col).astype(BF16)


def _proj(x2, seq, p, tm):
    n = x2.shape[0]
    tm = min(tm, seq)
    nseq_t = seq // tm
    row = lambda i: (i, 0)
    fixed = lambda i: (0, 0)
    pos = lambda i: (i % nseq_t, 0)
    full = lambda a: pl.BlockSpec(a.shape, fixed)
    return pl.pallas_call(
        _proj_kernel,
        grid=(n // tm,),
        in_specs=[pl.BlockSpec((tm, D_MODEL), row), full(p['g1']), full(p['w_in']), full(p['qag']),
                  full(p['w_uq']), full(p['kvag']), full(p['w_ukv']), full(p['gq']), full(p['gk']),
                  pl.BlockSpec((tm, LANE), pos), pl.BlockSpec((tm, LANE), pos)],
        out_specs=[pl.BlockSpec((tm, MLA_PAD), row), pl.BlockSpec((tm, MLA_PAD), row),
                   pl.BlockSpec((tm, MLA_PAD), row), pl.BlockSpec((tm, RW_COLS_PAD), row)],
        out_shape=[jax.ShapeDtypeStruct((n, MLA_PAD), BF16), jax.ShapeDtypeStruct((n, MLA_PAD), BF16),
                   jax.ShapeDtypeStruct((n, MLA_PAD), BF16), jax.ShapeDtypeStruct((n, RW_COLS_PAD), F32)],
        compiler_params=_cparams(("parallel",)),
        name="proj",
    )(x2, p['g1'], p['w_in'], p['qag'], p['w_uq'], p['kvag'], p['w_ukv'], p['gq'], p['gk'],
      p['ct'][:seq], p['st'][:seq])


def _attn_kernel(q_ref, k_ref, v_ref, o_ref, m_ref, acc_ref, *, running_max):
    j = pl.program_id(3)

    @pl.when(j == 0)
    def _():
        if running_max:
            m_ref[...] = jnp.full(m_ref.shape, -jnp.inf, F32)
        acc_ref[...] = jnp.zeros(acc_ref.shape, F32)

    heads = [slice(hh * HEAD_PAD, (hh + 1) * HEAD_PAD) for hh in range(2)]
    scores = [lax.dot_general(k_ref[:, sl], q_ref[:, sl], (((1,), (1,)), ((), ())),
                              preferred_element_type=F32) for sl in heads]
    for hh, sl in enumerate(heads):
        s = scores[hh]
        if running_max:
            m_old = m_ref[hh]
            m_new = jnp.maximum(m_old, jnp.max(s, axis=0, keepdims=True))
            p = jnp.exp2(s - m_new)
            m_ref[hh] = m_new
        else:
            p = jnp.exp2(s)
        pv = lax.dot_general(v_ref[:, sl], p.astype(BF16), (((0,), (0,)), ((), ())),
                             preferred_element_type=F32)
        if running_max:
            acc_ref[hh] = jnp.exp2(m_old - m_new) * acc_ref[hh] + pv
        else:
            acc_ref[hh] += pv

    @pl.when(j == pl.num_programs(3) - 1)
    def _():
        outs = []
        for hh in range(2):
            a = acc_ref[hh]
            outs.append(a[:V_HEAD] / a[V_HEAD:V_HEAD + 1])
        o_ref[...] = jnp.concatenate(outs, axis=0).T.astype(BF16)


def _attn(q, k, v, bsz, seq, fast_softmax):
    return lax.cond(fast_softmax,
                    lambda q, k, v: _attn_call(q, k, v, bsz, seq, TQ, TK, False),
                    lambda q, k, v: _attn_call(q, k, v, bsz, seq, TQ_MAX, TK_MAX, True),
                    q, k, v)


def _attn_call(q, k, v, bsz, seq, tq, tk, running_max):
    tq = min(tq, seq)
    tk = min(tk, seq)
    nq, nk = seq // tq, seq // tk
    n = q.shape[0]
    return pl.pallas_call(
        functools.partial(_attn_kernel, running_max=running_max),
        grid=(bsz, MLA_HEADS // 2, nq, nk),
        in_specs=[pl.BlockSpec((tq, 2 * HEAD_PAD), lambda b, h, i, j: (b * nq + i, h)),
                  pl.BlockSpec((tk, 2 * HEAD_PAD), lambda b, h, i, j: (b * nk + j, h)),
                  pl.BlockSpec((tk, 2 * HEAD_PAD), lambda b, h, i, j: (b * nk + j, h))],
        out_specs=pl.BlockSpec((tq, 2 * V_HEAD), lambda b, h, i, j: (b * nq + i, h)),
        out_shape=jax.ShapeDtypeStruct((n, MLA_WIDTH), BF16),
        scratch_shapes=[pltpu.VMEM((2, 1, tq), F32), pltpu.VMEM((2, HEAD_PAD, tq), F32)],
        compiler_params=_cparams(("parallel", "parallel", "parallel", "arbitrary")),
        name="attn_max" if running_max else "attn",
    )(q, k, v)


def _rwprep_kernel(u_ref, up_ref, un_ref, mu_ref, wdec_ref, w0_ref, aup_ref, a0_ref, gup_ref,
                   kk_ref, ka_ref, rk_ref, seg_ref,
                   r_out, k_out, v_out, kkn_out, kka_out, lwf_out, lwb_out, g_out, bonus_out,
                   *, tm, seq):
    i = pl.program_id(0)
    u = u_ref[...]
    first = (i * tm) % seq == 0
    last = ((i + 1) * tm) % seq == 0
    prev_row = jnp.where(first, 0.0, up_ref[0, 7:8, :])
    next_row = jnp.where(last, 0.0, un_ref[0, 0:1, :])
    rows = lax.broadcasted_iota(jnp.int32, (tm, 1), 0)
    prev = jnp.where(rows == 0, prev_row, pltpu.roll(u, 1, 0))
    nxt = jnp.where(rows == tm - 1, next_row, pltpu.roll(u, tm - 1, 0))
    us = u + (0.5 * (prev + nxt) - u) * mu_ref[...]

    r = us[:, :RW_WIDTH]
    k = us[:, RW_WIDTH:2 * RW_WIDTH]
    v = us[:, 2 * RW_WIDTH:3 * RW_WIDTH]
    xw = us[:, 3 * RW_WIDTH:3 * RW_WIDTH + LANE]
    xa = us[:, 3 * RW_WIDTH + LANE:3 * RW_WIDTH + 2 * LANE]
    xg = us[:, 3 * RW_WIDTH + 2 * LANE:]

    wpre = w0_ref[...] + _dot(jnp.tanh(xw).astype(BF16), wdec_ref[...])
    wlog = -(jnp.maximum(-wpre, 0.0) + jnp.log1p(jnp.exp(-jnp.abs(wpre)))) - 0.5
    lw = -jnp.exp(wlog)
    lwf_out[...] = lw[:, :RW_WIDTH]
    lwb_out[...] = lw[:, RW_WIDTH:]
    a = jax.nn.sigmoid(a0_ref[...] + _dot(xa.astype(BF16), aup_ref[...]))
    g_out[...] = _dot(jax.nn.sigmoid(xg).astype(BF16), gup_ref[...])

    seg = seg_ref[...]
    kk = k * kk_ref[...]
    kkn = kk / jnp.maximum(jnp.sqrt(_split_dot(kk * kk, seg)), 1e-12)
    k2 = k * (1.0 + (a - 1.0) * ka_ref[...])
    r_out[...] = r
    k_out[...] = k2
    v_out[...] = v
    kkn_out[...] = kkn
    kka_out[...] = kkn * a
    bonus_out[...] = _split_dot(r * k2 * rk_ref[...], seg) * v


def _rwprep(u, seq, p, tm):
    n = u.shape[0]
    tm = min(tm, seq)
    g8 = tm // 8
    u3 = u.reshape(n // 8, 8, RW_COLS_PAD)
    ngrp = n // 8
    row = lambda i: (i, 0)
    fixed = lambda i: (0, 0)
    full = lambda a: pl.BlockSpec(a.shape, fixed)
    out_sds = jax.ShapeDtypeStruct((n, RW_WIDTH), F32)
    out_spec = pl.BlockSpec((tm, RW_WIDTH), row)
    return pl.pallas_call(
        functools.partial(_rwprep_kernel, tm=tm, seq=seq),
        grid=(n // tm,),
        in_specs=[pl.BlockSpec((tm, RW_COLS_PAD), row),
                  pl.BlockSpec((1, 8, RW_COLS_PAD), lambda i: (jnp.maximum(i * g8 - 1, 0), 0, 0)),
                  pl.BlockSpec((1, 8, RW_COLS_PAD), lambda i: (jnp.minimum((i + 1) * g8, ngrp - 1), 0, 0)),
                  full(p['mu']), full(p['w_dec']), full(p['w0']), full(p['a_up']), full(p['a0']),
                  full(p['g_up']), full(p['k_k']), full(p['k_a']), full(p['r_k']), full(p['seg'])],
        out_specs=[out_spec] * 9,
        out_shape=[out_sds] * 9,
        compiler_params=_cparams(("parallel",)),
        name="rwprep",
    )(u, u3, u3, p['mu'], p['w_dec'], p['w0'], p['a_up'], p['a0'], p['g_up'], p['k_k'], p['k_a'],
      p['r_k'], p['seg'])


def _bd(x, bd16):
    return jnp.concatenate([x.astype(BF16)] * HG, axis=0) * bd16


def _chunk_intra(r, k, v, kkn, kka, lw, reverse, consts):
    tri, strict_m, incl_m, eye_p, bd16, bd32, eye32 = consts
    lg = _split_dot3(tri, lw)
    yield
    lgx = lg - lw
    tot = lg[0:1] if reverse else lg[CHUNK - 1:CHUNK]
    gi = jnp.exp(lg)
    ginv = jnp.exp(-lg)
    gend = jnp.exp(tot - lg)
    at = -kkn * jnp.exp(lgx)
    rt = r * gi
    bt = kka * ginv
    kt = k * ginv
    bh = kka * gend
    kh = k * gend

    lhs = jnp.concatenate([at, rt], axis=0).astype(BF16)
    rhs = jnp.concatenate([_bd(bt, bd16), _bd(kt, bd16)], axis=0)
    a_all = lax.dot_general(lhs, rhs, (((1,), (1,)), ((), ())), preferred_element_type=F32)
    yield
    n_ab = a_all[:CHUNK, :HGW] * strict_m
    a_ak = a_all[:CHUNK, HGW:] * strict_m
    a_rb = a_all[CHUNK:, :HGW] * incl_m
    a_rk = a_all[CHUNK:, HGW:] * incl_m

    t = eye_p + n_ab
    nk = _dot(n_ab.astype(BF16), _bd(n_ab, bd16))
    yield
    for _ in range(4):
        both = _dot(jnp.concatenate([nk, t], axis=0).astype(BF16), _bd(nk, bd16))
        yield
        nk = both[:CHUNK]
        t = t + both[CHUNK:]
    tn = _dot(t.astype(BF16), _bd(nk, bd16))
    yield
    t = t + tn

    tb = t.astype(BF16)
    vbd = _bd(v, bd16)
    w = _dot(tb, _bd(at, bd16))
    akv = _dot(a_ak.astype(BF16), vbd)
    yield
    uv = _dot(tb, _bd(akv, bd16))
    arb = a_rb.astype(BF16)
    qpd = _dot(arb, _bd(w, bd16))
    yield
    y_in = _dot(jnp.concatenate([arb, a_rk.astype(BF16)], axis=1),
                jnp.concatenate([_bd(uv, bd16), vbd], axis=0))
    lhs_t = jnp.concatenate([bh, kh], axis=0).astype(BF16)
    rhs_t = jnp.concatenate([jnp.concatenate([w, uv], axis=1),
                             jnp.concatenate([jnp.zeros_like(v), v], axis=1)], axis=0).astype(BF16)
    mc = lax.dot_general(lhs_t, rhs_t, (((0,), (0,)), ((), ())), preferred_element_type=F32)
    yield
    m_bd = mc[:, :HGW] * bd32 + eye32 * jnp.exp(tot)
    c_bd = mc[:, HGW:] * bd32
    return y_in, (rt + qpd).astype(BF16), m_bd.astype(BF16), c_bd


def _interleave(gens):
    results = [None] * len(gens)
    active = list(range(len(gens)))
    while active:
        still = []
        for i in active:
            try:
                next(gens[i])
                still.append(i)
            except StopIteration as stop:
                results[i] = stop.value
        active = still
    return results


def _split_dot3(b_exact, a):
    hi = a.astype(BF16)
    r1 = a - hi.astype(F32)
    mid = r1.astype(BF16)
    lo = (r1 - mid.astype(F32)).astype(BF16)
    return _dot(b_exact, hi) + _dot(b_exact, mid) + _dot(b_exact, lo)


def _rwscan_kernel(rf, kf, vf, nf, af, lf, rb, kb, vb, nb, ab, lb, tri_ref, m64_ref, bd16_ref, bd32_ref,
                   yf_out, yb_out, s_ref):
    c = pl.program_id(1)

    @pl.when(c == 0)
    def _():
        s_ref[...] = jnp.zeros(s_ref.shape, F32)

    nch = rf.shape[0] // CHUNK
    bd16 = bd16_ref[...]
    bd32 = bd32_ref[0]
    eye32 = bd32_ref[1]
    eye_p = m64_ref[4]
    ngrp = RW_HEADS // HG
    chains = []
    gens = []
    for d, refs, y_out in ((0, (rf, kf, vf, nf, af, lf), yf_out), (1, (rb, kb, vb, nb, ab, lb), yb_out)):
        reverse = d == 1
        consts = (tri_ref[d], m64_ref[2 * d], m64_ref[2 * d + 1], eye_p, bd16, bd32, eye32)
        order = list(range(nch - 1, -1, -1) if reverse else range(nch))
        for g in range(ngrp):
            chains.append((d, g, y_out, order))
            for j in order:
                ops = [x[j * CHUNK:(j + 1) * CHUNK, g * HGW:(g + 1) * HGW] for x in refs]
                gens.append(_chunk_intra(*ops, reverse, consts))
    parts = _interleave(gens)

    states = [s_ref[d, g] for d, g, _, _ in chains]
    for step in range(nch):
        for ci, (d, g, y_out, order) in enumerate(chains):
            j = order[step]
            y_in, qp, m_bd, c_bd = parts[ci * nch + step]
            sb = states[ci].astype(BF16)
            y_out[j * CHUNK:(j + 1) * CHUNK, g * HGW:(g + 1) * HGW] = y_in + _dot(qp, sb)
            states[ci] = _dot(m_bd, sb) + c_bd
    for ci, (d, g, _, _) in enumerate(chains):
        s_ref[d, g] = states[ci]


def _rwscan(r, k, v, kkn, kka, lwf, lwb, bsz, seq, p):
    n = r.shape[0]
    rows = min(NCH * CHUNK, seq)
    nb = seq // rows
    fwd = pl.BlockSpec((rows, RW_WIDTH), lambda b, c: (b * nb + c, 0))
    bwd = pl.BlockSpec((rows, RW_WIDTH), lambda b, c: (b * nb + nb - 1 - c, 0))
    full = lambda a: pl.BlockSpec(a.shape, lambda b, c: (0,) * a.ndim)
    sds = jax.ShapeDtypeStruct((n, RW_WIDTH), F32)
    return pl.pallas_call(
        _rwscan_kernel,
        grid=(bsz, nb),
        in_specs=[fwd] * 6 + [bwd] * 6 + [full(p['sc_tri']), full(p['sc_m64']), full(p['sc_bd16']),
                                           full(p['sc_bd32'])],
        out_specs=[fwd, bwd],
        out_shape=[sds, sds],
        scratch_shapes=[pltpu.VMEM((2, RW_HEADS // HG, HGW, HGW), F32)],
        compiler_params=_cparams(("parallel", "arbitrary")),
        name="rwscan",
    )(r, k, v, kkn, kka, lwf, r, k, v, kkn, kka, lwb, p['sc_tri'], p['sc_m64'], p['sc_bd16'], p['sc_bd32'])


def _post_kernel(x_ref, oa_ref, yf_ref, yb_ref, bonus_ref, g_ref, lng_ref, lnb_ref, seg_ref, wout_ref,
                 g2_ref, wr_ref, br_ref, x1_out, hp_out, gid_out):
    seg = seg_ref[...]
    y = yf_ref[...] + yb_ref[...]
    mean = _split_dot(y, seg) * (1.0 / RW_HEAD)
    dlt = y - mean
    var = _split_dot(dlt * dlt, seg) * (1.0 / RW_HEAD)
    yn = dlt * lax.rsqrt(var + LN_X_EPS) * lng_ref[...] + lnb_ref[...]
    ob = (yn + bonus_ref[...]) * g_ref[...]
    x1 = (x_ref[...] + _dot(oa_ref[...], wout_ref[:MLA_WIDTH, :])
          + _dot(ob.astype(BF16), wout_ref[MLA_WIDTH:, :]))
    x1_out[...] = x1
    h2 = x1 * lax.rsqrt(jnp.mean(x1 * x1, axis=-1, keepdims=True) + RMS_EPS) * g2_ref[...]
    h2b = h2.astype(BF16)
    bits = lax.bitcast_convert_type(h2b.astype(F32), jnp.uint32)
    hp_out[...] = (bits[:, :D_MODEL // 2] >> 16) | (bits[:, D_MODEL // 2:] & jnp.uint32(0xFFFF0000))
    logits = _dot(h2b, wr_ref[...]) + br_ref[...]
    _, gidx, _ = _group_choice(logits)
    gid_out[...] = jnp.broadcast_to(gidx, gid_out.shape)


def _group_choice(logits):
    lane_i = lax.broadcasted_iota(jnp.int32, logits.shape, 1)
    lane = lane_i.astype(F32)
    is_g = (lane_i >= N_EXPERTS) & (lane_i < N_EXPERTS + N_GROUPS)
    gl = jnp.where(is_g, logits, -jnp.inf)
    gmax = jnp.max(gl, axis=-1, keepdims=True)
    gidx = jnp.min(jnp.where(gl == gmax, lane, 1e9), axis=-1, keepdims=True) - N_EXPERTS
    p_g = 1.0 / jnp.sum(jnp.where(is_g, jnp.exp(gl - gmax), 0.0), axis=-1, keepdims=True)
    return p_g, gidx, lane


def _post(x2, oa, yf, yb, bonus, g, p, tm):
    n = x2.shape[0]
    tm = min(tm, n)
    row = lambda i: (i, 0)
    fixed = lambda i: (0, 0)
    full = lambda a: pl.BlockSpec(a.shape, fixed)
    rw = pl.BlockSpec((tm, RW_WIDTH), row)
    return pl.pallas_call(
        _post_kernel,
        grid=(n // tm,),
        in_specs=[pl.BlockSpec((tm, D_MODEL), row), pl.BlockSpec((tm, MLA_WIDTH), row), rw, rw, rw, rw,
                  full(p['ln_g']), full(p['ln_b']), full(p['seg']), full(p['w_out']), full(p['g2']),
                  full(p['w_r']), full(p['b_r'])],
        out_specs=[pl.BlockSpec((tm, D_MODEL), row), pl.BlockSpec((tm, D_MODEL // 2), row),
                   pl.BlockSpec((tm, LANE), row)],
        out_shape=[jax.ShapeDtypeStruct((n, D_MODEL), F32), jax.ShapeDtypeStruct((n, D_MODEL // 2), jnp.uint32),
                   jax.ShapeDtypeStruct((n, LANE), F32)],
        compiler_params=_cparams(("parallel",)),
        name="post",
    )(x2, oa, yf, yb, bonus, g, p['ln_g'], p['ln_b'], p['seg'], p['w_out'], p['g2'], p['w_r'], p['b_r'])


def _route_plan(gid, tm, npt):
    g = gid[:, 0].astype(jnp.int32)
    onehot = (g[:, None] == jnp.arange(N_GROUPS, dtype=jnp.int32)[None, :]).astype(jnp.int32)
    csum = jnp.cumsum(onehot, axis=0)
    rank = jnp.sum(csum * onehot, axis=1) - 1
    ntile = (csum[-1] + tm - 1) // tm
    tile_end = jnp.cumsum(ntile)
    off = (tile_end - ntile) * tm
    pos = jnp.sum(onehot * off[None, :], axis=1) + rank
    n_active = tile_end[-1]
    t = jnp.minimum(jnp.arange(npt, dtype=jnp.int32), n_active - 1)
    tile_gid = jnp.sum((t[:, None] >= tile_end[None, :]).astype(jnp.int32), axis=1)
    meta = jnp.concatenate([n_active[None], tile_gid]).astype(jnp.int32)
    return pos.astype(jnp.int32), meta


def _row_dma_loops(n_rows, make_copy):
    def issue(r, carry):
        make_copy(r).start()
        return carry
    lax.fori_loop(0, n_rows, issue, 0, unroll=8)

    def drain(r, carry):
        make_copy(0).wait()
        return carry
    lax.fori_loop(0, n_rows, drain, 0, unroll=8)


def _scatter_kernel(pos_ref, src_ref, init_ref, dst_ref, sem, *, tms):
    base = pl.program_id(0) * tms
    _row_dma_loops(tms, lambda r: pltpu.make_async_copy(
        src_ref.at[pl.ds(base + r, 1)], dst_ref.at[pl.ds(pos_ref[0, 0, r], 1)], sem))


def _scatter_rows(pos, src, n_dst, tms):
    n, w = src.shape
    tms = min(tms, n)
    return pl.pallas_call(
        functools.partial(_scatter_kernel, tms=tms),
        grid=(n // tms,),
        in_specs=[pl.BlockSpec((1, 1, tms), lambda i: (i, 0, 0), memory_space=pltpu.SMEM),
                  pl.BlockSpec(memory_space=pl.ANY), pl.BlockSpec(memory_space=pl.ANY)],
        out_specs=pl.BlockSpec(memory_space=pl.ANY),
        out_shape=jax.ShapeDtypeStruct((n_dst, w), src.dtype),
        scratch_shapes=[pltpu.SemaphoreType.DMA(())],
        input_output_aliases={2: 0},
        compiler_params=_cparams(("arbitrary",)),
        name="moe_scatter",
    )(pos.reshape(n // tms, 1, tms), src, jnp.zeros((n_dst, w), src.dtype))


def _moe_group_kernel(meta_ref, xs_ref, wr_ref, br_ref, wg_ref, wu_ref, wd_ref, ys_ref):
    i = pl.program_id(0)

    @pl.when(i >= meta_ref[0])
    def _():
        ys_ref[...] = jnp.zeros(ys_ref.shape, F32)

    @pl.when(i < meta_ref[0])
    def _():
        tm = xs_ref.shape[0]
        word = xs_ref[...]
        h2 = jnp.concatenate([lax.bitcast_convert_type(word << 16, F32),
                              lax.bitcast_convert_type(word & jnp.uint32(0xFFFF0000), F32)],
                             axis=1).astype(BF16)
        logits = _dot(h2, wr_ref[...]) + br_ref[...]
        p_g, _, lane = _group_choice(logits)
        first = (meta_ref[1 + i] * EXPERTS_PER_GROUP).astype(F32)
        in_grp = (lane >= first) & (lane < first + EXPERTS_PER_GROUP)
        el = jnp.where(in_grp, logits, -jnp.inf)
        v1 = jnp.max(el, axis=-1, keepdims=True)
        i1 = jnp.min(jnp.where(el == v1, lane, 1e9), axis=-1, keepdims=True)
        el2 = jnp.where(lane == i1, -jnp.inf, el)
        v2 = jnp.max(el2, axis=-1, keepdims=True)
        i2 = jnp.min(jnp.where(el2 == v2, lane, 1e9), axis=-1, keepdims=True)
        e2 = jnp.exp(v2 - v1)
        den = 1.0 + e2
        w1 = (1.0 / den) * p_g
        w2 = (e2 / den) * p_g

        def gate_cols(c):
            cols = []
            for k in range(c * EXPERTS_PER_CHUNK, (c + 1) * EXPERTS_PER_CHUNK):
                gk = jnp.where(i1 == first + k, w1, 0.0) + jnp.where(i2 == first + k, w2, 0.0)
                cols.append(jnp.broadcast_to(gk, (tm, D_EXPERT)))
            return jnp.concatenate(cols, axis=1)

        acc = None
        pend = None
        nchunk = GROUP_HIDDEN // MOE_CHUNK
        for c in range(nchunk + 1):
            cur = None
            if c < nchunk:
                cs = slice(c * MOE_CHUNK, (c + 1) * MOE_CHUNK)
                cur = (_dot(h2, wg_ref[0, :, cs]), _dot(h2, wu_ref[0, :, cs]), c)
            if pend is not None:
                hg, hu, pc = pend
                act = hg * jax.nn.sigmoid(hg) * hu * gate_cols(pc)
                part = _dot(act.astype(BF16), wd_ref[0, pc * MOE_CHUNK:(pc + 1) * MOE_CHUNK, :])
                acc = part if acc is None else acc + part
            pend = cur
        ys_ref[...] = acc


def _moe_grouped(xs, meta, p, tm, npt):
    def tile(i, m):
        return (jnp.minimum(i, m[0] - 1), 0)

    def group(i, m):
        return (m[1 + i], 0, 0)

    grid_spec = pltpu.PrefetchScalarGridSpec(
        num_scalar_prefetch=1,
        grid=(npt,),
        in_specs=[pl.BlockSpec((tm, D_MODEL // 2), tile),
                  pl.BlockSpec(p['w_r'].shape, lambda i, m: (0, 0)),
                  pl.BlockSpec(p['b_r'].shape, lambda i, m: (0, 0)),
                  pl.BlockSpec((1, D_MODEL, GROUP_HIDDEN), group),
                  pl.BlockSpec((1, D_MODEL, GROUP_HIDDEN), group),
                  pl.BlockSpec((1, GROUP_HIDDEN, D_MODEL), group)],
        out_specs=pl.BlockSpec((tm, D_MODEL), lambda i, m: (i, 0)),
    )
    return pl.pallas_call(
        _moe_group_kernel,
        grid_spec=grid_spec,
        out_shape=jax.ShapeDtypeStruct((npt * tm, D_MODEL), F32),
        compiler_params=_cparams(("arbitrary",)),
        name="moe_group",
    )(meta, xs, p['w_r'], p['b_r'], p['wg'], p['wu'], p['wd'])


def _gather_kernel(pos_ref, x1_ref, ys_ref, o_ref, buf, sem):
    _row_dma_loops(buf.shape[0], lambda r: pltpu.make_async_copy(
        ys_ref.at[pl.ds(pos_ref[0, 0, r], 1)], buf.at[pl.ds(r, 1)], sem))
    o_ref[...] = x1_ref[...] + buf[...]


def _gather_add(pos, x1, ys, tms):
    n, d = x1.shape
    tms = min(tms, n)
    return pl.pallas_call(
        _gather_kernel,
        grid=(n // tms,),
        in_specs=[pl.BlockSpec((1, 1, tms), lambda i: (i, 0, 0), memory_space=pltpu.SMEM),
                  pl.BlockSpec((tms, d), lambda i: (i, 0)),
                  pl.BlockSpec(memory_space=pl.ANY)],
        out_specs=pl.BlockSpec((tms, d), lambda i: (i, 0)),
        out_shape=jax.ShapeDtypeStruct((n, d), F32),
        scratch_shapes=[pltpu.VMEM((tms, d), F32), pltpu.SemaphoreType.DMA(())],
        compiler_params=_cparams(("arbitrary",)),
        name="moe_gather",
    )(pos.reshape(n // tms, 1, tms), x1, ys)


def _head_pad_cols(w, head_w, lo, hi, dst, width=HEAD_PAD, perm=None):
    rows = w.shape[0]
    wh = w.reshape(rows, -1, head_w)[:, :, lo:hi]
    if perm is not None:
        wh = wh[:, :, perm]
    out = jnp.zeros((rows, wh.shape[1], width), w.dtype)
    return out.at[:, :, dst:dst + (hi - lo)].set(wh).reshape(rows, -1)


def _pack_params(lp, max_seq):
    (norm1_g, w_in, q_a_norm_g, w_uq, kv_a_norm_g, w_ukv, q_norm_g, k_norm_g, mu_shift, w0_f, w_up_f,
     w0_b, w_up_b, a0, a_up, g_up, k_k, k_a, r_k, ln_x_g, ln_x_b, w_out, norm2_g, w_router_group,
     b_router_group, w_router_expert, b_router_expert, w_expert_gate, w_expert_up, w_expert_down) = lp
    half = QK_ROPE // 2
    swap = jnp.concatenate([jnp.arange(half, QK_ROPE), jnp.arange(half)])
    zcol = lambda rows, n: jnp.zeros((rows, n), F32)
    p = {}
    c0 = Q_LORA + KV_LORA
    w_kr = w_in[:, c0:c0 + QK_ROPE]
    u0 = c0 + QK_ROPE
    wu_ = w_in[:, u0:]
    d = D_MODEL
    w_in_p = jnp.concatenate([
        w_in[:, :c0],
        zcol(d, QK_NOPE), w_kr, zcol(d, LANE - QK_HEAD),
        zcol(d, QK_NOPE), w_kr[:, swap], zcol(d, LANE - QK_HEAD),
        wu_[:, :3 * RW_WIDTH + 2 * DECAY_LORA + A_LORA], zcol(d, LANE - A_LORA),
        wu_[:, 3 * RW_WIDTH + 2 * DECAY_LORA + A_LORA:]], axis=1)
    p['w_in'] = w_in_p.astype(BF16)
    mu = mu_shift[None, :]
    p['mu'] = jnp.concatenate([mu[:, :3 * RW_WIDTH + 2 * DECAY_LORA + A_LORA], zcol(1, LANE - A_LORA),
                               mu[:, 3 * RW_WIDTH + 2 * DECAY_LORA + A_LORA:]], axis=1)
    p['g1'] = norm1_g[None, :]
    p['qag'] = q_a_norm_g[None, :]
    p['kvag'] = kv_a_norm_g[None, :]
    p['w_uq'] = jnp.concatenate([
        _head_pad_cols(w_uq, QK_HEAD, 0, QK_HEAD, 0),
        _head_pad_cols(w_uq, QK_HEAD, QK_NOPE, QK_HEAD, QK_NOPE, perm=swap)], axis=1).astype(BF16)
    p['w_ukv'] = jnp.concatenate([
        _head_pad_cols(w_ukv, QK_NOPE + V_HEAD, 0, QK_NOPE, 0),
        _head_pad_cols(w_ukv, QK_NOPE + V_HEAD, QK_NOPE, QK_NOPE + V_HEAD, 0)], axis=1).astype(BF16)

    def gain_rows(g):
        main = jnp.concatenate([g, jnp.zeros((LANE - QK_HEAD,), F32)])
        swp = jnp.concatenate([jnp.zeros((QK_NOPE,), F32), g[QK_NOPE:][swap], jnp.zeros((LANE - QK_HEAD,), F32)])
        return jnp.stack([main, swp])
    kbound = math.sqrt(QK_HEAD) * jnp.max(jnp.abs(k_norm_g))
    qbound = jnp.max(jnp.abs(q_norm_g)) * LOG2E
    p['gq'] = jnp.concatenate([gain_rows(q_norm_g), jnp.full((1, LANE), qbound * kbound, F32)], axis=0)
    p['gk'] = gain_rows(k_norm_g)
    p['fast_softmax'] = 2.0 * qbound * kbound <= FAST_SOFTMAX_MAX_SHIFT
    inv_freq = 1.0 / (ROPE_THETA ** (jnp.arange(half, dtype=F32) / half))
    ang = jnp.arange(max_seq, dtype=F32)[:, None] * inv_freq[None, :]
    cos, sin = jnp.cos(ang), jnp.sin(ang)
    zpad = jnp.zeros((max_seq, LANE - QK_HEAD), F32)
    p['ct'] = jnp.concatenate([jnp.ones((max_seq, QK_NOPE), F32), cos, cos, zpad], axis=1)
    p['st'] = jnp.concatenate([jnp.zeros((max_seq, QK_NOPE), F32), -sin, sin, zpad], axis=1)

    zl = jnp.zeros((DECAY_LORA, RW_WIDTH), F32)
    p['w_dec'] = jnp.concatenate([jnp.concatenate([w_up_f, zl], axis=1),
                                  jnp.concatenate([zl, w_up_b], axis=1)], axis=0).astype(BF16)
    p['w0'] = jnp.concatenate([w0_f, w0_b])[None, :]
    p['a_up'] = jnp.concatenate([a_up, jnp.zeros((LANE - A_LORA, RW_WIDTH), F32)], axis=0).astype(BF16)
    p['a0'] = a0[None, :]
    p['g_up'] = g_up.astype(BF16)
    p['k_k'] = k_k[None, :]
    p['k_a'] = k_a[None, :]
    p['r_k'] = r_k.reshape(1, RW_WIDTH)
    hid = jnp.arange(RW_WIDTH) // RW_HEAD
    p['seg'] = (hid[:, None] == hid[None, :]).astype(BF16)
    ti = jnp.arange(CHUNK)[:, None]
    tj = jnp.arange(CHUNK)[None, :]
    p['sc_tri'] = jnp.stack([tj <= ti, tj >= ti]).astype(BF16)
    ps = (jnp.arange(HGW) % CHUNK)[None, :]
    p['sc_m64'] = jnp.stack([ps < ti, ps <= ti, ps > ti, ps >= ti, ps == ti]).astype(F32)
    blk = jnp.arange(HGW) // RW_HEAD
    same = blk[:, None] == blk[None, :]
    p['sc_bd16'] = same.astype(BF16)
    p['sc_bd32'] = jnp.stack([same, jnp.eye(HGW, dtype=bool)]).astype(F32)
    p['ln_g'] = ln_x_g[None, :]
    p['ln_b'] = ln_x_b[None, :]
    p['w_out'] = w_out.astype(BF16)
    p['g2'] = norm2_g[None, :]
    p['w_r'] = jnp.concatenate([w_router_expert, w_router_group,
                                zcol(d, LANE - N_EXPERTS - N_GROUPS)], axis=1).astype(BF16)
    p['b_r'] = jnp.concatenate([b_router_expert, b_router_group,
                                jnp.zeros((LANE - N_EXPERTS - N_GROUPS,), F32)])[None, :]
    def group_cols(w):
        w = w.astype(BF16).reshape(N_GROUPS, EXPERTS_PER_GROUP, d, D_EXPERT)
        return jnp.transpose(w, (0, 2, 1, 3)).reshape(N_GROUPS, d, GROUP_HIDDEN)
    p['wg'] = group_cols(w_expert_gate)
    p['wu'] = group_cols(w_expert_up)
    p['wd'] = w_expert_down.astype(BF16).reshape(N_GROUPS, GROUP_HIDDEN, d)
    return p


TM_PROJ = 512
TQ = 1024
TK = 2048
TQ_MAX = 1024
TK_MAX = 1024
TM_PREP = 512
TM_POST = 512
TM_MOE = 512
TM_ROWS = 512


def _layer(x, p):
    bsz, seq, d = x.shape
    n = bsz * seq
    x2 = x.reshape(n, d)
    q, k, v, u = _proj(x2, seq, p, TM_PROJ)
    oa = _attn(q, k, v, bsz, seq, p['fast_softmax'])
    r, k2, vv, kkn, kka, lwf, lwb, g, bonus = _rwprep(u, seq, p, TM_PREP)
    yf, yb = _rwscan(r, k2, vv, kkn, kka, lwf, lwb, bsz, seq, p)
    x1, hp, gid = _post(x2, oa, yf, yb, bonus, g, p, TM_POST)
    tm = min(TM_MOE, n)
    npt = n // tm + N_GROUPS
    pos, meta = _route_plan(gid, tm, npt)
    xs = _scatter_rows(pos, hp, npt * tm, TM_ROWS)
    ys = _moe_grouped(xs, meta, p, tm, npt)
    out = _gather_add(pos, x1, ys, TM_ROWS)
    return out.reshape(bsz, seq, d)


def kernel(x_prompt, x_sample, norm1_g, w_in, q_a_norm_g, w_uq, kv_a_norm_g, w_ukv, q_norm_g, k_norm_g, mu_shift, w0_f, w_up_f, w0_b, w_up_b, a0, a_up, g_up, k_k, k_a, r_k, ln_x_g, ln_x_b, w_out, norm2_g, w_router_group, b_router_group, w_router_expert, b_router_expert, w_expert_gate, w_expert_up, w_expert_down):
    layer_params = (norm1_g, w_in, q_a_norm_g, w_uq, kv_a_norm_g, w_ukv, q_norm_g, k_norm_g, mu_shift,
                    w0_f, w_up_f, w0_b, w_up_b, a0, a_up, g_up, k_k, k_a, r_k, ln_x_g, ln_x_b, w_out,
                    norm2_g, w_router_group, b_router_group, w_router_expert, b_router_expert,
                    w_expert_gate, w_expert_up, w_expert_down)
    y_prompt, y_sample = x_prompt, x_sample
    max_seq = max(x_prompt.shape[1], x_sample.shape[1])
    for layer in range(norm1_g.shape[0]):
        p = _pack_params([w[layer] for w in layer_params], max_seq)
        y_prompt = _layer(y_prompt, p)
        y_sample = _layer(y_sample, p)
    return (y_prompt, y_sample)
```

```python
import functools
import math

import jax
import jax.numpy as jnp
from jax import lax
from jax.experimental import pallas as pl
from jax.experimental.pallas import tpu as pltpu

D_MODEL = 1024
MLA_HEADS = 8
QK_NOPE = 64
QK_ROPE = 32
QK_HEAD = QK_NOPE + QK_ROPE
V_HEAD = 64
Q_LORA = 384
KV_LORA = 256
ROPE_THETA = 10000.0
RW_HEADS = 8
RW_HEAD = 64
RW_WIDTH = RW_HEADS * RW_HEAD
DECAY_LORA = 64
A_LORA = 64
GATE_LORA = 128
LN_X_EPS = 6.4e-4
MLA_WIDTH = MLA_HEADS * V_HEAD
N_GROUPS = 4
EXPERTS_PER_GROUP = 8
N_EXPERTS = N_GROUPS * EXPERTS_PER_GROUP
D_EXPERT = 256
GROUP_HIDDEN = EXPERTS_PER_GROUP * D_EXPERT
MOE_CHUNK = 512
EXPERTS_PER_CHUNK = MOE_CHUNK // D_EXPERT
RMS_EPS = 1e-6

LANE = 128
HEAD_PAD = LANE
MLA_PAD = MLA_HEADS * HEAD_PAD
RW_COLS_PAD = 3 * RW_WIDTH + 3 * LANE
MLA_COLS_PAD = Q_LORA + KV_LORA + 2 * LANE
D_IN_PAD = MLA_COLS_PAD + RW_COLS_PAD
CHUNK = 64
HG = 4
HGW = HG * RW_HEAD
VMEM_LIMIT = 56 * 1024 * 1024
LOG2E = 1.4426950408889634
NCH = 4
FAST_SOFTMAX_MAX_SHIFT = 100.0

F32 = jnp.float32
BF16 = jnp.bfloat16


def _dot(a, b):
    return jnp.dot(a, b, preferred_element_type=F32)


def _split_dot(a, b_exact):
    hi = a.astype(BF16)
    lo = (a - hi.astype(F32)).astype(BF16)
    return _dot(hi, b_exact) + _dot(lo, b_exact)


def _cparams(sem):
    return pltpu.CompilerParams(dimension_semantics=sem, vmem_limit_bytes=VMEM_LIMIT)


def _proj_kernel(x_ref, g1_ref, win_ref, qag_ref, wuq_ref, kvag_ref, wukv_ref, gq_ref, gk_ref,
                 ct_ref, st_ref, q_out, k_out, v_out, u_out):
    x = x_ref[...]
    h = x * lax.rsqrt(jnp.mean(x * x, axis=-1, keepdims=True) + RMS_EPS) * g1_ref[...]
    z = _dot(h.astype(BF16), win_ref[...])
    u_out[...] = z[:, MLA_COLS_PAD:]

    cq = z[:, :Q_LORA]
    cqn = cq * lax.rsqrt(jnp.mean(cq * cq, axis=-1, keepdims=True) + RMS_EPS) * qag_ref[...]
    q2 = _dot(cqn.astype(BF16), wuq_ref[...])
    ckv = z[:, Q_LORA:Q_LORA + KV_LORA]
    ckvn = ckv * lax.rsqrt(jnp.mean(ckv * ckv, axis=-1, keepdims=True) + RMS_EPS) * kvag_ref[...]
    kv2 = _dot(ckvn.astype(BF16), wukv_ref[...])
    kr_main = z[:, Q_LORA + KV_LORA:Q_LORA + KV_LORA + LANE]
    kr_swap = z[:, Q_LORA + KV_LORA + LANE:MLA_COLS_PAD]

    ct = ct_ref[...]
    st = st_ref[...]
    gq = gq_ref[...]
    gk = gk_ref[...]
    scale = QK_HEAD ** -0.5 * LOG2E
    lane = lax.broadcasted_iota(jnp.int32, (1, LANE), 1)
    ones_col = jnp.where(lane == V_HEAD, 1.0, 0.0).astype(F32)
    shift_col = jnp.where(lane == QK_HEAD, 1.0, 0.0).astype(F32)
    q_shift = shift_col * gq[2:3, 0:1]
    kr_ssq = jnp.sum(kr_main * kr_main, axis=-1, keepdims=True)
    for hd in range(MLA_HEADS):
        sl = slice(hd * HEAD_PAD, (hd + 1) * HEAD_PAD)
        qm = q2[:, sl]
        qs = q2[:, MLA_PAD + hd * HEAD_PAD:MLA_PAD + (hd + 1) * HEAD_PAD]
        rinv = lax.rsqrt(jnp.sum(qm * qm, axis=-1, keepdims=True) * (1.0 / QK_HEAD) + RMS_EPS)
        qh = (qm * gq[0:1] * ct + qs * gq[1:2] * st) * (rinv * scale)
        q_out[:, sl] = (qh - q_shift).astype(BF16)
        kn = kv2[:, sl]
        rinv_k = lax.rsqrt((jnp.sum(kn * kn, axis=-1, keepdims=True) + kr_ssq) * (1.0 / QK_HEAD) + RMS_EPS)
        kh = ((kn + kr_main) * gk[0:1] * ct + kr_swap * gk[1:2] * st) * rinv_k
        k_out[:, sl] = (kh + shift_col).astype(BF16)
        v_out[:, sl] = (kv2[:, MLA_PAD + hd * HEAD_PAD:MLA_PAD + (hd + 1) * HEAD_PAD] + ones_col).astype(BF16)


def _proj(x2, seq, p, tm):
    n = x2.shape[0]
    tm = min(tm, seq)
    nseq_t = seq // tm
    row = lambda i: (i, 0)
    fixed = lambda i: (0, 0)
    pos = lambda i: (i % nseq_t, 0)
    full = lambda a: pl.BlockSpec(a.shape, fixed)
    return pl.pallas_call(
        _proj_kernel,
        grid=(n // tm,),
        in_specs=[pl.BlockSpec((tm, D_MODEL), row), full(p['g1']), full(p['w_in']), full(p['qag']),
                  full(p['w_uq']), full(p['kvag']), full(p['w_ukv']), full(p['gq']), full(p['gk']),
                  pl.BlockSpec((tm, LANE), pos), pl.BlockSpec((tm, LANE), pos)],
        out_specs=[pl.BlockSpec((tm, MLA_PAD), row), pl.BlockSpec((tm, MLA_PAD), row),
                   pl.BlockSpec((tm, MLA_PAD), row), pl.BlockSpec((tm, RW_COLS_PAD), row)],
        out_shape=[jax.ShapeDtypeStruct((n, MLA_PAD), BF16), jax.ShapeDtypeStruct((n, MLA_PAD), BF16),
                   jax.ShapeDtypeStruct((n, MLA_PAD), BF16), jax.ShapeDtypeStruct((n, RW_COLS_PAD), F32)],
        compiler_params=_cparams(("parallel",)),
        name="proj",
    )(x2, p['g1'], p['w_in'], p['qag'], p['w_uq'], p['kvag'], p['w_ukv'], p['gq'], p['gk'],
      p['ct'][:seq], p['st'][:seq])


def _attn_kernel(q_ref, k_ref, v_ref, o_ref, m_ref, acc_ref, *, running_max, tks):
    j = pl.program_id(3)

    @pl.when(j == 0)
    def _():
        if running_max:
            m_ref[...] = jnp.full(m_ref.shape, -jnp.inf, F32)
        acc_ref[...] = jnp.zeros(acc_ref.shape, F32)

    heads = [slice(hh * HEAD_PAD, (hh + 1) * HEAD_PAD) for hh in range(2)]
    nsub = k_ref.shape[0] // tks

    def score_pair(sb):
        rows = slice(sb * tks, (sb + 1) * tks)
        return [lax.dot_general(k_ref[rows, sl], q_ref[:, sl], (((1,), (1,)), ((), ())),
                                preferred_element_type=F32) for sl in heads]

    accs = [acc_ref[hh] for hh in range(2)]
    maxes = [m_ref[hh] for hh in range(2)] if running_max else None
    scores = score_pair(0)
    for sb in range(nsub):
        nxt = score_pair(sb + 1) if sb + 1 < nsub else None
        rows = slice(sb * tks, (sb + 1) * tks)
        for hh, sl in enumerate(heads):
            s = scores[hh]
            if running_max:
                m_new = jnp.maximum(maxes[hh], jnp.max(s, axis=0, keepdims=True))
                p = jnp.exp2(s - m_new)
            else:
                p = jnp.exp2(s)
            pv = lax.dot_general(v_ref[rows, sl], p.astype(BF16), (((0,), (0,)), ((), ())),
                                 preferred_element_type=F32)
            if running_max:
                accs[hh] = jnp.exp2(maxes[hh] - m_new) * accs[hh] + pv
                maxes[hh] = m_new
            else:
                accs[hh] = accs[hh] + pv
        scores = nxt
    for hh in range(2):
        acc_ref[hh] = accs[hh]
        if running_max:
            m_ref[hh] = maxes[hh]

    @pl.when(j == pl.num_programs(3) - 1)
    def _():
        outs = []
        for hh in range(2):
            a = acc_ref[hh]
            outs.append(a[:V_HEAD] / a[V_HEAD:V_HEAD + 1])
        o_ref[...] = jnp.concatenate(outs, axis=0).T.astype(BF16)


def _attn(q, k, v, bsz, seq, fast_softmax):
    return lax.cond(fast_softmax,
                    lambda q, k, v: _attn_call(q, k, v, bsz, seq, TQ, TK, TKS, False),
                    lambda q, k, v: _attn_call(q, k, v, bsz, seq, TQ_MAX, TK_MAX, TKS_MAX, True),
                    q, k, v)


def _attn_call(q, k, v, bsz, seq, tq, tk, tks, running_max):
    tq = min(tq, seq)
    tk = min(tk, seq)
    tks = min(tks, tk)
    nq, nk = seq // tq, seq // tk
    n = q.shape[0]
    return pl.pallas_call(
        functools.partial(_attn_kernel, running_max=running_max, tks=tks),
        grid=(bsz, MLA_HEADS // 2, nq, nk),
        in_specs=[pl.BlockSpec((tq, 2 * HEAD_PAD), lambda b, h, i, j: (b * nq + i, h)),
                  pl.BlockSpec((tk, 2 * HEAD_PAD), lambda b, h, i, j: (b * nk + j, h)),
                  pl.BlockSpec((tk, 2 * HEAD_PAD), lambda b, h, i, j: (b * nk + j, h))],
        out_specs=pl.BlockSpec((tq, 2 * V_HEAD), lambda b, h, i, j: (b * nq + i, h)),
        out_shape=jax.ShapeDtypeStruct((n, MLA_WIDTH), BF16),
        scratch_shapes=[pltpu.VMEM((2, 1, tq), F32), pltpu.VMEM((2, HEAD_PAD, tq), F32)],
        compiler_params=_cparams(("parallel", "parallel", "parallel", "arbitrary")),
        name="attn_max" if running_max else "attn",
    )(q, k, v)


def _rwprep_kernel(u_ref, up_ref, un_ref, mu_ref, wdec_ref, w0_ref, aup_ref, a0_ref, gup_ref,
                   kk_ref, ka_ref, rk_ref, seg_ref,
                   r_out, k_out, v_out, kkn_out, kka_out, lwf_out, lwb_out, g_out, bonus_out,
                   *, tm, seq):
    i = pl.program_id(0)
    u = u_ref[...]
    first = (i * tm) % seq == 0
    last = ((i + 1) * tm) % seq == 0
    prev_row = jnp.where(first, 0.0, up_ref[0, 7:8, :])
    next_row = jnp.where(last, 0.0, un_ref[0, 0:1, :])
    rows = lax.broadcasted_iota(jnp.int32, (tm, 1), 0)
    prev = jnp.where(rows == 0, prev_row, pltpu.roll(u, 1, 0))
    nxt = jnp.where(rows == tm - 1, next_row, pltpu.roll(u, tm - 1, 0))
    us = u + (0.5 * (prev + nxt) - u) * mu_ref[...]

    r = us[:, :RW_WIDTH]
    k = us[:, RW_WIDTH:2 * RW_WIDTH]
    v = us[:, 2 * RW_WIDTH:3 * RW_WIDTH]
    xw = us[:, 3 * RW_WIDTH:3 * RW_WIDTH + LANE]
    xa = us[:, 3 * RW_WIDTH + LANE:3 * RW_WIDTH + 2 * LANE]
    xg = us[:, 3 * RW_WIDTH + 2 * LANE:]

    wpre = w0_ref[...] + _dot(jnp.tanh(xw).astype(BF16), wdec_ref[...])
    wlog = -(jnp.maximum(-wpre, 0.0) + jnp.log1p(jnp.exp(-jnp.abs(wpre)))) - 0.5
    lw = -jnp.exp(wlog)
    lwf_out[...] = lw[:, :RW_WIDTH]
    lwb_out[...] = lw[:, RW_WIDTH:]
    a = jax.nn.sigmoid(a0_ref[...] + _dot(xa.astype(BF16), aup_ref[...]))
    g_out[...] = _dot(jax.nn.sigmoid(xg).astype(BF16), gup_ref[...])

    seg = seg_ref[...]
    kk = k * kk_ref[...]
    kkn = kk / jnp.maximum(jnp.sqrt(_split_dot(kk * kk, seg)), 1e-12)
    k2 = k * (1.0 + (a - 1.0) * ka_ref[...])
    r_out[...] = r
    k_out[...] = k2
    v_out[...] = v
    kkn_out[...] = kkn
    kka_out[...] = kkn * a
    bonus_out[...] = _split_dot(r * k2 * rk_ref[...], seg) * v


def _rwprep(u, seq, p, tm):
    n = u.shape[0]
    tm = min(tm, seq)
    g8 = tm // 8
    u3 = u.reshape(n // 8, 8, RW_COLS_PAD)
    ngrp = n // 8
    row = lambda i: (i, 0)
    fixed = lambda i: (0, 0)
    full = lambda a: pl.BlockSpec(a.shape, fixed)
    out_sds = jax.ShapeDtypeStruct((n, RW_WIDTH), F32)
    out_spec = pl.BlockSpec((tm, RW_WIDTH), row)
    return pl.pallas_call(
        functools.partial(_rwprep_kernel, tm=tm, seq=seq),
        grid=(n // tm,),
        in_specs=[pl.BlockSpec((tm, RW_COLS_PAD), row),
                  pl.BlockSpec((1, 8, RW_COLS_PAD), lambda i: (jnp.maximum(i * g8 - 1, 0), 0, 0)),
                  pl.BlockSpec((1, 8, RW_COLS_PAD), lambda i: (jnp.minimum((i + 1) * g8, ngrp - 1), 0, 0)),
                  full(p['mu']), full(p['w_dec']), full(p['w0']), full(p['a_up']), full(p['a0']),
                  full(p['g_up']), full(p['k_k']), full(p['k_a']), full(p['r_k']), full(p['seg'])],
        out_specs=[out_spec] * 9,
        out_shape=[out_sds] * 9,
        compiler_params=_cparams(("parallel",)),
        name="rwprep",
    )(u, u3, u3, p['mu'], p['w_dec'], p['w0'], p['a_up'], p['a0'], p['g_up'], p['k_k'], p['k_a'],
      p['r_k'], p['seg'])


def _bd(x, bd16):
    return jnp.concatenate([x.astype(BF16)] * HG, axis=0) * bd16


def _chunk_intra(r, k, v, kkn, kka, lw, reverse, consts):
    tri, strict_m, incl_m, eye_p, bd16, bd32, eye32 = consts
    lg = _split_dot3(tri, lw)
    yield
    lgx = lg - lw
    tot = lg[0:1] if reverse else lg[CHUNK - 1:CHUNK]
    gi = jnp.exp(lg)
    ginv = jnp.exp(-lg)
    gend = jnp.exp(tot - lg)
    at = -kkn * jnp.exp(lgx)
    rt = r * gi
    bt = kka * ginv
    kt = k * ginv
    bh = kka * gend
    kh = k * gend

    lhs = jnp.concatenate([at, rt], axis=0).astype(BF16)
    rhs = jnp.concatenate([_bd(bt, bd16), _bd(kt, bd16)], axis=0)
    a_all = lax.dot_general(lhs, rhs, (((1,), (1,)), ((), ())), preferred_element_type=F32)
    yield
    n_ab = a_all[:CHUNK, :HGW] * strict_m
    a_ak = a_all[:CHUNK, HGW:] * strict_m
    a_rb = a_all[CHUNK:, :HGW] * incl_m
    a_rk = a_all[CHUNK:, HGW:] * incl_m

    t = eye_p + n_ab
    nk = _dot(n_ab.astype(BF16), _bd(n_ab, bd16))
    yield
    for _ in range(4):
        both = _dot(jnp.concatenate([nk, t], axis=0).astype(BF16), _bd(nk, bd16))
        yield
        nk = both[:CHUNK]
        t = t + both[CHUNK:]
    tn = _dot(t.astype(BF16), _bd(nk, bd16))
    yield
    t = t + tn

    tb = t.astype(BF16)
    vbd = _bd(v, bd16)
    w = _dot(tb, _bd(at, bd16))
    akv = _dot(a_ak.astype(BF16), vbd)
    yield
    uv = _dot(tb, _bd(akv, bd16))
    arb = a_rb.astype(BF16)
    qpd = _dot(arb, _bd(w, bd16))
    yield
    y_in = _dot(jnp.concatenate([arb, a_rk.astype(BF16)], axis=1),
                jnp.concatenate([_bd(uv, bd16), vbd], axis=0))
    lhs_t = jnp.concatenate([bh, kh], axis=0).astype(BF16)
    rhs_t = jnp.concatenate([jnp.concatenate([w, uv], axis=1),
                             jnp.concatenate([jnp.zeros_like(v), v], axis=1)], axis=0).astype(BF16)
    mc = lax.dot_general(lhs_t, rhs_t, (((0,), (0,)), ((), ())), preferred_element_type=F32)
    yield
    m_bd = mc[:, :HGW] * bd32 + eye32 * jnp.exp(tot)
    c_bd = mc[:, HGW:] * bd32
    return y_in, (rt + qpd).astype(BF16), m_bd.astype(BF16), c_bd


def _interleave(gens):
    results = [None] * len(gens)
    active = list(range(len(gens)))
    while active:
        still = []
        for i in active:
            try:
                next(gens[i])
                still.append(i)
            except StopIteration as stop:
                results[i] = stop.value
        active = still
    return results


def _split_dot3(b_exact, a):
    hi = a.astype(BF16)
    r1 = a - hi.astype(F32)
    mid = r1.astype(BF16)
    lo = (r1 - mid.astype(F32)).astype(BF16)
    return _dot(b_exact, hi) + _dot(b_exact, mid) + _dot(b_exact, lo)


def _rwscan_kernel(rf, kf, vf, nf, af, lf, rb, kb, vb, nb, ab, lb, tri_ref, m64_ref, bd16_ref, bd32_ref,
                   yf_out, yb_out, s_ref):
    c = pl.program_id(1)

    @pl.when(c == 0)
    def _():
        s_ref[...] = jnp.zeros(s_ref.shape, F32)

    nch = rf.shape[0] // CHUNK
    bd16 = bd16_ref[...]
    bd32 = bd32_ref[0]
    eye32 = bd32_ref[1]
    eye_p = m64_ref[4]
    ngrp = RW_HEADS // HG
    chains = []
    gens = []
    for d, refs, y_out in ((0, (rf, kf, vf, nf, af, lf), yf_out), (1, (rb, kb, vb, nb, ab, lb), yb_out)):
        reverse = d == 1
        consts = (tri_ref[d], m64_ref[2 * d], m64_ref[2 * d + 1], eye_p, bd16, bd32, eye32)
        order = list(range(nch - 1, -1, -1) if reverse else range(nch))
        for g in range(ngrp):
            chains.append((d, g, y_out, order))
            for j in order:
                ops = [x[j * CHUNK:(j + 1) * CHUNK, g * HGW:(g + 1) * HGW] for x in refs]
                gens.append(_chunk_intra(*ops, reverse, consts))
    parts = _interleave(gens)

    states = [s_ref[d, g] for d, g, _, _ in chains]
    for step in range(nch):
        for ci, (d, g, y_out, order) in enumerate(chains):
            j = order[step]
            y_in, qp, m_bd, c_bd = parts[ci * nch + step]
            sb = states[ci].astype(BF16)
            y_out[j * CHUNK:(j + 1) * CHUNK, g * HGW:(g + 1) * HGW] = y_in + _dot(qp, sb)
            states[ci] = _dot(m_bd, sb) + c_bd
    for ci, (d, g, _, _) in enumerate(chains):
        s_ref[d, g] = states[ci]


def _rwscan(r, k, v, kkn, kka, lwf, lwb, bsz, seq, p):
    n = r.shape[0]
    rows = min(NCH * CHUNK, seq)
    nb = seq // rows
    fwd = pl.BlockSpec((rows, RW_WIDTH), lambda b, c: (b * nb + c, 0))
    bwd = pl.BlockSpec((rows, RW_WIDTH), lambda b, c: (b * nb + nb - 1 - c, 0))
    full = lambda a: pl.BlockSpec(a.shape, lambda b, c: (0,) * a.ndim)
    sds = jax.ShapeDtypeStruct((n, RW_WIDTH), F32)
    return pl.pallas_call(
        _rwscan_kernel,
        grid=(bsz, nb),
        in_specs=[fwd] * 6 + [bwd] * 6 + [full(p['sc_tri']), full(p['sc_m64']), full(p['sc_bd16']),
                                           full(p['sc_bd32'])],
        out_specs=[fwd, bwd],
        out_shape=[sds, sds],
        scratch_shapes=[pltpu.VMEM((2, RW_HEADS // HG, HGW, HGW), F32)],
        compiler_params=_cparams(("parallel", "arbitrary")),
        name="rwscan",
    )(r, k, v, kkn, kka, lwf, r, k, v, kkn, kka, lwb, p['sc_tri'], p['sc_m64'], p['sc_bd16'], p['sc_bd32'])


def _post_kernel(x_ref, oa_ref, yf_ref, yb_ref, bonus_ref, g_ref, lng_ref, lnb_ref, seg_ref, wout_ref,
                 g2_ref, wr_ref, br_ref, x1_out, hp_out, gid_out):
    seg = seg_ref[...]
    y = yf_ref[...] + yb_ref[...]
    mean = _split_dot(y, seg) * (1.0 / RW_HEAD)
    dlt = y - mean
    var = _split_dot(dlt * dlt, seg) * (1.0 / RW_HEAD)
    yn = dlt * lax.rsqrt(var + LN_X_EPS) * lng_ref[...] + lnb_ref[...]
    ob = (yn + bonus_ref[...]) * g_ref[...]
    x1 = (x_ref[...] + _dot(oa_ref[...], wout_ref[:MLA_WIDTH, :])
          + _dot(ob.astype(BF16), wout_ref[MLA_WIDTH:, :]))
    x1_out[...] = x1
    h2 = x1 * lax.rsqrt(jnp.mean(x1 * x1, axis=-1, keepdims=True) + RMS_EPS) * g2_ref[...]
    h2b = h2.astype(BF16)
    bits = lax.bitcast_convert_type(h2b.astype(F32), jnp.uint32)
    hp_out[...] = (bits[:, :D_MODEL // 2] >> 16) | (bits[:, D_MODEL // 2:] & jnp.uint32(0xFFFF0000))
    logits = _dot(h2b, wr_ref[...]) + br_ref[...]
    _, gidx, _ = _group_choice(logits)
    gid_out[...] = jnp.broadcast_to(gidx, gid_out.shape)


def _group_choice(logits):
    lane_i = lax.broadcasted_iota(jnp.int32, logits.shape, 1)
    lane = lane_i.astype(F32)
    is_g = (lane_i >= N_EXPERTS) & (lane_i < N_EXPERTS + N_GROUPS)
    gl = jnp.where(is_g, logits, -jnp.inf)
    gmax = jnp.max(gl, axis=-1, keepdims=True)
    gidx = jnp.min(jnp.where(gl == gmax, lane, 1e9), axis=-1, keepdims=True) - N_EXPERTS
    p_g = 1.0 / jnp.sum(jnp.where(is_g, jnp.exp(gl - gmax), 0.0), axis=-1, keepdims=True)
    return p_g, gidx, lane


def _post(x2, oa, yf, yb, bonus, g, p, tm):
    n = x2.shape[0]
    tm = min(tm, n)
    row = lambda i: (i, 0)
    fixed = lambda i: (0, 0)
    full = lambda a: pl.BlockSpec(a.shape, fixed)
    rw = pl.BlockSpec((tm, RW_WIDTH), row)
    return pl.pallas_call(
        _post_kernel,
        grid=(n // tm,),
        in_specs=[pl.BlockSpec((tm, D_MODEL), row), pl.BlockSpec((tm, MLA_WIDTH), row), rw, rw, rw, rw,
                  full(p['ln_g']), full(p['ln_b']), full(p['seg']), full(p['w_out']), full(p['g2']),
                  full(p['w_r']), full(p['b_r'])],
        out_specs=[pl.BlockSpec((tm, D_MODEL), row), pl.BlockSpec((tm, D_MODEL // 2), row),
                   pl.BlockSpec((tm, LANE), row)],
        out_shape=[jax.ShapeDtypeStruct((n, D_MODEL), F32), jax.ShapeDtypeStruct((n, D_MODEL // 2), jnp.uint32),
                   jax.ShapeDtypeStruct((n, LANE), F32)],
        compiler_params=_cparams(("parallel",)),
        name="post",
    )(x2, oa, yf, yb, bonus, g, p['ln_g'], p['ln_b'], p['seg'], p['w_out'], p['g2'], p['w_r'], p['b_r'])


def _route_plan(gid, tm, npt):
    g = gid[:, 0].astype(jnp.int32)
    onehot = (g[:, None] == jnp.arange(N_GROUPS, dtype=jnp.int32)[None, :]).astype(jnp.int32)
    csum = jnp.cumsum(onehot, axis=0)
    rank = jnp.sum(csum * onehot, axis=1) - 1
    ntile = (csum[-1] + tm - 1) // tm
    tile_end = jnp.cumsum(ntile)
    off = (tile_end - ntile) * tm
    pos = jnp.sum(onehot * off[None, :], axis=1) + rank
    n_active = tile_end[-1]
    t = jnp.minimum(jnp.arange(npt, dtype=jnp.int32), n_active - 1)
    tile_gid = jnp.sum((t[:, None] >= tile_end[None, :]).astype(jnp.int32), axis=1)
    meta = jnp.concatenate([n_active[None], tile_gid]).astype(jnp.int32)
    return pos.astype(jnp.int32), meta


def _row_dma_loops(n_rows, make_copy):
    def issue(h, carry):
        make_copy(2 * h).start(priority=0)
        make_copy(2 * h + 1).start(priority=1)
        return carry
    lax.fori_loop(0, n_rows // 2, issue, 0, unroll=4)

    def drain(r, carry):
        make_copy(0).wait()
        return carry
    lax.fori_loop(0, n_rows, drain, 0, unroll=8)


def _scatter_kernel(pos_ref, src_ref, init_ref, dst_ref, sem):
    _row_dma_loops(src_ref.shape[0], lambda r: pltpu.make_async_copy(
        src_ref.at[pl.ds(r, 1)], dst_ref.at[pl.ds(pos_ref[0, 0, r], 1)], sem))


def _scatter_rows(pos, src, n_dst, tms):
    n, w = src.shape
    tms = min(tms, n)
    return pl.pallas_call(
        _scatter_kernel,
        grid=(n // tms,),
        in_specs=[pl.BlockSpec((1, 1, tms), lambda i: (i, 0, 0), memory_space=pltpu.SMEM),
                  pl.BlockSpec((tms, w), lambda i: (i, 0)), pl.BlockSpec(memory_space=pl.ANY)],
        out_specs=pl.BlockSpec(memory_space=pl.ANY),
        out_shape=jax.ShapeDtypeStruct((n_dst, w), src.dtype),
        scratch_shapes=[pltpu.SemaphoreType.DMA(())],
        input_output_aliases={2: 0},
        compiler_params=_cparams(("arbitrary",)),
        name="moe_scatter",
    )(pos.reshape(n // tms, 1, tms), src, jnp.zeros((n_dst, w), src.dtype))


def _moe_group_kernel(meta_ref, xs_ref, wr_ref, br_ref, wg_ref, wu_ref, wd_ref, ys_ref):
    i = pl.program_id(0)

    @pl.when(i >= meta_ref[0])
    def _():
        ys_ref[...] = jnp.zeros(ys_ref.shape, F32)

    @pl.when(i < meta_ref[0])
    def _():
        tm = xs_ref.shape[0]
        word = xs_ref[...]
        h2 = jnp.concatenate([lax.bitcast_convert_type(word << 16, F32),
                              lax.bitcast_convert_type(word & jnp.uint32(0xFFFF0000), F32)],
                             axis=1).astype(BF16)
        logits = _dot(h2, wr_ref[...]) + br_ref[...]
        p_g, _, lane = _group_choice(logits)
        first = (meta_ref[1 + i] * EXPERTS_PER_GROUP).astype(F32)
        in_grp = (lane >= first) & (lane < first + EXPERTS_PER_GROUP)
        el = jnp.where(in_grp, logits, -jnp.inf)
        v1 = jnp.max(el, axis=-1, keepdims=True)
        i1 = jnp.min(jnp.where(el == v1, lane, 1e9), axis=-1, keepdims=True)
        el2 = jnp.where(lane == i1, -jnp.inf, el)
        v2 = jnp.max(el2, axis=-1, keepdims=True)
        i2 = jnp.min(jnp.where(el2 == v2, lane, 1e9), axis=-1, keepdims=True)
        e2 = jnp.exp(v2 - v1)
        den = 1.0 + e2
        w1 = (1.0 / den) * p_g
        w2 = (e2 / den) * p_g

        def gate_cols(c):
            cols = []
            for k in range(c * EXPERTS_PER_CHUNK, (c + 1) * EXPERTS_PER_CHUNK):
                gk = jnp.where(i1 == first + k, w1, 0.0) + jnp.where(i2 == first + k, w2, 0.0)
                cols.append(jnp.broadcast_to(gk, (tm, D_EXPERT)))
            return jnp.concatenate(cols, axis=1)

        acc = None
        pend = None
        nchunk = GROUP_HIDDEN // MOE_CHUNK
        for c in range(nchunk + 1):
            cur = None
            if c < nchunk:
                cs = slice(c * MOE_CHUNK, (c + 1) * MOE_CHUNK)
                cur = (_dot(h2, wg_ref[0, :, cs]), _dot(h2, wu_ref[0, :, cs]), c)
            if pend is not None:
                hg, hu, pc = pend
                act = hg * jax.nn.sigmoid(hg) * hu * gate_cols(pc)
                part = _dot(act.astype(BF16), wd_ref[0, pc * MOE_CHUNK:(pc + 1) * MOE_CHUNK, :])
                acc = part if acc is None else acc + part
            pend = cur
        ys_ref[...] = acc


def _moe_grouped(xs, meta, p, tm, npt):
    def tile(i, m):
        return (jnp.minimum(i, m[0] - 1), 0)

    def group(i, m):
        return (m[1 + i], 0, 0)

    grid_spec = pltpu.PrefetchScalarGridSpec(
        num_scalar_prefetch=1,
        grid=(npt,),
        in_specs=[pl.BlockSpec((tm, D_MODEL // 2), tile),
                  pl.BlockSpec(p['w_r'].shape, lambda i, m: (0, 0)),
                  pl.BlockSpec(p['b_r'].shape, lambda i, m: (0, 0)),
                  pl.BlockSpec((1, D_MODEL, GROUP_HIDDEN), group),
                  pl.BlockSpec((1, D_MODEL, GROUP_HIDDEN), group),
                  pl.BlockSpec((1, GROUP_HIDDEN, D_MODEL), group)],
        out_specs=pl.BlockSpec((tm, D_MODEL), lambda i, m: (i, 0)),
    )
    return pl.pallas_call(
        _moe_group_kernel,
        grid_spec=grid_spec,
        out_shape=jax.ShapeDtypeStruct((npt * tm, D_MODEL), F32),
        compiler_params=_cparams(("arbitrary",)),
        name="moe_group",
    )(meta, xs, p['w_r'], p['b_r'], p['wg'], p['wu'], p['wd'])


def _gather_kernel(pos_ref, x1_ref, ys_ref, o_ref, buf, sem):
    _row_dma_loops(buf.shape[0], lambda r: pltpu.make_async_copy(
        ys_ref.at[pl.ds(pos_ref[0, 0, r], 1)], buf.at[pl.ds(r, 1)], sem))
    o_ref[...] = x1_ref[...] + buf[...]


def _gather_add(pos, x1, ys, tms):
    n, d = x1.shape
    tms = min(tms, n)
    return pl.pallas_call(
        _gather_kernel,
        grid=(n // tms,),
        in_specs=[pl.BlockSpec((1, 1, tms), lambda i: (i, 0, 0), memory_space=pltpu.SMEM),
                  pl.BlockSpec((tms, d), lambda i: (i, 0)),
                  pl.BlockSpec(memory_space=pl.ANY)],
        out_specs=pl.BlockSpec((tms, d), lambda i: (i, 0)),
        out_shape=jax.ShapeDtypeStruct((n, d), F32),
        scratch_shapes=[pltpu.VMEM((tms, d), F32), pltpu.SemaphoreType.DMA(())],
        compiler_params=_cparams(("arbitrary",)),
        name="moe_gather",
    )(pos.reshape(n // tms, 1, tms), x1, ys)


def _head_pad_cols(w, head_w, lo, hi, dst, width=HEAD_PAD, perm=None):
    rows = w.shape[0]
    wh = w.reshape(rows, -1, head_w)[:, :, lo:hi]
    if perm is not None:
        wh = wh[:, :, perm]
    out = jnp.zeros((rows, wh.shape[1], width), w.dtype)
    return out.at[:, :, dst:dst + (hi - lo)].set(wh).reshape(rows, -1)


def _pack_params(lp, max_seq):
    (norm1_g, w_in, q_a_norm_g, w_uq, kv_a_norm_g, w_ukv, q_norm_g, k_norm_g, mu_shift, w0_f, w_up_f,
     w0_b, w_up_b, a0, a_up, g_up, k_k, k_a, r_k, ln_x_g, ln_x_b, w_out, norm2_g, w_router_group,
     b_router_group, w_router_expert, b_router_expert, w_expert_gate, w_expert_up, w_expert_down) = lp
    half = QK_ROPE // 2
    swap = jnp.concatenate([jnp.arange(half, QK_ROPE), jnp.arange(half)])
    zcol = lambda rows, n: jnp.zeros((rows, n), F32)
    p = {}
    c0 = Q_LORA + KV_LORA
    w_kr = w_in[:, c0:c0 + QK_ROPE]
    u0 = c0 + QK_ROPE
    wu_ = w_in[:, u0:]
    d = D_MODEL
    w_in_p = jnp.concatenate([
        w_in[:, :c0],
        zcol(d, QK_NOPE), w_kr, zcol(d, LANE - QK_HEAD),
        zcol(d, QK_NOPE), w_kr[:, swap], zcol(d, LANE - QK_HEAD),
        wu_[:, :3 * RW_WIDTH + 2 * DECAY_LORA + A_LORA], zcol(d, LANE - A_LORA),
        wu_[:, 3 * RW_WIDTH + 2 * DECAY_LORA + A_LORA:]], axis=1)
    p['w_in'] = w_in_p.astype(BF16)
    mu = mu_shift[None, :]
    p['mu'] = jnp.concatenate([mu[:, :3 * RW_WIDTH + 2 * DECAY_LORA + A_LORA], zcol(1, LANE - A_LORA),
                               mu[:, 3 * RW_WIDTH + 2 * DECAY_LORA + A_LORA:]], axis=1)
    p['g1'] = norm1_g[None, :]
    p['qag'] = q_a_norm_g[None, :]
    p['kvag'] = kv_a_norm_g[None, :]
    p['w_uq'] = jnp.concatenate([
        _head_pad_cols(w_uq, QK_HEAD, 0, QK_HEAD, 0),
        _head_pad_cols(w_uq, QK_HEAD, QK_NOPE, QK_HEAD, QK_NOPE, perm=swap)], axis=1).astype(BF16)
    p['w_ukv'] = jnp.concatenate([
        _head_pad_cols(w_ukv, QK_NOPE + V_HEAD, 0, QK_NOPE, 0),
        _head_pad_cols(w_ukv, QK_NOPE + V_HEAD, QK_NOPE, QK_NOPE + V_HEAD, 0)], axis=1).astype(BF16)

    def gain_rows(g):
        main = jnp.concatenate([g, jnp.zeros((LANE - QK_HEAD,), F32)])
        swp = jnp.concatenate([jnp.zeros((QK_NOPE,), F32), g[QK_NOPE:][swap], jnp.zeros((LANE - QK_HEAD,), F32)])
        return jnp.stack([main, swp])
    kbound = math.sqrt(QK_HEAD) * jnp.max(jnp.abs(k_norm_g))
    qbound = jnp.max(jnp.abs(q_norm_g)) * LOG2E
    p['gq'] = jnp.concatenate([gain_rows(q_norm_g), jnp.full((1, LANE), qbound * kbound, F32)], axis=0)
    p['gk'] = gain_rows(k_norm_g)
    p['fast_softmax'] = 2.0 * qbound * kbound <= FAST_SOFTMAX_MAX_SHIFT
    inv_freq = 1.0 / (ROPE_THETA ** (jnp.arange(half, dtype=F32) / half))
    ang = jnp.arange(max_seq, dtype=F32)[:, None] * inv_freq[None, :]
    cos, sin = jnp.cos(ang), jnp.sin(ang)
    zpad = jnp.zeros((max_seq, LANE - QK_HEAD), F32)
    p['ct'] = jnp.concatenate([jnp.ones((max_seq, QK_NOPE), F32), cos, cos, zpad], axis=1)
    p['st'] = jnp.concatenate([jnp.zeros((max_seq, QK_NOPE), F32), -sin, sin, zpad], axis=1)

    zl = jnp.zeros((DECAY_LORA, RW_WIDTH), F32)
    p['w_dec'] = jnp.concatenate([jnp.concatenate([w_up_f, zl], axis=1),
                                  jnp.concatenate([zl, w_up_b], axis=1)], axis=0).astype(BF16)
    p['w0'] = jnp.concatenate([w0_f, w0_b])[None, :]
    p['a_up'] = jnp.concatenate([a_up, jnp.zeros((LANE - A_LORA, RW_WIDTH), F32)], axis=0).astype(BF16)
    p['a0'] = a0[None, :]
    p['g_up'] = g_up.astype(BF16)
    p['k_k'] = k_k[None, :]
    p['k_a'] = k_a[None, :]
    p['r_k'] = r_k.reshape(1, RW_WIDTH)
    hid = jnp.arange(RW_WIDTH) // RW_HEAD
    p['seg'] = (hid[:, None] == hid[None, :]).astype(BF16)
    ti = jnp.arange(CHUNK)[:, None]
    tj = jnp.arange(CHUNK)[None, :]
    p['sc_tri'] = jnp.stack([tj <= ti, tj >= ti]).astype(BF16)
    ps = (jnp.arange(HGW) % CHUNK)[None, :]
    p['sc_m64'] = jnp.stack([ps < ti, ps <= ti, ps > ti, ps >= ti, ps == ti]).astype(F32)
    blk = jnp.arange(HGW) // RW_HEAD
    same = blk[:, None] == blk[None, :]
    p['sc_bd16'] = same.astype(BF16)
    p['sc_bd32'] = jnp.stack([same, jnp.eye(HGW, dtype=bool)]).astype(F32)
    p['ln_g'] = ln_x_g[None, :]
    p['ln_b'] = ln_x_b[None, :]
    p['w_out'] = w_out.astype(BF16)
    p['g2'] = norm2_g[None, :]
    p['w_r'] = jnp.concatenate([w_router_expert, w_router_group,
                                zcol(d, LANE - N_EXPERTS - N_GROUPS)], axis=1).astype(BF16)
    p['b_r'] = jnp.concatenate([b_router_expert, b_router_group,
                                jnp.zeros((LANE - N_EXPERTS - N_GROUPS,), F32)])[None, :]
    def group_cols(w):
        w = w.astype(BF16).reshape(N_GROUPS, EXPERTS_PER_GROUP, d, D_EXPERT)
        return jnp.transpose(w, (0, 2, 1, 3)).reshape(N_GROUPS, d, GROUP_HIDDEN)
    p['wg'] = group_cols(w_expert_gate)
    p['wu'] = group_cols(w_expert_up)
    p['wd'] = w_expert_down.astype(BF16).reshape(N_GROUPS, GROUP_HIDDEN, d)
    return p


TM_PROJ = 512
TQ = 1024
TK = 4096
TKS = 2048
TQ_MAX = 1024
TK_MAX = 2048
TKS_MAX = 1024
TM_PREP = 512
TM_POST = 512
TM_MOE = 512
TM_ROWS = 512


def _layer(x, p):
    bsz, seq, d = x.shape
    n = bsz * seq
    x2 = x.reshape(n, d)
    q, k, v, u = _proj(x2, seq, p, TM_PROJ)
    oa = _attn(q, k, v, bsz, seq, p['fast_softmax'])
    r, k2, vv, kkn, kka, lwf, lwb, g, bonus = _rwprep(u, seq, p, TM_PREP)
    yf, yb = _rwscan(r, k2, vv, kkn, kka, lwf, lwb, bsz, seq, p)
    x1, hp, gid = _post(x2, oa, yf, yb, bonus, g, p, TM_POST)
    tm = min(TM_MOE, n)
    npt = n // tm + N_GROUPS
    pos, meta = _route_plan(gid, tm, npt)
    xs = _scatter_rows(pos, hp, npt * tm, TM_ROWS)
    ys = _moe_grouped(xs, meta, p, tm, npt)
    out = _gather_add(pos, x1, ys, TM_ROWS)
    return out.reshape(bsz, seq, d)


def kernel(x_prompt, x_sample, norm1_g, w_in, q_a_norm_g, w_uq, kv_a_norm_g, w_ukv, q_norm_g, k_norm_g, mu_shift, w0_f, w_up_f, w0_b, w_up_b, a0, a_up, g_up, k_k, k_a, r_k, ln_x_g, ln_x_b, w_out, norm2_g, w_router_group, b_router_group, w_router_expert, b_router_expert, w_expert_gate, w_expert_up, w_expert_down):
    layer_params = (norm1_g, w_in, q_a_norm_g, w_uq, kv_a_norm_g, w_ukv, q_norm_g, k_norm_g, mu_shift,
                    w0_f, w_up_f, w0_b, w_up_b, a0, a_up, g_up, k_k, k_a, r_k, ln_x_g, ln_x_b, w_out,
                    norm2_g, w_router_group, b_router_group, w_router_expert, b_router_expert,
                    w_expert_gate, w_expert_up, w_expert_down)
    y_prompt, y_sample = x_prompt, x_sample
    max_seq = max(x_prompt.shape[1], x_sample.shape[1])
    for layer in range(norm1_g.shape[0]):
        p = _pack_params([w[layer] for w in layer_params], max_seq)
        y_prompt = _layer(y_prompt, p)
        y_sample = _layer(y_sample, p)
    return (y_prompt, y_sample)
```

```python
import functools
import math

import jax
import jax.numpy as jnp
from jax import lax
from jax.experimental import pallas as pl
from jax.experimental.pallas import tpu as pltpu

D_MODEL = 1024
MLA_HEADS = 8
QK_NOPE = 64
QK_ROPE = 32
QK_HEAD = QK_NOPE + QK_ROPE
V_HEAD = 64
Q_LORA = 384
KV_LORA = 256
ROPE_THETA = 10000.0
RW_HEADS = 8
RW_HEAD = 64
RW_WIDTH = RW_HEADS * RW_HEAD
DECAY_LORA = 64
A_LORA = 64
GATE_LORA = 128
LN_X_EPS = 6.4e-4
MLA_WIDTH = MLA_HEADS * V_HEAD
N_GROUPS = 4
EXPERTS_PER_GROUP = 8
N_EXPERTS = N_GROUPS * EXPERTS_PER_GROUP
D_EXPERT = 256
MOE_EPC = 2
RMS_EPS = 1e-6

LANE = 128
HEAD_PAD = LANE
MLA_PAD = MLA_HEADS * HEAD_PAD
RW_COLS_PAD = 3 * RW_WIDTH + 3 * LANE
MLA_COLS_PAD = Q_LORA + KV_LORA + 2 * LANE
D_IN_PAD = MLA_COLS_PAD + RW_COLS_PAD
CHUNK = 64
HG = 4
HGW = HG * RW_HEAD
VMEM_LIMIT = 56 * 1024 * 1024
LOG2E = 1.4426950408889634
NCH = 4
FAST_SOFTMAX_MAX_SHIFT = 100.0

F32 = jnp.float32
BF16 = jnp.bfloat16


def _dot(a, b):
    return jnp.dot(a, b, preferred_element_type=F32)


def _split_dot(a, b_exact):
    hi = a.astype(BF16)
    lo = (a - hi.astype(F32)).astype(BF16)
    return _dot(hi, b_exact) + _dot(lo, b_exact)


def _cparams(sem):
    return pltpu.CompilerParams(dimension_semantics=sem, vmem_limit_bytes=VMEM_LIMIT)


def _proj_kernel(x_ref, xp_ref, xn_ref, g1_ref, win_ref, qag_ref, wuq_ref, kvag_ref, wukv_ref, gq_ref, gk_ref,
                 ct_ref, st_ref, *rw_refs, tm, seq):
    q_out, k_out, v_out = rw_refs[10:13]
    i = pl.program_id(0)
    x = jnp.concatenate([xp_ref[0], x_ref[...], xn_ref[0]], axis=0)
    h = x * lax.rsqrt(jnp.mean(x * x, axis=-1, keepdims=True) + RMS_EPS) * g1_ref[...]
    ze = _dot(h.astype(BF16), win_ref[...])
    z = ze[8:tm + 8]
    first = (i * tm) % seq == 0
    last = ((i + 1) * tm) % seq == 0
    prev_row = jnp.where(first, 0.0, ze[7:8, MLA_COLS_PAD:])
    next_row = jnp.where(last, 0.0, ze[tm + 8:tm + 9, MLA_COLS_PAD:])
    _rw_token_maps(z[:, MLA_COLS_PAD:], prev_row, next_row, *rw_refs[:10], *rw_refs[13:])

    cq = z[:, :Q_LORA]
    cqn = cq * lax.rsqrt(jnp.mean(cq * cq, axis=-1, keepdims=True) + RMS_EPS) * qag_ref[...]
    q2 = _dot(cqn.astype(BF16), wuq_ref[...])
    ckv = z[:, Q_LORA:Q_LORA + KV_LORA]
    ckvn = ckv * lax.rsqrt(jnp.mean(ckv * ckv, axis=-1, keepdims=True) + RMS_EPS) * kvag_ref[...]
    kv2 = _dot(ckvn.astype(BF16), wukv_ref[...])
    kr_main = z[:, Q_LORA + KV_LORA:Q_LORA + KV_LORA + LANE]
    kr_swap = z[:, Q_LORA + KV_LORA + LANE:MLA_COLS_PAD]

    ct = ct_ref[...]
    st = st_ref[...]
    gq = gq_ref[...]
    gk = gk_ref[...]
    scale = QK_HEAD ** -0.5 * LOG2E
    lane = lax.broadcasted_iota(jnp.int32, (1, LANE), 1)
    ones_col = jnp.where(lane == V_HEAD, 1.0, 0.0).astype(F32)
    shift_col = jnp.where(lane == QK_HEAD, 1.0, 0.0).astype(F32)
    q_shift = shift_col * gq[2:3, 0:1]
    kr_ssq = jnp.sum(kr_main * kr_main, axis=-1, keepdims=True)
    for hd in range(MLA_HEADS):
        sl = slice(hd * HEAD_PAD, (hd + 1) * HEAD_PAD)
        qm = q2[:, sl]
        qs = q2[:, MLA_PAD + hd * HEAD_PAD:MLA_PAD + (hd + 1) * HEAD_PAD]
        rinv = lax.rsqrt(jnp.sum(qm * qm, axis=-1, keepdims=True) * (1.0 / QK_HEAD) + RMS_EPS)
        qh = (qm * gq[0:1] * ct + qs * gq[1:2] * st) * (rinv * scale)
        q_out[:, sl] = (qh - q_shift).astype(BF16)
        kn = kv2[:, sl]
        rinv_k = lax.rsqrt((jnp.sum(kn * kn, axis=-1, keepdims=True) + kr_ssq) * (1.0 / QK_HEAD) + RMS_EPS)
        kh = ((kn + kr_main) * gk[0:1] * ct + kr_swap * gk[1:2] * st) * rinv_k
        k_out[:, sl] = (kh + shift_col).astype(BF16)
        v_out[:, sl] = (kv2[:, MLA_PAD + hd * HEAD_PAD:MLA_PAD + (hd + 1) * HEAD_PAD] + ones_col).astype(BF16)


def _proj(x2, seq, p, tm):
    n = x2.shape[0]
    tm = min(tm, seq)
    nseq_t = seq // tm
    row = lambda i: (i, 0)
    fixed = lambda i: (0, 0)
    pos = lambda i: (i % nseq_t, 0)
    full = lambda a: pl.BlockSpec(a.shape, fixed)
    g8 = tm // 8
    ngrp = n // 8
    x3 = x2.reshape(ngrp, 8, D_MODEL)
    rw_names = ('mu', 'w_dec', 'w0', 'a_up', 'a0', 'g_up', 'k_k', 'k_a', 'r_k', 'seg')
    mla_sds = jax.ShapeDtypeStruct((n, MLA_PAD), BF16)
    rw_sds = jax.ShapeDtypeStruct((n, RW_WIDTH), F32)
    return pl.pallas_call(
        functools.partial(_proj_kernel, tm=tm, seq=seq),
        grid=(n // tm,),
        in_specs=[pl.BlockSpec((tm, D_MODEL), row),
                  pl.BlockSpec((1, 8, D_MODEL), lambda i: (jnp.maximum(i * g8 - 1, 0), 0, 0)),
                  pl.BlockSpec((1, 8, D_MODEL), lambda i: (jnp.minimum((i + 1) * g8, ngrp - 1), 0, 0)),
                  full(p['g1']), full(p['w_in']), full(p['qag']),
                  full(p['w_uq']), full(p['kvag']), full(p['w_ukv']), full(p['gq']), full(p['gk']),
                  pl.BlockSpec((tm, LANE), pos), pl.BlockSpec((tm, LANE), pos)]
                 + [full(p[k]) for k in rw_names],
        out_specs=[pl.BlockSpec((tm, MLA_PAD), row)] * 3 + [pl.BlockSpec((tm, RW_WIDTH), row)] * 9,
        out_shape=[mla_sds] * 3 + [rw_sds] * 9,
        compiler_params=_cparams(("parallel",)),
        name="proj",
    )(x2, x3, x3, p['g1'], p['w_in'], p['qag'], p['w_uq'], p['kvag'], p['w_ukv'], p['gq'], p['gk'],
      p['ct'][:seq], p['st'][:seq], *[p[k] for k in rw_names])


def _attn_kernel(q_ref, k_ref, v_ref, o_ref, m_ref, acc_ref, *, running_max, tks):
    j = pl.program_id(3)

    @pl.when(j == 0)
    def _():
        if running_max:
            m_ref[...] = jnp.full(m_ref.shape, -jnp.inf, F32)
        acc_ref[...] = jnp.zeros(acc_ref.shape, F32)

    heads = [slice(hh * HEAD_PAD, (hh + 1) * HEAD_PAD) for hh in range(2)]
    nsub = k_ref.shape[0] // tks

    def score_pair(sb):
        rows = slice(sb * tks, (sb + 1) * tks)
        return [lax.dot_general(k_ref[rows, sl], q_ref[:, sl], (((1,), (1,)), ((), ())),
                                preferred_element_type=F32) for sl in heads]

    accs = [acc_ref[hh] for hh in range(2)]
    maxes = [m_ref[hh] for hh in range(2)] if running_max else None
    scores = score_pair(0)
    for sb in range(nsub):
        nxt = score_pair(sb + 1) if sb + 1 < nsub else None
        rows = slice(sb * tks, (sb + 1) * tks)
        for hh, sl in enumerate(heads):
            s = scores[hh]
            if running_max:
                m_new = jnp.maximum(maxes[hh], jnp.max(s, axis=0, keepdims=True))
                p = jnp.exp2(s - m_new)
            else:
                p = jnp.exp2(s)
            pv = lax.dot_general(v_ref[rows, sl], p.astype(BF16), (((0,), (0,)), ((), ())),
                                 preferred_element_type=F32)
            if running_max:
                accs[hh] = jnp.exp2(maxes[hh] - m_new) * accs[hh] + pv
                maxes[hh] = m_new
            else:
                accs[hh] = accs[hh] + pv
        scores = nxt
    for hh in range(2):
        acc_ref[hh] = accs[hh]
        if running_max:
            m_ref[hh] = maxes[hh]

    @pl.when(j == pl.num_programs(3) - 1)
    def _():
        outs = []
        for hh in range(2):
            a = acc_ref[hh]
            outs.append(a[:V_HEAD] / a[V_HEAD:V_HEAD + 1])
        o_ref[...] = jnp.concatenate(outs, axis=0).T.astype(BF16)


def _attn(q, k, v, bsz, seq, fast_softmax):
    return lax.cond(fast_softmax,
                    lambda q, k, v: _attn_call(q, k, v, bsz, seq, TQ, TK, TKS, False),
                    lambda q, k, v: _attn_call(q, k, v, bsz, seq, TQ_MAX, TK_MAX, TKS_MAX, True),
                    q, k, v)


def _attn_call(q, k, v, bsz, seq, tq, tk, tks, running_max):
    tq = min(tq, seq)
    tk = min(tk, seq)
    tks = min(tks, tk)
    nq, nk = seq // tq, seq // tk
    n = q.shape[0]
    return pl.pallas_call(
        functools.partial(_attn_kernel, running_max=running_max, tks=tks),
        grid=(bsz, MLA_HEADS // 2, nq, nk),
        in_specs=[pl.BlockSpec((tq, 2 * HEAD_PAD), lambda b, h, i, j: (b * nq + i, h)),
                  pl.BlockSpec((tk, 2 * HEAD_PAD), lambda b, h, i, j: (b * nk + j, h)),
                  pl.BlockSpec((tk, 2 * HEAD_PAD), lambda b, h, i, j: (b * nk + j, h))],
        out_specs=pl.BlockSpec((tq, 2 * V_HEAD), lambda b, h, i, j: (b * nq + i, h)),
        out_shape=jax.ShapeDtypeStruct((n, MLA_WIDTH), BF16),
        scratch_shapes=[pltpu.VMEM((2, 1, tq), F32), pltpu.VMEM((2, HEAD_PAD, tq), F32)],
        compiler_params=_cparams(("parallel", "parallel", "parallel", "arbitrary")),
        name="attn_max" if running_max else "attn",
    )(q, k, v)


def _rw_token_maps(u, prev_row, next_row, mu_ref, wdec_ref, w0_ref, aup_ref, a0_ref, gup_ref,
                   kk_ref, ka_ref, rk_ref, seg_ref,
                   r_out, k_out, v_out, kkn_out, kka_out, lwf_out, lwb_out, g_out, bonus_out):
    tm = u.shape[0]
    rows = lax.broadcasted_iota(jnp.int32, (tm, 1), 0)
    prev = jnp.where(rows == 0, prev_row, pltpu.roll(u, 1, 0))
    nxt = jnp.where(rows == tm - 1, next_row, pltpu.roll(u, tm - 1, 0))
    us = u + (0.5 * (prev + nxt) - u) * mu_ref[...]

    r = us[:, :RW_WIDTH]
    k = us[:, RW_WIDTH:2 * RW_WIDTH]
    v = us[:, 2 * RW_WIDTH:3 * RW_WIDTH]
    xw = us[:, 3 * RW_WIDTH:3 * RW_WIDTH + LANE]
    xa = us[:, 3 * RW_WIDTH + LANE:3 * RW_WIDTH + 2 * LANE]
    xg = us[:, 3 * RW_WIDTH + 2 * LANE:]

    wpre = w0_ref[...] + _dot(jnp.tanh(xw).astype(BF16), wdec_ref[...])
    wlog = -(jnp.maximum(-wpre, 0.0) + jnp.log(1.0 + jnp.exp(-jnp.abs(wpre)))) - 0.5
    lw = -jnp.exp(wlog)
    lwf_out[...] = lw[:, :RW_WIDTH]
    lwb_out[...] = lw[:, RW_WIDTH:]
    a = jax.nn.sigmoid(a0_ref[...] + _dot(xa.astype(BF16), aup_ref[...]))
    g_out[...] = _dot(jax.nn.sigmoid(xg).astype(BF16), gup_ref[...])

    seg = seg_ref[...]
    kk = k * kk_ref[...]
    kkn = kk / jnp.maximum(jnp.sqrt(_split_dot(kk * kk, seg)), 1e-12)
    k2 = k * (1.0 + (a - 1.0) * ka_ref[...])
    r_out[...] = r
    k_out[...] = k2
    v_out[...] = v
    kkn_out[...] = kkn
    kka_out[...] = kkn * a
    bonus_out[...] = _split_dot(r * k2 * rk_ref[...], seg) * v


def _bd(x, bd16):
    return jnp.concatenate([x.astype(BF16)] * HG, axis=0) * bd16


def _chunk_intra(r, k, v, kkn, kka, lw, reverse, consts):
    tri, strict_m, incl_m, eye_p, bd16, bd32, eye32 = consts
    lg = _split_dot_left(tri, lw)
    yield
    lgx = lg - lw
    tot = lg[0:1] if reverse else lg[CHUNK - 1:CHUNK]
    gi = jnp.exp(lg)
    ginv = jnp.exp(-lg)
    gend = jnp.exp(tot - lg)
    at = -kkn * jnp.exp(lgx)
    rt = r * gi
    bt = kka * ginv
    kt = k * ginv
    bh = kka * gend
    kh = k * gend

    lhs = jnp.concatenate([at, rt], axis=0).astype(BF16)
    rhs = jnp.concatenate([_bd(bt, bd16), _bd(kt, bd16)], axis=0)
    a_all = lax.dot_general(lhs, rhs, (((1,), (1,)), ((), ())), preferred_element_type=F32)
    yield
    n_ab = a_all[:CHUNK, :HGW] * strict_m
    a_ak = a_all[:CHUNK, HGW:] * strict_m
    a_rb = a_all[CHUNK:, :HGW] * incl_m
    a_rk = a_all[CHUNK:, HGW:] * incl_m

    t = eye_p + n_ab
    nk = _dot(n_ab.astype(BF16), _bd(n_ab, bd16))
    yield
    for _ in range(4):
        both = _dot(jnp.concatenate([nk, t], axis=0).astype(BF16), _bd(nk, bd16))
        yield
        nk = both[:CHUNK]
        t = t + both[CHUNK:]
    tn = _dot(t.astype(BF16), _bd(nk, bd16))
    yield
    t = t + tn

    tb = t.astype(BF16)
    vbd = _bd(v, bd16)
    w = _dot(tb, _bd(at, bd16))
    akv = _dot(a_ak.astype(BF16), vbd)
    yield
    uv = _dot(tb, _bd(akv, bd16))
    arb = a_rb.astype(BF16)
    qpd = _dot(arb, _bd(w, bd16))
    yield
    y_in = _dot(jnp.concatenate([arb, a_rk.astype(BF16)], axis=1),
                jnp.concatenate([_bd(uv, bd16), vbd], axis=0))
    lhs_t = jnp.concatenate([bh, kh], axis=0).astype(BF16)
    rhs_t = jnp.concatenate([jnp.concatenate([w, uv], axis=1),
                             jnp.concatenate([jnp.zeros_like(v), v], axis=1)], axis=0).astype(BF16)
    mc = lax.dot_general(lhs_t, rhs_t, (((0,), (0,)), ((), ())), preferred_element_type=F32)
    yield
    m_bd = mc[:, :HGW] * bd32 + eye32 * jnp.exp(tot)
    c_bd = mc[:, HGW:] * bd32
    return y_in, (rt + qpd).astype(BF16), m_bd.astype(BF16), c_bd


def _interleave(gens):
    results = [None] * len(gens)
    active = list(range(len(gens)))
    while active:
        still = []
        for i in active:
            try:
                next(gens[i])
                still.append(i)
            except StopIteration as stop:
                results[i] = stop.value
        active = still
    return results


def _split_dot_left(b_exact, a):
    hi = a.astype(BF16)
    lo = (a - hi.astype(F32)).astype(BF16)
    return _dot(b_exact, hi) + _dot(b_exact, lo)


def _rwscan_kernel(rf, kf, vf, nf, af, lf, rb, kb, vb, nb, ab, lb, tri_ref, m64_ref, bd16_ref, bd32_ref,
                   yf_out, yb_out, s_ref):
    c = pl.program_id(1)

    @pl.when(c == 0)
    def _():
        s_ref[...] = jnp.zeros(s_ref.shape, F32)

    nch = rf.shape[0] // CHUNK
    bd16 = bd16_ref[...]
    bd32 = bd32_ref[0]
    eye32 = bd32_ref[1]
    eye_p = m64_ref[4]
    ngrp = RW_HEADS // HG
    chains = []
    gens = []
    for d, refs, y_out in ((0, (rf, kf, vf, nf, af, lf), yf_out), (1, (rb, kb, vb, nb, ab, lb), yb_out)):
        reverse = d == 1
        consts = (tri_ref[d], m64_ref[2 * d], m64_ref[2 * d + 1], eye_p, bd16, bd32, eye32)
        order = list(range(nch - 1, -1, -1) if reverse else range(nch))
        for g in range(ngrp):
            chains.append((d, g, y_out, order))
            for j in order:
                ops = [x[j * CHUNK:(j + 1) * CHUNK, g * HGW:(g + 1) * HGW] for x in refs]
                gens.append(_chunk_intra(*ops, reverse, consts))
    parts = _interleave(gens)

    states = [s_ref[d, g] for d, g, _, _ in chains]
    for step in range(nch):
        for ci, (d, g, y_out, order) in enumerate(chains):
            j = order[step]
            y_in, qp, m_bd, c_bd = parts[ci * nch + step]
            sb = states[ci].astype(BF16)
            y_out[j * CHUNK:(j + 1) * CHUNK, g * HGW:(g + 1) * HGW] = y_in + _dot(qp, sb)
            states[ci] = _dot(m_bd, sb) + c_bd
    for ci, (d, g, _, _) in enumerate(chains):
        s_ref[d, g] = states[ci]


def _rwscan(r, k, v, kkn, kka, lwf, lwb, bsz, seq, p):
    n = r.shape[0]
    rows = min(NCH * CHUNK, seq)
    nb = seq // rows
    fwd = pl.BlockSpec((rows, RW_WIDTH), lambda b, c: (b * nb + c, 0))
    bwd = pl.BlockSpec((rows, RW_WIDTH), lambda b, c: (b * nb + nb - 1 - c, 0))
    full = lambda a: pl.BlockSpec(a.shape, lambda b, c: (0,) * a.ndim)
    sds = jax.ShapeDtypeStruct((n, RW_WIDTH), F32)
    return pl.pallas_call(
        _rwscan_kernel,
        grid=(bsz, nb),
        in_specs=[fwd] * 6 + [bwd] * 6 + [full(p['sc_tri']), full(p['sc_m64']), full(p['sc_bd16']),
                                           full(p['sc_bd32'])],
        out_specs=[fwd, bwd],
        out_shape=[sds, sds],
        scratch_shapes=[pltpu.VMEM((2, RW_HEADS // HG, HGW, HGW), F32)],
        compiler_params=_cparams(("parallel", "arbitrary")),
        name="rwscan",
    )(r, k, v, kkn, kka, lwf, r, k, v, kkn, kka, lwb, p['sc_tri'], p['sc_m64'], p['sc_bd16'], p['sc_bd32'])


def _post_kernel(x_ref, oa_ref, yf_ref, yb_ref, bonus_ref, g_ref, lng_ref, lnb_ref, seg_ref, wout_ref,
                 g2_ref, wr_ref, br_ref, x1_out, hp_out, gid_out):
    seg = seg_ref[...]
    y = yf_ref[...] + yb_ref[...]
    mean = _split_dot(y, seg) * (1.0 / RW_HEAD)
    dlt = y - mean
    var = _split_dot(dlt * dlt, seg) * (1.0 / RW_HEAD)
    yn = dlt * lax.rsqrt(var + LN_X_EPS) * lng_ref[...] + lnb_ref[...]
    ob = (yn + bonus_ref[...]) * g_ref[...]
    x1 = (x_ref[...] + _dot(oa_ref[...], wout_ref[:MLA_WIDTH, :])
          + _dot(ob.astype(BF16), wout_ref[MLA_WIDTH:, :]))
    x1_out[...] = x1
    h2 = x1 * lax.rsqrt(jnp.mean(x1 * x1, axis=-1, keepdims=True) + RMS_EPS) * g2_ref[...]
    h2b = h2.astype(BF16)
    bits = lax.bitcast_convert_type(h2b.astype(F32), jnp.uint32)
    hp_out[...] = (bits[:, :D_MODEL // 2] >> 16) | (bits[:, D_MODEL // 2:] & jnp.uint32(0xFFFF0000))
    logits = _dot(h2b, wr_ref[...]) + br_ref[...]
    _, gidx, _ = _group_choice(logits)
    gid_out[...] = jnp.broadcast_to(gidx, gid_out.shape)


def _group_choice(logits):
    lane_i = lax.broadcasted_iota(jnp.int32, logits.shape, 1)
    lane = lane_i.astype(F32)
    is_g = (lane_i >= N_EXPERTS) & (lane_i < N_EXPERTS + N_GROUPS)
    gl = jnp.where(is_g, logits, -jnp.inf)
    gmax = jnp.max(gl, axis=-1, keepdims=True)
    gidx = jnp.min(jnp.where(gl == gmax, lane, 1e9), axis=-1, keepdims=True) - N_EXPERTS
    p_g = 1.0 / jnp.sum(jnp.where(is_g, jnp.exp(gl - gmax), 0.0), axis=-1, keepdims=True)
    return p_g, gidx, lane


def _post(x2, oa, yf, yb, bonus, g, p, tm):
    n = x2.shape[0]
    tm = min(tm, n)
    row = lambda i: (i, 0)
    fixed = lambda i: (0, 0)
    full = lambda a: pl.BlockSpec(a.shape, fixed)
    rw = pl.BlockSpec((tm, RW_WIDTH), row)
    return pl.pallas_call(
        _post_kernel,
        grid=(n // tm,),
        in_specs=[pl.BlockSpec((tm, D_MODEL), row), pl.BlockSpec((tm, MLA_WIDTH), row), rw, rw, rw, rw,
                  full(p['ln_g']), full(p['ln_b']), full(p['seg']), full(p['w_out']), full(p['g2']),
                  full(p['w_r']), full(p['b_r'])],
        out_specs=[pl.BlockSpec((tm, D_MODEL), row), pl.BlockSpec((tm, D_MODEL // 2), row),
                   pl.BlockSpec((tm, LANE), row)],
        out_shape=[jax.ShapeDtypeStruct((n, D_MODEL), F32), jax.ShapeDtypeStruct((n, D_MODEL // 2), jnp.uint32),
                   jax.ShapeDtypeStruct((n, LANE), F32)],
        compiler_params=_cparams(("parallel",)),
        name="post",
    )(x2, oa, yf, yb, bonus, g, p['ln_g'], p['ln_b'], p['seg'], p['w_out'], p['g2'], p['w_r'], p['b_r'])


def _route_plan(gid, tm, npt):
    g = gid[:, 0].astype(jnp.int32)
    onehot = (g[:, None] == jnp.arange(N_GROUPS, dtype=jnp.int32)[None, :]).astype(jnp.int32)
    csum = jnp.cumsum(onehot, axis=0)
    rank = jnp.sum(csum * onehot, axis=1) - 1
    ntile = (csum[-1] + tm - 1) // tm
    tile_end = jnp.cumsum(ntile)
    off = (tile_end - ntile) * tm
    pos = jnp.sum(onehot * off[None, :], axis=1) + rank
    n_active = tile_end[-1]
    t = jnp.minimum(jnp.arange(npt, dtype=jnp.int32), n_active - 1)
    tile_gid = jnp.sum((t[:, None] >= tile_end[None, :]).astype(jnp.int32), axis=1)
    meta = jnp.concatenate([n_active[None], tile_gid]).astype(jnp.int32)
    return pos.astype(jnp.int32), meta


def _row_dma_loops(n_rows, make_copy):
    def issue(h, carry):
        make_copy(2 * h).start(priority=0)
        make_copy(2 * h + 1).start(priority=1)
        return carry
    lax.fori_loop(0, n_rows // 2, issue, 0, unroll=4)

    def drain(r, carry):
        make_copy(0).wait()
        return carry
    lax.fori_loop(0, n_rows, drain, 0, unroll=8)


def _scatter_kernel(pos_ref, src_ref, init_ref, dst_ref, sem):
    _row_dma_loops(src_ref.shape[0], lambda r: pltpu.make_async_copy(
        src_ref.at[pl.ds(r, 1)], dst_ref.at[pl.ds(pos_ref[0, 0, r], 1)], sem))


def _scatter_rows(pos, src, n_dst, tms):
    n, w = src.shape
    tms = min(tms, n)
    return pl.pallas_call(
        _scatter_kernel,
        grid=(n // tms,),
        in_specs=[pl.BlockSpec((1, 1, tms), lambda i: (i, 0, 0), memory_space=pltpu.SMEM),
                  pl.BlockSpec((tms, w), lambda i: (i, 0)), pl.BlockSpec(memory_space=pl.ANY)],
        out_specs=pl.BlockSpec(memory_space=pl.ANY),
        out_shape=jax.ShapeDtypeStruct((n_dst, w), src.dtype),
        scratch_shapes=[pltpu.SemaphoreType.DMA(())],
        input_output_aliases={2: 0},
        compiler_params=_cparams(("arbitrary",)),
        name="moe_scatter",
    )(pos.reshape(n // tms, 1, tms), src, jnp.zeros((n_dst, w), src.dtype))


def _moe_group_kernel(meta_ref, xs_ref, wr_ref, br_ref, wg_ref, wu_ref, wd_ref, ys_ref):
    i = pl.program_id(0)

    @pl.when(i >= meta_ref[0])
    def _():
        ys_ref[...] = jnp.zeros(ys_ref.shape, F32)

    @pl.when(i < meta_ref[0])
    def _():
        tm = xs_ref.shape[0]
        word = xs_ref[...]
        h2 = jnp.concatenate([lax.bitcast_convert_type(word << 16, F32),
                              lax.bitcast_convert_type(word & jnp.uint32(0xFFFF0000), F32)],
                             axis=1).astype(BF16)
        def up(c):
            ks = range(c * MOE_EPC, (c + 1) * MOE_EPC)
            return (jnp.concatenate([_dot(h2, wg_ref[k]) for k in ks], axis=1),
                    jnp.concatenate([_dot(h2, wu_ref[k]) for k in ks], axis=1))

        pend = up(0)
        logits = _dot(h2, wr_ref[...]) + br_ref[...]
        p_g, _, lane = _group_choice(logits)
        first = (meta_ref[1 + i] * EXPERTS_PER_GROUP).astype(F32)
        in_grp = (lane >= first) & (lane < first + EXPERTS_PER_GROUP)
        el = jnp.where(in_grp, logits, -jnp.inf)
        v1 = jnp.max(el, axis=-1, keepdims=True)
        i1 = jnp.min(jnp.where(el == v1, lane, 1e9), axis=-1, keepdims=True)
        el2 = jnp.where(lane == i1, -jnp.inf, el)
        v2 = jnp.max(el2, axis=-1, keepdims=True)
        i2 = jnp.min(jnp.where(el2 == v2, lane, 1e9), axis=-1, keepdims=True)
        e2 = jnp.exp(v2 - v1)
        den = 1.0 + e2
        w1 = (1.0 / den) * p_g
        w2 = (e2 / den) * p_g

        acc = None
        nchunk = EXPERTS_PER_GROUP // MOE_EPC
        for c in range(nchunk):
            nxt = up(c + 1) if c + 1 < nchunk else None
            hg, hu = pend
            gate = jnp.concatenate(
                [jnp.broadcast_to(jnp.where(i1 == first + k, w1, 0.0) + jnp.where(i2 == first + k, w2, 0.0),
                                  (tm, D_EXPERT)) for k in range(c * MOE_EPC, (c + 1) * MOE_EPC)], axis=1)
            act = hg * jax.nn.sigmoid(hg) * hu * gate
            wd = wd_ref[c * MOE_EPC:(c + 1) * MOE_EPC].reshape(MOE_EPC * D_EXPERT, D_MODEL)
            part = _dot(act.astype(BF16), wd)
            acc = part if acc is None else acc + part
            pend = nxt
        ys_ref[...] = acc


def _moe_grouped(xs, meta, p, tm, npt):
    def tile(i, m):
        return (jnp.minimum(i, m[0] - 1), 0)

    def group(i, m):
        return (m[1 + i], 0, 0)

    grid_spec = pltpu.PrefetchScalarGridSpec(
        num_scalar_prefetch=1,
        grid=(npt,),
        in_specs=[pl.BlockSpec((tm, D_MODEL // 2), tile),
                  pl.BlockSpec(p['w_r'].shape, lambda i, m: (0, 0)),
                  pl.BlockSpec(p['b_r'].shape, lambda i, m: (0, 0)),
                  pl.BlockSpec((EXPERTS_PER_GROUP, D_MODEL, D_EXPERT), group),
                  pl.BlockSpec((EXPERTS_PER_GROUP, D_MODEL, D_EXPERT), group),
                  pl.BlockSpec((EXPERTS_PER_GROUP, D_EXPERT, D_MODEL), group)],
        out_specs=pl.BlockSpec((tm, D_MODEL), lambda i, m: (i, 0)),
    )
    return pl.pallas_call(
        _moe_group_kernel,
        grid_spec=grid_spec,
        out_shape=jax.ShapeDtypeStruct((npt * tm, D_MODEL), F32),
        compiler_params=_cparams(("arbitrary",)),
        name="moe_group",
    )(meta, xs, p['w_r'], p['b_r'], p['wg'], p['wu'], p['wd'])


def _gather_kernel(pos_ref, x1_ref, ys_ref, o_ref, buf, sem):
    _row_dma_loops(buf.shape[0], lambda r: pltpu.make_async_copy(
        ys_ref.at[pl.ds(pos_ref[0, 0, r], 1)], buf.at[pl.ds(r, 1)], sem))
    o_ref[...] = x1_ref[...] + buf[...]


def _gather_add(pos, x1, ys, tms):
    n, d = x1.shape
    tms = min(tms, n)
    return pl.pallas_call(
        _gather_kernel,
        grid=(n // tms,),
        in_specs=[pl.BlockSpec((1, 1, tms), lambda i: (i, 0, 0), memory_space=pltpu.SMEM),
                  pl.BlockSpec((tms, d), lambda i: (i, 0)),
                  pl.BlockSpec(memory_space=pl.ANY)],
        out_specs=pl.BlockSpec((tms, d), lambda i: (i, 0)),
        out_shape=jax.ShapeDtypeStruct((n, d), F32),
        scratch_shapes=[pltpu.VMEM((tms, d), F32), pltpu.SemaphoreType.DMA(())],
        compiler_params=_cparams(("arbitrary",)),
        name="moe_gather",
    )(pos.reshape(n // tms, 1, tms), x1, ys)


def _head_pad_cols(w, head_w, lo, hi, dst, width=HEAD_PAD, perm=None):
    rows = w.shape[0]
    wh = w.reshape(rows, -1, head_w)[:, :, lo:hi]
    if perm is not None:
        wh = wh[:, :, perm]
    out = jnp.zeros((rows, wh.shape[1], width), w.dtype)
    return out.at[:, :, dst:dst + (hi - lo)].set(wh).reshape(rows, -1)


def _pack_params(lp, max_seq):
    (norm1_g, w_in, q_a_norm_g, w_uq, kv_a_norm_g, w_ukv, q_norm_g, k_norm_g, mu_shift, w0_f, w_up_f,
     w0_b, w_up_b, a0, a_up, g_up, k_k, k_a, r_k, ln_x_g, ln_x_b, w_out, norm2_g, w_router_group,
     b_router_group, w_router_expert, b_router_expert, w_expert_gate, w_expert_up, w_expert_down) = lp
    half = QK_ROPE // 2
    swap = jnp.concatenate([jnp.arange(half, QK_ROPE), jnp.arange(half)])
    zcol = lambda rows, n: jnp.zeros((rows, n), F32)
    p = {}
    c0 = Q_LORA + KV_LORA
    w_kr = w_in[:, c0:c0 + QK_ROPE]
    u0 = c0 + QK_ROPE
    wu_ = w_in[:, u0:]
    d = D_MODEL
    w_in_p = jnp.concatenate([
        w_in[:, :c0],
        zcol(d, QK_NOPE), w_kr, zcol(d, LANE - QK_HEAD),
        zcol(d, QK_NOPE), w_kr[:, swap], zcol(d, LANE - QK_HEAD),
        wu_[:, :3 * RW_WIDTH + 2 * DECAY_LORA + A_LORA], zcol(d, LANE - A_LORA),
        wu_[:, 3 * RW_WIDTH + 2 * DECAY_LORA + A_LORA:]], axis=1)
    p['w_in'] = w_in_p.astype(BF16)
    mu = mu_shift[None, :]
    p['mu'] = jnp.concatenate([mu[:, :3 * RW_WIDTH + 2 * DECAY_LORA + A_LORA], zcol(1, LANE - A_LORA),
                               mu[:, 3 * RW_WIDTH + 2 * DECAY_LORA + A_LORA:]], axis=1)
    p['g1'] = norm1_g[None, :]
    p['qag'] = q_a_norm_g[None, :]
    p['kvag'] = kv_a_norm_g[None, :]
    p['w_uq'] = jnp.concatenate([
        _head_pad_cols(w_uq, QK_HEAD, 0, QK_HEAD, 0),
        _head_pad_cols(w_uq, QK_HEAD, QK_NOPE, QK_HEAD, QK_NOPE, perm=swap)], axis=1).astype(BF16)
    p['w_ukv'] = jnp.concatenate([
        _head_pad_cols(w_ukv, QK_NOPE + V_HEAD, 0, QK_NOPE, 0),
        _head_pad_cols(w_ukv, QK_NOPE + V_HEAD, QK_NOPE, QK_NOPE + V_HEAD, 0)], axis=1).astype(BF16)

    def gain_rows(g):
        main = jnp.concatenate([g, jnp.zeros((LANE - QK_HEAD,), F32)])
        swp = jnp.concatenate([jnp.zeros((QK_NOPE,), F32), g[QK_NOPE:][swap], jnp.zeros((LANE - QK_HEAD,), F32)])
        return jnp.stack([main, swp])
    kbound = math.sqrt(QK_HEAD) * jnp.max(jnp.abs(k_norm_g))
    qbound = jnp.max(jnp.abs(q_norm_g)) * LOG2E
    p['gq'] = jnp.concatenate([gain_rows(q_norm_g), jnp.full((1, LANE), qbound * kbound, F32)], axis=0)
    p['gk'] = gain_rows(k_norm_g)
    p['fast_softmax'] = 2.0 * qbound * kbound <= FAST_SOFTMAX_MAX_SHIFT
    inv_freq = 1.0 / (ROPE_THETA ** (jnp.arange(half, dtype=F32) / half))
    ang = jnp.arange(max_seq, dtype=F32)[:, None] * inv_freq[None, :]
    cos, sin = jnp.cos(ang), jnp.sin(ang)
    zpad = jnp.zeros((max_seq, LANE - QK_HEAD), F32)
    p['ct'] = jnp.concatenate([jnp.ones((max_seq, QK_NOPE), F32), cos, cos, zpad], axis=1)
    p['st'] = jnp.concatenate([jnp.zeros((max_seq, QK_NOPE), F32), -sin, sin, zpad], axis=1)

    zl = jnp.zeros((DECAY_LORA, RW_WIDTH), F32)
    p['w_dec'] = jnp.concatenate([jnp.concatenate([w_up_f, zl], axis=1),
                                  jnp.concatenate([zl, w_up_b], axis=1)], axis=0).astype(BF16)
    p['w0'] = jnp.concatenate([w0_f, w0_b])[None, :]
    p['a_up'] = jnp.concatenate([a_up, jnp.zeros((LANE - A_LORA, RW_WIDTH), F32)], axis=0).astype(BF16)
    p['a0'] = a0[None, :]
    p['g_up'] = g_up.astype(BF16)
    p['k_k'] = k_k[None, :]
    p['k_a'] = k_a[None, :]
    p['r_k'] = r_k.reshape(1, RW_WIDTH)
    hid = jnp.arange(RW_WIDTH) // RW_HEAD
    p['seg'] = (hid[:, None] == hid[None, :]).astype(BF16)
    ti = jnp.arange(CHUNK)[:, None]
    tj = jnp.arange(CHUNK)[None, :]
    p['sc_tri'] = jnp.stack([tj <= ti, tj >= ti]).astype(BF16)
    ps = (jnp.arange(HGW) % CHUNK)[None, :]
    p['sc_m64'] = jnp.stack([ps < ti, ps <= ti, ps > ti, ps >= ti, ps == ti]).astype(F32)
    blk = jnp.arange(HGW) // RW_HEAD
    same = blk[:, None] == blk[None, :]
    p['sc_bd16'] = same.astype(BF16)
    p['sc_bd32'] = jnp.stack([same, jnp.eye(HGW, dtype=bool)]).astype(F32)
    p['ln_g'] = ln_x_g[None, :]
    p['ln_b'] = ln_x_b[None, :]
    p['w_out'] = w_out.astype(BF16)
    p['g2'] = norm2_g[None, :]
    p['w_r'] = jnp.concatenate([w_router_expert, w_router_group,
                                zcol(d, LANE - N_EXPERTS - N_GROUPS)], axis=1).astype(BF16)
    p['b_r'] = jnp.concatenate([b_router_expert, b_router_group,
                                jnp.zeros((LANE - N_EXPERTS - N_GROUPS,), F32)])[None, :]
    p['wg'] = w_expert_gate.astype(BF16)
    p['wu'] = w_expert_up.astype(BF16)
    p['wd'] = w_expert_down.astype(BF16)
    return p


TM_PROJ = 512
TQ = 1024
TK = 4096
TKS = 2048
TQ_MAX = 1024
TK_MAX = 2048
TKS_MAX = 1024
TM_POST = 512
TM_MOE = 512
TM_ROWS = 512


def _layer(x, p):
    bsz, seq, d = x.shape
    n = bsz * seq
    x2 = x.reshape(n, d)
    q, k, v, r, k2, vv, kkn, kka, lwf, lwb, g, bonus = _proj(x2, seq, p, TM_PROJ)
    oa = _attn(q, k, v, bsz, seq, p['fast_softmax'])
    yf, yb = _rwscan(r, k2, vv, kkn, kka, lwf, lwb, bsz, seq, p)
    x1, hp, gid = _post(x2, oa, yf, yb, bonus, g, p, TM_POST)
    tm = min(TM_MOE, n)
    npt = n // tm + N_GROUPS
    pos, meta = _route_plan(gid, tm, npt)
    xs = _scatter_rows(pos, hp, npt * tm, TM_ROWS)
    ys = _moe_grouped(xs, meta, p, tm, npt)
    out = _gather_add(pos, x1, ys, TM_ROWS)
    return out.reshape(bsz, seq, d)


def kernel(x_prompt, x_sample, norm1_g, w_in, q_a_norm_g, w_uq, kv_a_norm_g, w_ukv, q_norm_g, k_norm_g, mu_shift, w0_f, w_up_f, w0_b, w_up_b, a0, a_up, g_up, k_k, k_a, r_k, ln_x_g, ln_x_b, w_out, norm2_g, w_router_group, b_router_group, w_router_expert, b_router_expert, w_expert_gate, w_expert_up, w_expert_down):
    layer_params = (norm1_g, w_in, q_a_norm_g, w_uq, kv_a_norm_g, w_ukv, q_norm_g, k_norm_g, mu_shift,
                    w0_f, w_up_f, w0_b, w_up_b, a0, a_up, g_up, k_k, k_a, r_k, ln_x_g, ln_x_b, w_out,
                    norm2_g, w_router_group, b_router_group, w_router_expert, b_router_expert,
                    w_expert_gate, w_expert_up, w_expert_down)
    y_prompt, y_sample = x_prompt, x_sample
    max_seq = max(x_prompt.shape[1], x_sample.shape[1])
    for layer in range(norm1_g.shape[0]):
        p = _pack_params([w[layer] for w in layer_params], max_seq)
        y_prompt = _layer(y_prompt, p)
        y_sample = _layer(y_sample, p)
    return (y_prompt, y_sample)
```

```python
import functools
import math

import jax
import jax.numpy as jnp
from jax import lax
from jax.experimental import pallas as pl
from jax.experimental.pallas import tpu as pltpu

D_MODEL = 1024
MLA_HEADS = 8
QK_NOPE = 64
QK_ROPE = 32
QK_HEAD = QK_NOPE + QK_ROPE
V_HEAD = 64
Q_LORA = 384
KV_LORA = 256
ROPE_THETA = 10000.0
RW_HEADS = 8
RW_HEAD = 64
RW_WIDTH = RW_HEADS * RW_HEAD
DECAY_LORA = 64
A_LORA = 64
GATE_LORA = 128
LN_X_EPS = 6.4e-4
MLA_WIDTH = MLA_HEADS * V_HEAD
N_GROUPS = 4
EXPERTS_PER_GROUP = 8
N_EXPERTS = N_GROUPS * EXPERTS_PER_GROUP
D_EXPERT = 256
MOE_EPC = 2
RMS_EPS = 1e-6

LANE = 128
HEAD_PAD = LANE
MLA_PAD = MLA_HEADS * HEAD_PAD
RW_COLS_PAD = 3 * RW_WIDTH + 3 * LANE
MLA_COLS_PAD = Q_LORA + KV_LORA + 2 * LANE
D_IN_PAD = MLA_COLS_PAD + RW_COLS_PAD
CHUNK = 64
HG = 4
HGW = HG * RW_HEAD
VMEM_LIMIT = 56 * 1024 * 1024
LOG2E = 1.4426950408889634
NCH = 4
FAST_SOFTMAX_MAX_SHIFT = 100.0

F32 = jnp.float32
BF16 = jnp.bfloat16


def _dot(a, b):
    return jnp.dot(a, b, preferred_element_type=F32)


def _split_dot(a, b_exact):
    hi = a.astype(BF16)
    lo = (a - hi.astype(F32)).astype(BF16)
    return _dot(hi, b_exact) + _dot(lo, b_exact)


def _cparams(sem):
    return pltpu.CompilerParams(dimension_semantics=sem, vmem_limit_bytes=VMEM_LIMIT)


def _proj_kernel(x_ref, xp_ref, xn_ref, g1_ref, win_ref, qag_ref, wuq_ref, kvag_ref, wukv_ref, gq_ref, gk_ref,
                 ct_ref, st_ref, *rw_refs, tm, seq):
    q_out, k_out, v_out = rw_refs[10:13]
    i = pl.program_id(0)
    x = jnp.concatenate([xp_ref[0], x_ref[...], xn_ref[0]], axis=0)
    h = x * lax.rsqrt(jnp.mean(x * x, axis=-1, keepdims=True) + RMS_EPS) * g1_ref[...]
    ze = _dot(h.astype(BF16), win_ref[...])
    z = ze[8:tm + 8]
    first = (i * tm) % seq == 0
    last = ((i + 1) * tm) % seq == 0
    prev_row = jnp.where(first, 0.0, ze[7:8, MLA_COLS_PAD:])
    next_row = jnp.where(last, 0.0, ze[tm + 8:tm + 9, MLA_COLS_PAD:])
    _rw_token_maps(z[:, MLA_COLS_PAD:], prev_row, next_row, *rw_refs[:10], *rw_refs[13:])

    cq = z[:, :Q_LORA]
    cqn = cq * lax.rsqrt(jnp.mean(cq * cq, axis=-1, keepdims=True) + RMS_EPS) * qag_ref[...]
    q2 = _dot(cqn.astype(BF16), wuq_ref[...])
    ckv = z[:, Q_LORA:Q_LORA + KV_LORA]
    ckvn = ckv * lax.rsqrt(jnp.mean(ckv * ckv, axis=-1, keepdims=True) + RMS_EPS) * kvag_ref[...]
    kv2 = _dot(ckvn.astype(BF16), wukv_ref[...])
    kr_main = z[:, Q_LORA + KV_LORA:Q_LORA + KV_LORA + LANE]
    kr_swap = z[:, Q_LORA + KV_LORA + LANE:MLA_COLS_PAD]

    ct = ct_ref[...]
    st = st_ref[...]
    gq = gq_ref[...]
    gk = gk_ref[...]
    scale = QK_HEAD ** -0.5 * LOG2E
    lane = lax.broadcasted_iota(jnp.int32, (1, LANE), 1)
    ones_col = jnp.where(lane == V_HEAD, 1.0, 0.0).astype(F32)
    shift_col = jnp.where(lane == QK_HEAD, 1.0, 0.0).astype(F32)
    q_shift = shift_col * gq[2:3, 0:1]
    kr_ssq = jnp.sum(kr_main * kr_main, axis=-1, keepdims=True)
    q_main_f = gq[0:1] * ct
    q_swap_f = gq[1:2] * st
    k_rot = kr_main * (gk[0:1] * ct) + kr_swap * (gk[1:2] * st)
    for hd in range(MLA_HEADS):
        sl = slice(hd * HEAD_PAD, (hd + 1) * HEAD_PAD)
        qm = q2[:, sl]
        qs = q2[:, MLA_PAD + hd * HEAD_PAD:MLA_PAD + (hd + 1) * HEAD_PAD]
        rinv = lax.rsqrt(jnp.sum(qm * qm, axis=-1, keepdims=True) * (1.0 / QK_HEAD) + RMS_EPS)
        qh = (qm * q_main_f + qs * q_swap_f) * (rinv * scale)
        q_out[:, sl] = (qh - q_shift).astype(BF16)
        kn = kv2[:, sl]
        rinv_k = lax.rsqrt((jnp.sum(kn * kn, axis=-1, keepdims=True) + kr_ssq) * (1.0 / QK_HEAD) + RMS_EPS)
        kh = (kn * gk[0:1] + k_rot) * rinv_k
        k_out[:, sl] = (kh + shift_col).astype(BF16)
        v_out[:, sl] = (kv2[:, MLA_PAD + hd * HEAD_PAD:MLA_PAD + (hd + 1) * HEAD_PAD] + ones_col).astype(BF16)


def _proj(x2, seq, p, tm):
    n = x2.shape[0]
    tm = min(tm, seq)
    nseq_t = seq // tm
    row = lambda i: (i, 0)
    fixed = lambda i: (0, 0)
    pos = lambda i: (i % nseq_t, 0)
    full = lambda a: pl.BlockSpec(a.shape, fixed)
    g8 = tm // 8
    ngrp = n // 8
    x3 = x2.reshape(ngrp, 8, D_MODEL)
    rw_names = ('mu', 'w_dec', 'w0', 'a_up', 'a0', 'g_up', 'k_k', 'k_a', 'r_k', 'seg')
    mla_sds = jax.ShapeDtypeStruct((n, MLA_PAD), BF16)
    rw_sds = jax.ShapeDtypeStruct((n, RW_WIDTH), F32)
    return pl.pallas_call(
        functools.partial(_proj_kernel, tm=tm, seq=seq),
        grid=(n // tm,),
        in_specs=[pl.BlockSpec((tm, D_MODEL), row),
                  pl.BlockSpec((1, 8, D_MODEL), lambda i: (jnp.maximum(i * g8 - 1, 0), 0, 0)),
                  pl.BlockSpec((1, 8, D_MODEL), lambda i: (jnp.minimum((i + 1) * g8, ngrp - 1), 0, 0)),
                  full(p['g1']), full(p['w_in']), full(p['qag']),
                  full(p['w_uq']), full(p['kvag']), full(p['w_ukv']), full(p['gq']), full(p['gk']),
                  pl.BlockSpec((tm, LANE), pos), pl.BlockSpec((tm, LANE), pos)]
                 + [full(p[k]) for k in rw_names],
        out_specs=[pl.BlockSpec((tm, MLA_PAD), row)] * 3 + [pl.BlockSpec((tm, RW_WIDTH), row)] * 9,
        out_shape=[mla_sds] * 3 + [rw_sds] * 9,
        compiler_params=_cparams(("parallel",)),
        name="proj",
    )(x2, x3, x3, p['g1'], p['w_in'], p['qag'], p['w_uq'], p['kvag'], p['w_ukv'], p['gq'], p['gk'],
      p['ct'][:seq], p['st'][:seq], *[p[k] for k in rw_names])


def _attn_kernel(q_ref, k_ref, v_ref, o_ref, m_ref, acc_ref, *, running_max, tks):
    j = pl.program_id(3)

    @pl.when(j == 0)
    def _():
        if running_max:
            m_ref[...] = jnp.full(m_ref.shape, -jnp.inf, F32)
        acc_ref[...] = jnp.zeros(acc_ref.shape, F32)

    heads = [slice(hh * HEAD_PAD, (hh + 1) * HEAD_PAD) for hh in range(2)]
    nsub = k_ref.shape[0] // tks

    def score_pair(sb):
        rows = slice(sb * tks, (sb + 1) * tks)
        return [lax.dot_general(k_ref[rows, sl], q_ref[:, sl], (((1,), (1,)), ((), ())),
                                preferred_element_type=F32) for sl in heads]

    accs = [acc_ref[hh] for hh in range(2)]
    maxes = [m_ref[hh] for hh in range(2)] if running_max else None
    scores = score_pair(0)
    for sb in range(nsub):
        nxt = score_pair(sb + 1) if sb + 1 < nsub else None
        rows = slice(sb * tks, (sb + 1) * tks)
        for hh, sl in enumerate(heads):
            s = scores[hh]
            if running_max:
                m_new = jnp.maximum(maxes[hh], jnp.max(s, axis=0, keepdims=True))
                p = jnp.exp2(s - m_new)
            else:
                p = jnp.exp2(s)
            pv = lax.dot_general(v_ref[rows, sl], p.astype(BF16), (((0,), (0,)), ((), ())),
                                 preferred_element_type=F32)
            if running_max:
                accs[hh] = jnp.exp2(maxes[hh] - m_new) * accs[hh] + pv
                maxes[hh] = m_new
            else:
                accs[hh] = accs[hh] + pv
        scores = nxt
    for hh in range(2):
        acc_ref[hh] = accs[hh]
        if running_max:
            m_ref[hh] = maxes[hh]

    @pl.when(j == pl.num_programs(3) - 1)
    def _():
        outs = []
        for hh in range(2):
            a = acc_ref[hh]
            outs.append(a[:V_HEAD] / a[V_HEAD:V_HEAD + 1])
        o_ref[...] = jnp.concatenate(outs, axis=0).T.astype(BF16)


def _attn(q, k, v, bsz, seq, fast_softmax):
    return lax.cond(fast_softmax,
                    lambda q, k, v: _attn_call(q, k, v, bsz, seq, TQ, TK, TKS, False),
                    lambda q, k, v: _attn_call(q, k, v, bsz, seq, TQ_MAX, TK_MAX, TKS_MAX, True),
                    q, k, v)


def _attn_call(q, k, v, bsz, seq, tq, tk, tks, running_max):
    tq = min(tq, seq)
    tk = min(tk, seq)
    tks = min(tks, tk)
    nq, nk = seq // tq, seq // tk
    n = q.shape[0]
    return pl.pallas_call(
        functools.partial(_attn_kernel, running_max=running_max, tks=tks),
        grid=(bsz, MLA_HEADS // 2, nq, nk),
        in_specs=[pl.BlockSpec((tq, 2 * HEAD_PAD), lambda b, h, i, j: (b * nq + i, h)),
                  pl.BlockSpec((tk, 2 * HEAD_PAD), lambda b, h, i, j: (b * nk + j, h)),
                  pl.BlockSpec((tk, 2 * HEAD_PAD), lambda b, h, i, j: (b * nk + j, h))],
        out_specs=pl.BlockSpec((tq, 2 * V_HEAD), lambda b, h, i, j: (b * nq + i, h)),
        out_shape=jax.ShapeDtypeStruct((n, MLA_WIDTH), BF16),
        scratch_shapes=[pltpu.VMEM((2, 1, tq), F32), pltpu.VMEM((2, HEAD_PAD, tq), F32)],
        compiler_params=_cparams(("parallel", "parallel", "parallel", "arbitrary")),
        name="attn_max" if running_max else "attn",
    )(q, k, v)


def _rw_token_maps(u, prev_row, next_row, mu_ref, wdec_ref, w0_ref, aup_ref, a0_ref, gup_ref,
                   kk_ref, ka_ref, rk_ref, seg_ref,
                   r_out, k_out, v_out, kkn_out, kka_out, lwf_out, lwb_out, g_out, bonus_out):
    tm = u.shape[0]
    sub = lax.broadcasted_iota(jnp.int32, (8, 1), 0)
    prev = pltpu.roll(u, 1, 0)
    prev = jnp.concatenate([jnp.where(sub == 0, prev_row, prev[:8]), prev[8:]], axis=0)
    nxt = pltpu.roll(u, tm - 1, 0)
    nxt = jnp.concatenate([nxt[:tm - 8], jnp.where(sub == 7, next_row, nxt[tm - 8:])], axis=0)
    us = u + (0.5 * (prev + nxt) - u) * mu_ref[...]

    r = us[:, :RW_WIDTH]
    k = us[:, RW_WIDTH:2 * RW_WIDTH]
    v = us[:, 2 * RW_WIDTH:3 * RW_WIDTH]
    xw = us[:, 3 * RW_WIDTH:3 * RW_WIDTH + LANE]
    xa = us[:, 3 * RW_WIDTH + LANE:3 * RW_WIDTH + 2 * LANE]
    xg = us[:, 3 * RW_WIDTH + 2 * LANE:]

    wpre = w0_ref[...] + _dot(jnp.tanh(xw).astype(BF16), wdec_ref[...])
    lw = jax.nn.sigmoid(wpre) * (-math.exp(-0.5))
    lwf_out[...] = lw[:, :RW_WIDTH]
    lwb_out[...] = lw[:, RW_WIDTH:]
    a = jax.nn.sigmoid(a0_ref[...] + _dot(xa.astype(BF16), aup_ref[...]))
    g_out[...] = _dot(jax.nn.sigmoid(xg).astype(BF16), gup_ref[...])

    seg = seg_ref[...]
    kk = k * kk_ref[...]
    kkn = kk * lax.rsqrt(jnp.maximum(_split_dot(kk * kk, seg), 1e-24))
    k2 = k * (1.0 + (a - 1.0) * ka_ref[...])
    r_out[...] = r
    k_out[...] = k2
    v_out[...] = v
    kkn_out[...] = kkn
    kka_out[...] = kkn * a
    bonus_out[...] = _split_dot(r * k2 * rk_ref[...], seg) * v


def _bd(x, bd16):
    return jnp.concatenate([x.astype(BF16)] * HG, axis=0) * bd16


def _chunk_intra(r, k, v, kkn, kka, lw, reverse, consts):
    tri, strict_m, incl_m, eye_p, bd16, bd32, eye32 = consts
    lg = _split_dot_left(tri, lw)
    yield
    lgx = lg - lw
    tot = lg[0:1] if reverse else lg[CHUNK - 1:CHUNK]
    gi = jnp.exp(lg)
    ginv = jnp.exp(-lg)
    gend = jnp.exp(tot - lg)
    at = -kkn * jnp.exp(lgx)
    rt = r * gi
    bt = kka * ginv
    kt = k * ginv
    bh = kka * gend
    kh = k * gend

    lhs = jnp.concatenate([at, rt], axis=0).astype(BF16)
    rhs = jnp.concatenate([_bd(bt, bd16), _bd(kt, bd16)], axis=0)
    a_all = lax.dot_general(lhs, rhs, (((1,), (1,)), ((), ())), preferred_element_type=F32)
    yield
    n_ab = a_all[:CHUNK, :HGW] * strict_m
    a_ak = a_all[:CHUNK, HGW:] * strict_m
    a_rb = a_all[CHUNK:, :HGW] * incl_m
    a_rk = a_all[CHUNK:, HGW:] * incl_m

    t = eye_p + n_ab
    nk = _dot(n_ab.astype(BF16), _bd(n_ab, bd16))
    yield
    for _ in range(4):
        both = _dot(jnp.concatenate([nk, t], axis=0).astype(BF16), _bd(nk, bd16))
        yield
        nk = both[:CHUNK]
        t = t + both[CHUNK:]
    tn = _dot(t.astype(BF16), _bd(nk, bd16))
    yield
    t = t + tn

    tb = t.astype(BF16)
    vbd = _bd(v, bd16)
    w = _dot(tb, _bd(at, bd16))
    akv = _dot(a_ak.astype(BF16), vbd)
    yield
    uv = _dot(tb, _bd(akv, bd16))
    arb = a_rb.astype(BF16)
    qpd = _dot(arb, _bd(w, bd16))
    yield
    y_in = _dot(jnp.concatenate([arb, a_rk.astype(BF16)], axis=1),
                jnp.concatenate([_bd(uv, bd16), vbd], axis=0))
    lhs_t = jnp.concatenate([bh, kh], axis=0).astype(BF16)
    rhs_t = jnp.concatenate([jnp.concatenate([w, uv], axis=1),
                             jnp.concatenate([jnp.zeros_like(v), v], axis=1)], axis=0).astype(BF16)
    mc = lax.dot_general(lhs_t, rhs_t, (((0,), (0,)), ((), ())), preferred_element_type=F32)
    yield
    m_bd = mc[:, :HGW] * bd32 + eye32 * jnp.exp(tot)
    c_bd = mc[:, HGW:] * bd32
    return y_in, (rt + qpd).astype(BF16), m_bd.astype(BF16), c_bd


def _interleave(gens):
    results = [None] * len(gens)
    active = list(range(len(gens)))
    while active:
        still = []
        for i in active:
            try:
                next(gens[i])
                still.append(i)
            except StopIteration as stop:
                results[i] = stop.value
        active = still
    return results


def _split_dot_left(b_exact, a):
    hi = a.astype(BF16)
    lo = (a - hi.astype(F32)).astype(BF16)
    return _dot(b_exact, hi) + _dot(b_exact, lo)


def _rwscan_kernel(rf, kf, vf, nf, af, lf, rb, kb, vb, nb, ab, lb, tri_ref, m64_ref, bd16_ref, bd32_ref,
                   yf_out, yb_out, s_ref):
    c = pl.program_id(1)

    @pl.when(c == 0)
    def _():
        s_ref[...] = jnp.zeros(s_ref.shape, F32)

    nch = rf.shape[0] // CHUNK
    bd16 = bd16_ref[...]
    bd32 = bd32_ref[0]
    eye32 = bd32_ref[1]
    eye_p = m64_ref[4]
    ngrp = RW_HEADS // HG
    chains = []
    gens = []
    for d, refs, y_out in ((0, (rf, kf, vf, nf, af, lf), yf_out), (1, (rb, kb, vb, nb, ab, lb), yb_out)):
        reverse = d == 1
        consts = (tri_ref[d], m64_ref[2 * d], m64_ref[2 * d + 1], eye_p, bd16, bd32, eye32)
        order = list(range(nch - 1, -1, -1) if reverse else range(nch))
        for g in range(ngrp):
            chains.append((d, g, y_out, order))
            for j in order:
                ops = [x[j * CHUNK:(j + 1) * CHUNK, g * HGW:(g + 1) * HGW] for x in refs]
                gens.append(_chunk_intra(*ops, reverse, consts))
    parts = _interleave(gens)

    states = [s_ref[d, g] for d, g, _, _ in chains]
    for step in range(nch):
        for ci, (d, g, y_out, order) in enumerate(chains):
            j = order[step]
            y_in, qp, m_bd, c_bd = parts[ci * nch + step]
            sb = states[ci].astype(BF16)
            y_out[j * CHUNK:(j + 1) * CHUNK, g * HGW:(g + 1) * HGW] = y_in + _dot(qp, sb)
            states[ci] = _dot(m_bd, sb) + c_bd
    for ci, (d, g, _, _) in enumerate(chains):
        s_ref[d, g] = states[ci]


def _rwscan(r, k, v, kkn, kka, lwf, lwb, bsz, seq, p):
    n = r.shape[0]
    rows = min(NCH * CHUNK, seq)
    nb = seq // rows
    fwd = pl.BlockSpec((rows, RW_WIDTH), lambda b, c: (b * nb + c, 0))
    bwd = pl.BlockSpec((rows, RW_WIDTH), lambda b, c: (b * nb + nb - 1 - c, 0))
    full = lambda a: pl.BlockSpec(a.shape, lambda b, c: (0,) * a.ndim)
    sds = jax.ShapeDtypeStruct((n, RW_WIDTH), F32)
    return pl.pallas_call(
        _rwscan_kernel,
        grid=(bsz, nb),
        in_specs=[fwd] * 6 + [bwd] * 6 + [full(p['sc_tri']), full(p['sc_m64']), full(p['sc_bd16']),
                                           full(p['sc_bd32'])],
        out_specs=[fwd, bwd],
        out_shape=[sds, sds],
        scratch_shapes=[pltpu.VMEM((2, RW_HEADS // HG, HGW, HGW), F32)],
        compiler_params=_cparams(("parallel", "arbitrary")),
        name="rwscan",
    )(r, k, v, kkn, kka, lwf, r, k, v, kkn, kka, lwb, p['sc_tri'], p['sc_m64'], p['sc_bd16'], p['sc_bd32'])


def _post_kernel(x_ref, oa_ref, yf_ref, yb_ref, bonus_ref, g_ref, lng_ref, lnb_ref, seg_ref, wout_ref,
                 g2_ref, wr_ref, br_ref, x1_out, hp_out, gid_out):
    tm = x_ref.shape[0]
    nsplit = 2 if tm % 16 == 0 else 1
    rows_per = tm // nsplit

    def rows_gen(rs):
        seg = seg_ref[...]
        oa_part = _dot(oa_ref[rs, :], wout_ref[:MLA_WIDTH, :])
        y = yf_ref[rs, :] + yb_ref[rs, :]
        mean = _split_dot(y, seg) * (1.0 / RW_HEAD)
        yield
        dlt = y - mean
        var = _split_dot(dlt * dlt, seg) * (1.0 / RW_HEAD)
        yield
        yn = dlt * lax.rsqrt(var + LN_X_EPS) * lng_ref[...] + lnb_ref[...]
        ob = (yn + bonus_ref[rs, :]) * g_ref[rs, :]
        x1 = x_ref[rs, :] + oa_part + _dot(ob.astype(BF16), wout_ref[MLA_WIDTH:, :])
        yield
        x1_out[rs, :] = x1
        h2 = x1 * lax.rsqrt(jnp.mean(x1 * x1, axis=-1, keepdims=True) + RMS_EPS) * g2_ref[...]
        h2b = h2.astype(BF16)
        bits = lax.bitcast_convert_type(h2b.astype(F32), jnp.uint32)
        hp_out[rs, :] = (bits[:, :D_MODEL // 2] >> 16) | (bits[:, D_MODEL // 2:] & jnp.uint32(0xFFFF0000))
        logits = _dot(h2b, wr_ref[...]) + br_ref[...]
        yield
        _, gidx, _ = _group_choice(logits)
        gid_out[rs, :] = jnp.broadcast_to(gidx, (rows_per, LANE))

    _interleave([rows_gen(slice(j * rows_per, (j + 1) * rows_per)) for j in range(nsplit)])


def _group_choice(logits):
    lane_i = lax.broadcasted_iota(jnp.int32, logits.shape, 1)
    lane = lane_i.astype(F32)
    is_g = (lane_i >= N_EXPERTS) & (lane_i < N_EXPERTS + N_GROUPS)
    gl = jnp.where(is_g, logits, -jnp.inf)
    gmax = jnp.max(gl, axis=-1, keepdims=True)
    gidx = jnp.min(jnp.where(gl == gmax, lane, 1e9), axis=-1, keepdims=True) - N_EXPERTS
    p_g = 1.0 / jnp.sum(jnp.where(is_g, jnp.exp(gl - gmax), 0.0), axis=-1, keepdims=True)
    return p_g, gidx, lane


def _post(x2, oa, yf, yb, bonus, g, p, tm):
    n = x2.shape[0]
    tm = min(tm, n)
    row = lambda i: (i, 0)
    fixed = lambda i: (0, 0)
    full = lambda a: pl.BlockSpec(a.shape, fixed)
    rw = pl.BlockSpec((tm, RW_WIDTH), row)
    return pl.pallas_call(
        _post_kernel,
        grid=(n // tm,),
        in_specs=[pl.BlockSpec((tm, D_MODEL), row), pl.BlockSpec((tm, MLA_WIDTH), row), rw, rw, rw, rw,
                  full(p['ln_g']), full(p['ln_b']), full(p['seg']), full(p['w_out']), full(p['g2']),
                  full(p['w_r']), full(p['b_r'])],
        out_specs=[pl.BlockSpec((tm, D_MODEL), row), pl.BlockSpec((tm, D_MODEL // 2), row),
                   pl.BlockSpec((tm, LANE), row)],
        out_shape=[jax.ShapeDtypeStruct((n, D_MODEL), F32), jax.ShapeDtypeStruct((n, D_MODEL // 2), jnp.uint32),
                   jax.ShapeDtypeStruct((n, LANE), F32)],
        compiler_params=_cparams(("parallel",)),
        name="post",
    )(x2, oa, yf, yb, bonus, g, p['ln_g'], p['ln_b'], p['seg'], p['w_out'], p['g2'], p['w_r'], p['b_r'])


def _route_plan(gid, tm, npt):
    g = gid[:, 0].astype(jnp.int32)
    onehot = (g[:, None] == jnp.arange(N_GROUPS, dtype=jnp.int32)[None, :]).astype(jnp.int32)
    csum = jnp.cumsum(onehot, axis=0)
    rank = jnp.sum(csum * onehot, axis=1) - 1
    ntile = (csum[-1] + tm - 1) // tm
    tile_end = jnp.cumsum(ntile)
    off = (tile_end - ntile) * tm
    pos = jnp.sum(onehot * off[None, :], axis=1) + rank
    n_active = tile_end[-1]
    t = jnp.minimum(jnp.arange(npt, dtype=jnp.int32), n_active - 1)
    tile_gid = jnp.sum((t[:, None] >= tile_end[None, :]).astype(jnp.int32), axis=1)
    meta = jnp.concatenate([n_active[None], tile_gid]).astype(jnp.int32)
    return pos.astype(jnp.int32), meta


def _row_dma_loops(n_rows, make_copy):
    def issue(h, carry):
        make_copy(2 * h).start(priority=0)
        make_copy(2 * h + 1).start(priority=1)
        return carry
    lax.fori_loop(0, n_rows // 2, issue, 0, unroll=4)

    def drain(r, carry):
        make_copy(0).wait()
        return carry
    lax.fori_loop(0, n_rows, drain, 0, unroll=8)


def _scatter_kernel(pos_ref, src_ref, init_ref, dst_ref, sem):
    _row_dma_loops(src_ref.shape[0], lambda r: pltpu.make_async_copy(
        src_ref.at[pl.ds(r, 1)], dst_ref.at[pl.ds(pos_ref[0, 0, r], 1)], sem))


def _scatter_rows(pos, src, n_dst, tms):
    n, w = src.shape
    tms = min(tms, n)
    return pl.pallas_call(
        _scatter_kernel,
        grid=(n // tms,),
        in_specs=[pl.BlockSpec((1, 1, tms), lambda i: (i, 0, 0), memory_space=pltpu.SMEM),
                  pl.BlockSpec((tms, w), lambda i: (i, 0)), pl.BlockSpec(memory_space=pl.ANY)],
        out_specs=pl.BlockSpec(memory_space=pl.ANY),
        out_shape=jax.ShapeDtypeStruct((n_dst, w), src.dtype),
        scratch_shapes=[pltpu.SemaphoreType.DMA(())],
        input_output_aliases={2: 0},
        compiler_params=_cparams(("arbitrary",)),
        name="moe_scatter",
    )(pos.reshape(n // tms, 1, tms), src, jnp.zeros((n_dst, w), src.dtype))


def _moe_group_kernel(meta_ref, xs_ref, wr_ref, br_ref, wg_ref, wu_ref, wd_ref, ys_ref):
    i = pl.program_id(0)

    @pl.when(i >= meta_ref[0])
    def _():
        ys_ref[...] = jnp.zeros(ys_ref.shape, F32)

    @pl.when(i < meta_ref[0])
    def _():
        tm = xs_ref.shape[0]
        word = xs_ref[...]
        h2 = jnp.concatenate([lax.bitcast_convert_type(word << 16, F32),
                              lax.bitcast_convert_type(word & jnp.uint32(0xFFFF0000), F32)],
                             axis=1).astype(BF16)
        def up(c):
            ks = range(c * MOE_EPC, (c + 1) * MOE_EPC)
            return (jnp.concatenate([_dot(h2, wg_ref[k]) for k in ks], axis=1),
                    jnp.concatenate([_dot(h2, wu_ref[k]) for k in ks], axis=1))

        pend = up(0)
        logits = _dot(h2, wr_ref[...]) + br_ref[...]
        p_g, _, lane = _group_choice(logits)
        first = (meta_ref[1 + i] * EXPERTS_PER_GROUP).astype(F32)
        in_grp = (lane >= first) & (lane < first + EXPERTS_PER_GROUP)
        el = jnp.where(in_grp, logits, -jnp.inf)
        v1 = jnp.max(el, axis=-1, keepdims=True)
        i1 = jnp.min(jnp.where(el == v1, lane, 1e9), axis=-1, keepdims=True)
        el2 = jnp.where(lane == i1, -jnp.inf, el)
        v2 = jnp.max(el2, axis=-1, keepdims=True)
        i2 = jnp.min(jnp.where(el2 == v2, lane, 1e9), axis=-1, keepdims=True)
        e2 = jnp.exp(v2 - v1)
        den = 1.0 + e2
        w1 = (1.0 / den) * p_g
        w2 = (e2 / den) * p_g

        acc = None
        nchunk = EXPERTS_PER_GROUP // MOE_EPC
        for c in range(nchunk):
            nxt = up(c + 1) if c + 1 < nchunk else None
            hg, hu = pend
            gate = jnp.concatenate(
                [jnp.broadcast_to(jnp.where(i1 == first + k, w1, 0.0) + jnp.where(i2 == first + k, w2, 0.0),
                                  (tm, D_EXPERT)) for k in range(c * MOE_EPC, (c + 1) * MOE_EPC)], axis=1)
            act = hg * jax.nn.sigmoid(hg) * hu * gate
            wd = wd_ref[c * MOE_EPC:(c + 1) * MOE_EPC].reshape(MOE_EPC * D_EXPERT, D_MODEL)
            part = _dot(act.astype(BF16), wd)
            acc = part if acc is None else acc + part
            pend = nxt
        ys_ref[...] = acc


def _moe_grouped(xs, meta, p, tm, npt):
    def tile(i, m):
        return (jnp.minimum(i, m[0] - 1), 0)

    def group(i, m):
        return (m[1 + i], 0, 0)

    grid_spec = pltpu.PrefetchScalarGridSpec(
        num_scalar_prefetch=1,
        grid=(npt,),
        in_specs=[pl.BlockSpec((tm, D_MODEL // 2), tile),
                  pl.BlockSpec(p['w_r'].shape, lambda i, m: (0, 0)),
                  pl.BlockSpec(p['b_r'].shape, lambda i, m: (0, 0)),
                  pl.BlockSpec((EXPERTS_PER_GROUP, D_MODEL, D_EXPERT), group),
                  pl.BlockSpec((EXPERTS_PER_GROUP, D_MODEL, D_EXPERT), group),
                  pl.BlockSpec((EXPERTS_PER_GROUP, D_EXPERT, D_MODEL), group)],
        out_specs=pl.BlockSpec((tm, D_MODEL), lambda i, m: (i, 0)),
    )
    return pl.pallas_call(
        _moe_group_kernel,
        grid_spec=grid_spec,
        out_shape=jax.ShapeDtypeStruct((npt * tm, D_MODEL), F32),
        compiler_params=_cparams(("arbitrary",)),
        name="moe_group",
    )(meta, xs, p['w_r'], p['b_r'], p['wg'], p['wu'], p['wd'])


def _gather_kernel(pos_ref, x1_ref, ys_ref, o_ref, buf, sem):
    _row_dma_loops(buf.shape[0], lambda r: pltpu.make_async_copy(
        ys_ref.at[pl.ds(pos_ref[0, 0, r], 1)], buf.at[pl.ds(r, 1)], sem))
    o_ref[...] = x1_ref[...] + buf[...]


def _gather_add(pos, x1, ys, tms):
    n, d = x1.shape
    tms = min(tms, n)
    return pl.pallas_call(
        _gather_kernel,
        grid=(n // tms,),
        in_specs=[pl.BlockSpec((1, 1, tms), lambda i: (i, 0, 0), memory_space=pltpu.SMEM),
                  pl.BlockSpec((tms, d), lambda i: (i, 0)),
                  pl.BlockSpec(memory_space=pl.ANY)],
        out_specs=pl.BlockSpec((tms, d), lambda i: (i, 0)),
        out_shape=jax.ShapeDtypeStruct((n, d), F32),
        scratch_shapes=[pltpu.VMEM((tms, d), F32), pltpu.SemaphoreType.DMA(())],
        compiler_params=_cparams(("arbitrary",)),
        name="moe_gather",
    )(pos.reshape(n // tms, 1, tms), x1, ys)


def _head_pad_cols(w, head_w, lo, hi, dst, width=HEAD_PAD, perm=None):
    rows = w.shape[0]
    wh = w.reshape(rows, -1, head_w)[:, :, lo:hi]
    if perm is not None:
        wh = wh[:, :, perm]
    out = jnp.zeros((rows, wh.shape[1], width), w.dtype)
    return out.at[:, :, dst:dst + (hi - lo)].set(wh).reshape(rows, -1)


def _pack_params(lp, max_seq):
    (norm1_g, w_in, q_a_norm_g, w_uq, kv_a_norm_g, w_ukv, q_norm_g, k_norm_g, mu_shift, w0_f, w_up_f,
     w0_b, w_up_b, a0, a_up, g_up, k_k, k_a, r_k, ln_x_g, ln_x_b, w_out, norm2_g, w_router_group,
     b_router_group, w_router_expert, b_router_expert, w_expert_gate, w_expert_up, w_expert_down) = lp
    half = QK_ROPE // 2
    swap = jnp.concatenate([jnp.arange(half, QK_ROPE), jnp.arange(half)])
    zcol = lambda rows, n: jnp.zeros((rows, n), F32)
    p = {}
    c0 = Q_LORA + KV_LORA
    w_kr = w_in[:, c0:c0 + QK_ROPE]
    u0 = c0 + QK_ROPE
    wu_ = w_in[:, u0:]
    d = D_MODEL
    w_in_p = jnp.concatenate([
        w_in[:, :c0],
        zcol(d, QK_NOPE), w_kr, zcol(d, LANE - QK_HEAD),
        zcol(d, QK_NOPE), w_kr[:, swap], zcol(d, LANE - QK_HEAD),
        wu_[:, :3 * RW_WIDTH + 2 * DECAY_LORA + A_LORA], zcol(d, LANE - A_LORA),
        wu_[:, 3 * RW_WIDTH + 2 * DECAY_LORA + A_LORA:]], axis=1)
    p['w_in'] = w_in_p.astype(BF16)
    mu = mu_shift[None, :]
    p['mu'] = jnp.concatenate([mu[:, :3 * RW_WIDTH + 2 * DECAY_LORA + A_LORA], zcol(1, LANE - A_LORA),
                               mu[:, 3 * RW_WIDTH + 2 * DECAY_LORA + A_LORA:]], axis=1)
    p['g1'] = norm1_g[None, :]
    p['qag'] = q_a_norm_g[None, :]
    p['kvag'] = kv_a_norm_g[None, :]
    p['w_uq'] = jnp.concatenate([
        _head_pad_cols(w_uq, QK_HEAD, 0, QK_HEAD, 0),
        _head_pad_cols(w_uq, QK_HEAD, QK_NOPE, QK_HEAD, QK_NOPE, perm=swap)], axis=1).astype(BF16)
    p['w_ukv'] = jnp.concatenate([
        _head_pad_cols(w_ukv, QK_NOPE + V_HEAD, 0, QK_NOPE, 0),
        _head_pad_cols(w_ukv, QK_NOPE + V_HEAD, QK_NOPE, QK_NOPE + V_HEAD, 0)], axis=1).astype(BF16)

    def gain_rows(g):
        main = jnp.concatenate([g, jnp.zeros((LANE - QK_HEAD,), F32)])
        swp = jnp.concatenate([jnp.zeros((QK_NOPE,), F32), g[QK_NOPE:][swap], jnp.zeros((LANE - QK_HEAD,), F32)])
        return jnp.stack([main, swp])
    kbound = math.sqrt(QK_HEAD) * jnp.max(jnp.abs(k_norm_g))
    qbound = jnp.max(jnp.abs(q_norm_g)) * LOG2E
    p['gq'] = jnp.concatenate([gain_rows(q_norm_g), jnp.full((1, LANE), qbound * kbound, F32)], axis=0)
    p['gk'] = gain_rows(k_norm_g)
    p['fast_softmax'] = 2.0 * qbound * kbound <= FAST_SOFTMAX_MAX_SHIFT
    inv_freq = 1.0 / (ROPE_THETA ** (jnp.arange(half, dtype=F32) / half))
    ang = jnp.arange(max_seq, dtype=F32)[:, None] * inv_freq[None, :]
    cos, sin = jnp.cos(ang), jnp.sin(ang)
    zpad = jnp.zeros((max_seq, LANE - QK_HEAD), F32)
    p['ct'] = jnp.concatenate([jnp.ones((max_seq, QK_NOPE), F32), cos, cos, zpad], axis=1)
    p['st'] = jnp.concatenate([jnp.zeros((max_seq, QK_NOPE), F32), -sin, sin, zpad], axis=1)

    zl = jnp.zeros((DECAY_LORA, RW_WIDTH), F32)
    p['w_dec'] = jnp.concatenate([jnp.concatenate([w_up_f, zl], axis=1),
                                  jnp.concatenate([zl, w_up_b], axis=1)], axis=0).astype(BF16)
    p['w0'] = jnp.concatenate([w0_f, w0_b])[None, :]
    p['a_up'] = jnp.concatenate([a_up, jnp.zeros((LANE - A_LORA, RW_WIDTH), F32)], axis=0).astype(BF16)
    p['a0'] = a0[None, :]
    p['g_up'] = g_up.astype(BF16)
    p['k_k'] = k_k[None, :]
    p['k_a'] = k_a[None, :]
    p['r_k'] = r_k.reshape(1, RW_WIDTH)
    hid = jnp.arange(RW_WIDTH) // RW_HEAD
    p['seg'] = (hid[:, None] == hid[None, :]).astype(BF16)
    ti = jnp.arange(CHUNK)[:, None]
    tj = jnp.arange(CHUNK)[None, :]
    p['sc_tri'] = jnp.stack([tj <= ti, tj >= ti]).astype(BF16)
    ps = (jnp.arange(HGW) % CHUNK)[None, :]
    p['sc_m64'] = jnp.stack([ps < ti, ps <= ti, ps > ti, ps >= ti, ps == ti]).astype(F32)
    blk = jnp.arange(HGW) // RW_HEAD
    same = blk[:, None] == blk[None, :]
    p['sc_bd16'] = same.astype(BF16)
    p['sc_bd32'] = jnp.stack([same, jnp.eye(HGW, dtype=bool)]).astype(F32)
    p['ln_g'] = ln_x_g[None, :]
    p['ln_b'] = ln_x_b[None, :]
    p['w_out'] = w_out.astype(BF16)
    p['g2'] = norm2_g[None, :]
    p['w_r'] = jnp.concatenate([w_router_expert, w_router_group,
                                zcol(d, LANE - N_EXPERTS - N_GROUPS)], axis=1).astype(BF16)
    p['b_r'] = jnp.concatenate([b_router_expert, b_router_group,
                                jnp.zeros((LANE - N_EXPERTS - N_GROUPS,), F32)])[None, :]
    p['wg'] = w_expert_gate.astype(BF16)
    p['wu'] = w_expert_up.astype(BF16)
    p['wd'] = w_expert_down.astype(BF16)
    return p


TM_PROJ = 512
TQ = 1024
TK = 4096
TKS = 2048
TQ_MAX = 1024
TK_MAX = 2048
TKS_MAX = 1024
TM_POST = 512
TM_MOE = 512
TM_ROWS = 512


def _layer(x, p):
    bsz, seq, d = x.shape
    n = bsz * seq
    x2 = x.reshape(n, d)
    q, k, v, r, k2, vv, kkn, kka, lwf, lwb, g, bonus = _proj(x2, seq, p, TM_PROJ)
    oa = _attn(q, k, v, bsz, seq, p['fast_softmax'])
    yf, yb = _rwscan(r, k2, vv, kkn, kka, lwf, lwb, bsz, seq, p)
    x1, hp, gid = _post(x2, oa, yf, yb, bonus, g, p, TM_POST)
    tm = min(TM_MOE, n)
    npt = n // tm + N_GROUPS
    pos, meta = _route_plan(gid, tm, npt)
    xs = _scatter_rows(pos, hp, npt * tm, TM_ROWS)
    ys = _moe_grouped(xs, meta, p, tm, npt)
    out = _gather_add(pos, x1, ys, TM_ROWS)
    return out.reshape(bsz, seq, d)


def kernel(x_prompt, x_sample, norm1_g, w_in, q_a_norm_g, w_uq, kv_a_norm_g, w_ukv, q_norm_g, k_norm_g, mu_shift, w0_f, w_up_f, w0_b, w_up_b, a0, a_up, g_up, k_k, k_a, r_k, ln_x_g, ln_x_b, w_out, norm2_g, w_router_group, b_router_group, w_router_expert, b_router_expert, w_expert_gate, w_expert_up, w_expert_down):
    layer_params = (norm1_g, w_in, q_a_norm_g, w_uq, kv_a_norm_g, w_ukv, q_norm_g, k_norm_g, mu_shift,
                    w0_f, w_up_f, w0_b, w_up_b, a0, a_up, g_up, k_k, k_a, r_k, ln_x_g, ln_x_b, w_out,
                    norm2_g, w_router_group, b_router_group, w_router_expert, b_router_expert,
                    w_expert_gate, w_expert_up, w_expert_down)
    y_prompt, y_sample = x_prompt, x_sample
    max_seq = max(x_prompt.shape[1], x_sample.shape[1])
    for layer in range(norm1_g.shape[0]):
        p = _pack_params([w[layer] for w in layer_params], max_seq)
        y_prompt = _layer(y_prompt, p)
        y_sample = _layer(y_sample, p)
    return (y_prompt, y_sample)
```

```python
import functools
import math

import jax
import jax.numpy as jnp
from jax import lax
from jax.experimental import pallas as pl
from jax.experimental.pallas import tpu as pltpu

D_MODEL = 1024
MLA_HEADS = 8
QK_NOPE = 64
QK_ROPE = 32
QK_HEAD = QK_NOPE + QK_ROPE
V_HEAD = 64
Q_LORA = 384
KV_LORA = 256
ROPE_THETA = 10000.0
RW_HEADS = 8
RW_HEAD = 64
RW_WIDTH = RW_HEADS * RW_HEAD
DECAY_LORA = 64
A_LORA = 64
GATE_LORA = 128
LN_X_EPS = 6.4e-4
MLA_WIDTH = MLA_HEADS * V_HEAD
N_GROUPS = 4
EXPERTS_PER_GROUP = 8
N_EXPERTS = N_GROUPS * EXPERTS_PER_GROUP
D_EXPERT = 256
MOE_EPC = 2
RMS_EPS = 1e-6

LANE = 128
HEAD_PAD = LANE
MLA_PAD = MLA_HEADS * HEAD_PAD
RW_COLS_PAD = 3 * RW_WIDTH + 3 * LANE
MLA_COLS_PAD = Q_LORA + KV_LORA + 2 * LANE
D_IN_PAD = MLA_COLS_PAD + RW_COLS_PAD
CHUNK = 64
HG = 4
HGW = HG * RW_HEAD
VMEM_LIMIT = 56 * 1024 * 1024
LOG2E = 1.4426950408889634
NCH = 4
FAST_SOFTMAX_MAX_SHIFT = 100.0

F32 = jnp.float32
BF16 = jnp.bfloat16


def _dot(a, b):
    return jnp.dot(a, b, preferred_element_type=F32)


def _split_dot(a, b_exact):
    hi = a.astype(BF16)
    lo = (a - hi.astype(F32)).astype(BF16)
    return _dot(hi, b_exact) + _dot(lo, b_exact)


def _cparams(sem):
    return pltpu.CompilerParams(dimension_semantics=sem, vmem_limit_bytes=VMEM_LIMIT)


def _proj_kernel(x_ref, xp_ref, xn_ref, g1_ref, win_ref, qag_ref, wuq_ref, kvag_ref, wukv_ref, gq_ref, gk_ref,
                 ct_ref, st_ref, *rw_refs, tm, seq):
    q_out, k_out, v_out = rw_refs[10:13]
    i = pl.program_id(0)
    x = jnp.concatenate([xp_ref[0], x_ref[...], xn_ref[0]], axis=0)
    h = x * lax.rsqrt(jnp.mean(x * x, axis=-1, keepdims=True) + RMS_EPS) * g1_ref[...]
    ze = _dot(h.astype(BF16), win_ref[...])
    z = ze[8:tm + 8]
    first = (i * tm) % seq == 0
    last = ((i + 1) * tm) % seq == 0
    prev_row = jnp.where(first, 0.0, ze[7:8, MLA_COLS_PAD:])
    next_row = jnp.where(last, 0.0, ze[tm + 8:tm + 9, MLA_COLS_PAD:])
    _rw_token_maps(z[:, MLA_COLS_PAD:], prev_row, next_row, *rw_refs[:10], *rw_refs[13:])

    cq = z[:, :Q_LORA]
    cqn = cq * lax.rsqrt(jnp.mean(cq * cq, axis=-1, keepdims=True) + RMS_EPS) * qag_ref[...]
    q2 = _dot(cqn.astype(BF16), wuq_ref[...])
    ckv = z[:, Q_LORA:Q_LORA + KV_LORA]
    ckvn = ckv * lax.rsqrt(jnp.mean(ckv * ckv, axis=-1, keepdims=True) + RMS_EPS) * kvag_ref[...]
    kv2 = _dot(ckvn.astype(BF16), wukv_ref[...])
    kr_main = z[:, Q_LORA + KV_LORA:Q_LORA + KV_LORA + LANE]
    kr_swap = z[:, Q_LORA + KV_LORA + LANE:MLA_COLS_PAD]

    ct = ct_ref[...]
    st = st_ref[...]
    gq = gq_ref[...]
    gk = gk_ref[...]
    scale = QK_HEAD ** -0.5 * LOG2E
    lane = lax.broadcasted_iota(jnp.int32, (1, LANE), 1)
    ones_col = jnp.where(lane == V_HEAD, 1.0, 0.0).astype(F32)
    shift_col = jnp.where(lane == QK_HEAD, 1.0, 0.0).astype(F32)
    q_shift = shift_col * gq[2:3, 0:1]
    kr_ssq = jnp.sum(kr_main * kr_main, axis=-1, keepdims=True)
    q_main_f = gq[0:1] * ct
    q_swap_f = gq[1:2] * st
    k_rot = kr_main * (gk[0:1] * ct) + kr_swap * (gk[1:2] * st)
    for hd in range(MLA_HEADS):
        sl = slice(hd * HEAD_PAD, (hd + 1) * HEAD_PAD)
        qm = q2[:, sl]
        qs = q2[:, MLA_PAD + hd * HEAD_PAD:MLA_PAD + (hd + 1) * HEAD_PAD]
        rinv = lax.rsqrt(jnp.sum(qm * qm, axis=-1, keepdims=True) * (1.0 / QK_HEAD) + RMS_EPS)
        qh = (qm * q_main_f + qs * q_swap_f) * (rinv * scale)
        q_out[:, sl] = (qh - q_shift).astype(BF16)
        kn = kv2[:, sl]
        rinv_k = lax.rsqrt((jnp.sum(kn * kn, axis=-1, keepdims=True) + kr_ssq) * (1.0 / QK_HEAD) + RMS_EPS)
        kh = (kn * gk[0:1] + k_rot) * rinv_k
        k_out[:, sl] = (kh + shift_col).astype(BF16)
        v_out[:, sl] = (kv2[:, MLA_PAD + hd * HEAD_PAD:MLA_PAD + (hd + 1) * HEAD_PAD] + ones_col).astype(BF16)


def _proj(x2, seq, p, tm):
    n = x2.shape[0]
    tm = min(tm, seq)
    nseq_t = seq // tm
    row = lambda i: (i, 0)
    fixed = lambda i: (0, 0)
    pos = lambda i: (i % nseq_t, 0)
    full = lambda a: pl.BlockSpec(a.shape, fixed)
    g8 = tm // 8
    ngrp = n // 8
    x3 = x2.reshape(ngrp, 8, D_MODEL)
    rw_names = ('mu', 'w_dec', 'w0', 'a_up', 'a0', 'g_up', 'k_k', 'k_a', 'r_k', 'seg')
    mla_sds = jax.ShapeDtypeStruct((n, MLA_PAD), BF16)
    rw_sds = jax.ShapeDtypeStruct((n, RW_WIDTH), F32)
    return pl.pallas_call(
        functools.partial(_proj_kernel, tm=tm, seq=seq),
        grid=(n // tm,),
        in_specs=[pl.BlockSpec((tm, D_MODEL), row),
                  pl.BlockSpec((1, 8, D_MODEL), lambda i: (jnp.maximum(i * g8 - 1, 0), 0, 0)),
                  pl.BlockSpec((1, 8, D_MODEL), lambda i: (jnp.minimum((i + 1) * g8, ngrp - 1), 0, 0)),
                  full(p['g1']), full(p['w_in']), full(p['qag']),
                  full(p['w_uq']), full(p['kvag']), full(p['w_ukv']), full(p['gq']), full(p['gk']),
                  pl.BlockSpec((tm, LANE), pos), pl.BlockSpec((tm, LANE), pos)]
                 + [full(p[k]) for k in rw_names],
        out_specs=[pl.BlockSpec((tm, MLA_PAD), row)] * 3 + [pl.BlockSpec((tm, RW_WIDTH), row)] * 9,
        out_shape=[mla_sds] * 3 + [rw_sds] * 9,
        compiler_params=_cparams(("parallel",)),
        name="proj",
    )(x2, x3, x3, p['g1'], p['w_in'], p['qag'], p['w_uq'], p['kvag'], p['w_ukv'], p['gq'], p['gk'],
      p['ct'][:seq], p['st'][:seq], *[p[k] for k in rw_names])


def _attn_kernel(q_ref, k_ref, v_ref, o_ref, m_ref, acc_ref, *, running_max, tks):
    j = pl.program_id(3)

    @pl.when(j == 0)
    def _():
        if running_max:
            m_ref[...] = jnp.full(m_ref.shape, -jnp.inf, F32)
        acc_ref[...] = jnp.zeros(acc_ref.shape, F32)

    heads = [slice(hh * HEAD_PAD, (hh + 1) * HEAD_PAD) for hh in range(2)]
    nsub = k_ref.shape[0] // tks

    def score_pair(sb):
        rows = slice(sb * tks, (sb + 1) * tks)
        return [lax.dot_general(k_ref[rows, sl], q_ref[:, sl], (((1,), (1,)), ((), ())),
                                preferred_element_type=F32) for sl in heads]

    accs = [acc_ref[hh] for hh in range(2)]
    maxes = [m_ref[hh] for hh in range(2)] if running_max else None
    scores = score_pair(0)
    for sb in range(nsub):
        nxt = score_pair(sb + 1) if sb + 1 < nsub else None
        rows = slice(sb * tks, (sb + 1) * tks)
        for hh, sl in enumerate(heads):
            s = scores[hh]
            if running_max:
                m_new = jnp.maximum(maxes[hh], jnp.max(s, axis=0, keepdims=True))
                p = jnp.exp2(s - m_new)
            else:
                p = jnp.exp2(s)
            pv = lax.dot_general(v_ref[rows, sl], p.astype(BF16), (((0,), (0,)), ((), ())),
                                 preferred_element_type=F32)
            if running_max:
                accs[hh] = jnp.exp2(maxes[hh] - m_new) * accs[hh] + pv
                maxes[hh] = m_new
            else:
                accs[hh] = accs[hh] + pv
        scores = nxt
    for hh in range(2):
        acc_ref[hh] = accs[hh]
        if running_max:
            m_ref[hh] = maxes[hh]

    @pl.when(j == pl.num_programs(3) - 1)
    def _():
        outs = []
        for hh in range(2):
            a = acc_ref[hh]
            outs.append(a[:V_HEAD] / a[V_HEAD:V_HEAD + 1])
        o_ref[...] = jnp.concatenate(outs, axis=0).T.astype(BF16)


def _attn(q, k, v, bsz, seq, fast_softmax):
    return lax.cond(fast_softmax,
                    lambda q, k, v: _attn_call(q, k, v, bsz, seq, TQ, TK, TKS, False),
                    lambda q, k, v: _attn_call(q, k, v, bsz, seq, TQ_MAX, TK_MAX, TKS_MAX, True),
                    q, k, v)


def _attn_call(q, k, v, bsz, seq, tq, tk, tks, running_max):
    tq = min(tq, seq)
    tk = min(tk, seq)
    tks = min(tks, tk)
    nq, nk = seq // tq, seq // tk
    n = q.shape[0]
    return pl.pallas_call(
        functools.partial(_attn_kernel, running_max=running_max, tks=tks),
        grid=(bsz, MLA_HEADS // 2, nq, nk),
        in_specs=[pl.BlockSpec((tq, 2 * HEAD_PAD), lambda b, h, i, j: (b * nq + i, h)),
                  pl.BlockSpec((tk, 2 * HEAD_PAD), lambda b, h, i, j: (b * nk + j, h)),
                  pl.BlockSpec((tk, 2 * HEAD_PAD), lambda b, h, i, j: (b * nk + j, h))],
        out_specs=pl.BlockSpec((tq, 2 * V_HEAD), lambda b, h, i, j: (b * nq + i, h)),
        out_shape=jax.ShapeDtypeStruct((n, MLA_WIDTH), BF16),
        scratch_shapes=[pltpu.VMEM((2, 1, tq), F32), pltpu.VMEM((2, HEAD_PAD, tq), F32)],
        compiler_params=_cparams(("parallel", "parallel", "parallel", "arbitrary")),
        name="attn_max" if running_max else "attn",
    )(q, k, v)


def _rw_token_maps(u, prev_row, next_row, mu_ref, wdec_ref, w0_ref, aup_ref, a0_ref, gup_ref,
                   kk_ref, ka_ref, rk_ref, seg_ref,
                   r_out, k_out, v_out, kkn_out, kka_out, lwf_out, lwb_out, g_out, bonus_out):
    tm = u.shape[0]
    sub = lax.broadcasted_iota(jnp.int32, (8, 1), 0)
    prev = pltpu.roll(u, 1, 0)
    prev = jnp.concatenate([jnp.where(sub == 0, prev_row, prev[:8]), prev[8:]], axis=0)
    nxt = pltpu.roll(u, tm - 1, 0)
    nxt = jnp.concatenate([nxt[:tm - 8], jnp.where(sub == 7, next_row, nxt[tm - 8:])], axis=0)
    us = u + (0.5 * (prev + nxt) - u) * mu_ref[...]

    r = us[:, :RW_WIDTH]
    k = us[:, RW_WIDTH:2 * RW_WIDTH]
    v = us[:, 2 * RW_WIDTH:3 * RW_WIDTH]
    xw = us[:, 3 * RW_WIDTH:3 * RW_WIDTH + LANE]
    xa = us[:, 3 * RW_WIDTH + LANE:3 * RW_WIDTH + 2 * LANE]
    xg = us[:, 3 * RW_WIDTH + 2 * LANE:]

    wpre = w0_ref[...] + _dot(jnp.tanh(xw).astype(BF16), wdec_ref[...])
    lw = jax.nn.sigmoid(wpre) * (-math.exp(-0.5))
    lwf_out[...] = lw[:, :RW_WIDTH]
    lwb_out[...] = lw[:, RW_WIDTH:]
    a = jax.nn.sigmoid(a0_ref[...] + _dot(xa.astype(BF16), aup_ref[...]))
    g_out[...] = _dot(jax.nn.sigmoid(xg).astype(BF16), gup_ref[...])

    seg = seg_ref[...]
    kk = k * kk_ref[...]
    kkn = kk * lax.rsqrt(jnp.maximum(_split_dot(kk * kk, seg), 1e-24))
    k2 = k * (1.0 + (a - 1.0) * ka_ref[...])
    r_out[...] = r
    k_out[...] = k2
    v_out[...] = v
    kkn_out[...] = kkn
    kka_out[...] = kkn * a
    bonus_out[...] = _split_dot(r * k2 * rk_ref[...], seg) * v


def _bd(x, bd16):
    return jnp.concatenate([x.astype(BF16)] * HG, axis=0) * bd16


def _chunk_intra(r, k, v, kkn, kka, lw, reverse, consts):
    tri, strict_m, incl_m, eye_p, bd16, bd32, eye32 = consts
    lg = _split_dot_left(tri, lw)
    yield
    lgx = lg - lw
    tot = lg[0:1] if reverse else lg[CHUNK - 1:CHUNK]
    gi = jnp.exp(lg)
    ginv = jnp.exp(-lg)
    gend = jnp.exp(tot - lg)
    at = -kkn * jnp.exp(lgx)
    rt = r * gi
    bt = kka * ginv
    kt = k * ginv
    bh = kka * gend
    kh = k * gend

    lhs = jnp.concatenate([at, rt], axis=0).astype(BF16)
    rhs = jnp.concatenate([_bd(bt, bd16), _bd(kt, bd16)], axis=0)
    a_all = lax.dot_general(lhs, rhs, (((1,), (1,)), ((), ())), preferred_element_type=F32)
    yield
    n_ab = a_all[:CHUNK, :HGW] * strict_m
    a_ak = a_all[:CHUNK, HGW:] * strict_m
    a_rb = a_all[CHUNK:, :HGW] * incl_m
    a_rk = a_all[CHUNK:, HGW:] * incl_m

    t = eye_p + n_ab
    nk = _dot(n_ab.astype(BF16), _bd(n_ab, bd16))
    yield
    for _ in range(4):
        both = _dot(jnp.concatenate([nk, t], axis=0).astype(BF16), _bd(nk, bd16))
        yield
        nk = both[:CHUNK]
        t = t + both[CHUNK:]
    tn = _dot(t.astype(BF16), _bd(nk, bd16))
    yield
    t = t + tn

    tb = t.astype(BF16)
    vbd = _bd(v, bd16)
    w = _dot(tb, _bd(at, bd16))
    akv = _dot(a_ak.astype(BF16), vbd)
    yield
    uv = _dot(tb, _bd(akv, bd16))
    arb = a_rb.astype(BF16)
    qpd = _dot(arb, _bd(w, bd16))
    yield
    y_in = _dot(jnp.concatenate([arb, a_rk.astype(BF16)], axis=1),
                jnp.concatenate([_bd(uv, bd16), vbd], axis=0))
    lhs_t = jnp.concatenate([bh, kh], axis=0).astype(BF16)
    rhs_t = jnp.concatenate([jnp.concatenate([w, uv], axis=1),
                             jnp.concatenate([jnp.zeros_like(v), v], axis=1)], axis=0).astype(BF16)
    mc = lax.dot_general(lhs_t, rhs_t, (((0,), (0,)), ((), ())), preferred_element_type=F32)
    yield
    m_bd = mc[:, :HGW] * bd32 + eye32 * jnp.exp(tot)
    c_bd = mc[:, HGW:] * bd32
    return y_in, (rt + qpd).astype(BF16), m_bd.astype(BF16), c_bd


def _interleave(gens):
    results = [None] * len(gens)
    active = list(range(len(gens)))
    while active:
        still = []
        for i in active:
            try:
                next(gens[i])
                still.append(i)
            except StopIteration as stop:
                results[i] = stop.value
        active = still
    return results


def _split_dot_left(b_exact, a):
    hi = a.astype(BF16)
    lo = (a - hi.astype(F32)).astype(BF16)
    return _dot(b_exact, hi) + _dot(b_exact, lo)


def _rwscan_kernel(rf, kf, vf, nf, af, lf, rb, kb, vb, nb, ab, lb, tri_ref, m64_ref, bd16_ref, bd32_ref,
                   yf_out, yb_out, s_ref):
    c = pl.program_id(1)

    @pl.when(c == 0)
    def _():
        s_ref[...] = jnp.zeros(s_ref.shape, F32)

    nch = rf.shape[0] // CHUNK
    bd16 = bd16_ref[...]
    bd32 = bd32_ref[0]
    eye32 = bd32_ref[1]
    eye_p = m64_ref[4]
    ngrp = RW_HEADS // HG
    chains = []
    gens = []
    for d, refs, y_out in ((0, (rf, kf, vf, nf, af, lf), yf_out), (1, (rb, kb, vb, nb, ab, lb), yb_out)):
        reverse = d == 1
        consts = (tri_ref[d], m64_ref[2 * d], m64_ref[2 * d + 1], eye_p, bd16, bd32, eye32)
        order = list(range(nch - 1, -1, -1) if reverse else range(nch))
        for g in range(ngrp):
            chains.append((d, g, y_out, order))
            for j in order:
                ops = [x[j * CHUNK:(j + 1) * CHUNK, g * HGW:(g + 1) * HGW] for x in refs]
                gens.append(_chunk_intra(*ops, reverse, consts))
    parts = _interleave(gens)

    states = [s_ref[d, g] for d, g, _, _ in chains]
    for step in range(nch):
        for ci, (d, g, y_out, order) in enumerate(chains):
            j = order[step]
            y_in, qp, m_bd, c_bd = parts[ci * nch + step]
            sb = states[ci].astype(BF16)
            y_out[j * CHUNK:(j + 1) * CHUNK, g * HGW:(g + 1) * HGW] = y_in + _dot(qp, sb)
            states[ci] = _dot(m_bd, sb) + c_bd
    for ci, (d, g, _, _) in enumerate(chains):
        s_ref[d, g] = states[ci]


def _rwscan(r, k, v, kkn, kka, lwf, lwb, bsz, seq, p):
    n = r.shape[0]
    rows = min(NCH * CHUNK, seq)
    nb = seq // rows
    fwd = pl.BlockSpec((rows, RW_WIDTH), lambda b, c: (b * nb + c, 0))
    bwd = pl.BlockSpec((rows, RW_WIDTH), lambda b, c: (b * nb + nb - 1 - c, 0))
    full = lambda a: pl.BlockSpec(a.shape, lambda b, c: (0,) * a.ndim)
    sds = jax.ShapeDtypeStruct((n, RW_WIDTH), F32)
    return pl.pallas_call(
        _rwscan_kernel,
        grid=(bsz, nb),
        in_specs=[fwd] * 6 + [bwd] * 6 + [full(p['sc_tri']), full(p['sc_m64']), full(p['sc_bd16']),
                                           full(p['sc_bd32'])],
        out_specs=[fwd, bwd],
        out_shape=[sds, sds],
        scratch_shapes=[pltpu.VMEM((2, RW_HEADS // HG, HGW, HGW), F32)],
        compiler_params=_cparams(("parallel", "arbitrary")),
        name="rwscan",
    )(r, k, v, kkn, kka, lwf, r, k, v, kkn, kka, lwb, p['sc_tri'], p['sc_m64'], p['sc_bd16'], p['sc_bd32'])


def _post_kernel(x_ref, oa_ref, yf_ref, yb_ref, bonus_ref, g_ref, lng_ref, lnb_ref, seg_ref, wout_ref,
                 g2_ref, wr_ref, br_ref, x1_out, hp_out, gid_out):
    tm = x_ref.shape[0]
    nsplit = 2 if tm % 16 == 0 else 1
    rows_per = tm // nsplit

    def rows_gen(rs):
        seg = seg_ref[...]
        oa_part = _dot(oa_ref[rs, :], wout_ref[:MLA_WIDTH, :])
        y = yf_ref[rs, :] + yb_ref[rs, :]
        mean = _split_dot(y, seg) * (1.0 / RW_HEAD)
        yield
        dlt = y - mean
        var = _split_dot(dlt * dlt, seg) * (1.0 / RW_HEAD)
        yield
        yn = dlt * lax.rsqrt(var + LN_X_EPS) * lng_ref[...] + lnb_ref[...]
        ob = (yn + bonus_ref[rs, :]) * g_ref[rs, :]
        x1 = x_ref[rs, :] + oa_part + _dot(ob.astype(BF16), wout_ref[MLA_WIDTH:, :])
        yield
        x1_out[rs, :] = x1
        h2 = x1 * lax.rsqrt(jnp.mean(x1 * x1, axis=-1, keepdims=True) + RMS_EPS) * g2_ref[...]
        h2b = h2.astype(BF16)
        bits = lax.bitcast_convert_type(h2b.astype(F32), jnp.uint32)
        hp_out[rs, :] = (bits[:, :D_MODEL // 2] >> 16) | (bits[:, D_MODEL // 2:] & jnp.uint32(0xFFFF0000))
        logits = _dot(h2b, wr_ref[...]) + br_ref[...]
        yield
        _, gidx, _ = _group_choice(logits)
        gid_out[rs, :] = jnp.broadcast_to(gidx, (rows_per, LANE))

    _interleave([rows_gen(slice(j * rows_per, (j + 1) * rows_per)) for j in range(nsplit)])


def _group_choice(logits):
    lane_i = lax.broadcasted_iota(jnp.int32, logits.shape, 1)
    lane = lane_i.astype(F32)
    is_g = (lane_i >= N_EXPERTS) & (lane_i < N_EXPERTS + N_GROUPS)
    gl = jnp.where(is_g, logits, -jnp.inf)
    gmax = jnp.max(gl, axis=-1, keepdims=True)
    gidx = jnp.min(jnp.where(gl == gmax, lane, 1e9), axis=-1, keepdims=True) - N_EXPERTS
    p_g = 1.0 / jnp.sum(jnp.where(is_g, jnp.exp(gl - gmax), 0.0), axis=-1, keepdims=True)
    return p_g, gidx, lane


def _post(x2, oa, yf, yb, bonus, g, p, tm):
    n = x2.shape[0]
    tm = min(tm, n)
    row = lambda i: (i, 0)
    fixed = lambda i: (0, 0)
    full = lambda a: pl.BlockSpec(a.shape, fixed)
    rw = pl.BlockSpec((tm, RW_WIDTH), row)
    return pl.pallas_call(
        _post_kernel,
        grid=(n // tm,),
        in_specs=[pl.BlockSpec((tm, D_MODEL), row), pl.BlockSpec((tm, MLA_WIDTH), row), rw, rw, rw, rw,
                  full(p['ln_g']), full(p['ln_b']), full(p['seg']), full(p['w_out']), full(p['g2']),
                  full(p['w_r']), full(p['b_r'])],
        out_specs=[pl.BlockSpec((tm, D_MODEL), row), pl.BlockSpec((tm, D_MODEL // 2), row),
                   pl.BlockSpec((tm, LANE), row)],
        out_shape=[jax.ShapeDtypeStruct((n, D_MODEL), F32), jax.ShapeDtypeStruct((n, D_MODEL // 2), jnp.uint32),
                   jax.ShapeDtypeStruct((n, LANE), F32)],
        compiler_params=_cparams(("parallel",)),
        name="post",
    )(x2, oa, yf, yb, bonus, g, p['ln_g'], p['ln_b'], p['seg'], p['w_out'], p['g2'], p['w_r'], p['b_r'])


def _route_plan(gid, tm, npt):
    g = gid[:, 0].astype(jnp.int32)
    onehot = (g[:, None] == jnp.arange(N_GROUPS, dtype=jnp.int32)[None, :]).astype(jnp.int32)
    csum = jnp.cumsum(onehot, axis=0)
    rank = jnp.sum(csum * onehot, axis=1) - 1
    ntile = (csum[-1] + tm - 1) // tm
    tile_end = jnp.cumsum(ntile)
    off = (tile_end - ntile) * tm
    pos = jnp.sum(onehot * off[None, :], axis=1) + rank
    n_active = tile_end[-1]
    t = jnp.minimum(jnp.arange(npt, dtype=jnp.int32), n_active - 1)
    tile_gid = jnp.sum((t[:, None] >= tile_end[None, :]).astype(jnp.int32), axis=1)
    meta = jnp.concatenate([n_active[None], tile_gid]).astype(jnp.int32)
    return pos.astype(jnp.int32), meta


def _row_dma_loops(n_rows, make_copy):
    def issue(h, carry):
        make_copy(2 * h).start(priority=0)
        make_copy(2 * h + 1).start(priority=1)
        return carry
    lax.fori_loop(0, n_rows // 2, issue, 0, unroll=4)

    def drain(r, carry):
        make_copy(0).wait()
        return carry
    lax.fori_loop(0, n_rows, drain, 0, unroll=8)


def _scatter_kernel(pos_ref, src_ref, init_ref, dst_ref, sem):
    _row_dma_loops(src_ref.shape[0], lambda r: pltpu.make_async_copy(
        src_ref.at[pl.ds(r, 1)], dst_ref.at[pl.ds(pos_ref[0, 0, r], 1)], sem))


def _scatter_rows(pos, src, n_dst, tms):
    n, w = src.shape
    tms = min(tms, n)
    return pl.pallas_call(
        _scatter_kernel,
        grid=(n // tms,),
        in_specs=[pl.BlockSpec((1, 1, tms), lambda i: (i, 0, 0), memory_space=pltpu.SMEM),
                  pl.BlockSpec((tms, w), lambda i: (i, 0)), pl.BlockSpec(memory_space=pl.ANY)],
        out_specs=pl.BlockSpec(memory_space=pl.ANY),
        out_shape=jax.ShapeDtypeStruct((n_dst, w), src.dtype),
        scratch_shapes=[pltpu.SemaphoreType.DMA(())],
        input_output_aliases={2: 0},
        compiler_params=_cparams(("arbitrary",)),
        name="moe_scatter",
    )(pos.reshape(n // tms, 1, tms), src, jnp.zeros((n_dst, w), src.dtype))


def _moe_group_kernel(meta_ref, xs_ref, wr_ref, br_ref, wg_ref, wu_ref, wd_ref, ys_ref):
    i = pl.program_id(0)

    @pl.when(i >= meta_ref[0])
    def _():
        ys_ref[...] = jnp.zeros(ys_ref.shape, F32)

    @pl.when(i < meta_ref[0])
    def _():
        tm = xs_ref.shape[0]
        word = xs_ref[...]
        h2 = jnp.concatenate([lax.bitcast_convert_type(word << 16, F32),
                              lax.bitcast_convert_type(word & jnp.uint32(0xFFFF0000), F32)],
                             axis=1).astype(BF16)
        def up(c):
            ks = range(c * MOE_EPC, (c + 1) * MOE_EPC)
            return (jnp.concatenate([_dot(h2, wg_ref[k]) for k in ks], axis=1),
                    jnp.concatenate([_dot(h2, wu_ref[k]) for k in ks], axis=1))

        pend = up(0)
        logits = _dot(h2, wr_ref[...]) + br_ref[...]
        p_g, _, lane = _group_choice(logits)
        first = (meta_ref[1 + i] * EXPERTS_PER_GROUP).astype(F32)
        in_grp = (lane >= first) & (lane < first + EXPERTS_PER_GROUP)
        el = jnp.where(in_grp, logits, -jnp.inf)
        v1 = jnp.max(el, axis=-1, keepdims=True)
        i1 = jnp.min(jnp.where(el == v1, lane, 1e9), axis=-1, keepdims=True)
        el2 = jnp.where(lane == i1, -jnp.inf, el)
        v2 = jnp.max(el2, axis=-1, keepdims=True)
        i2 = jnp.min(jnp.where(el2 == v2, lane, 1e9), axis=-1, keepdims=True)
        e2 = jnp.exp(v2 - v1)
        den = 1.0 + e2
        w1 = (1.0 / den) * p_g
        w2 = (e2 / den) * p_g

        acc = None
        nchunk = EXPERTS_PER_GROUP // MOE_EPC
        for c in range(nchunk):
            nxt = up(c + 1) if c + 1 < nchunk else None
            hg, hu = pend
            gate = jnp.concatenate(
                [jnp.broadcast_to(jnp.where(i1 == first + k, w1, 0.0) + jnp.where(i2 == first + k, w2, 0.0),
                                  (tm, D_EXPERT)) for k in range(c * MOE_EPC, (c + 1) * MOE_EPC)], axis=1)
            act = hg * jax.nn.sigmoid(hg) * hu * gate
            wd = wd_ref[c * MOE_EPC:(c + 1) * MOE_EPC].reshape(MOE_EPC * D_EXPERT, D_MODEL)
            part = _dot(act.astype(BF16), wd)
            acc = part if acc is None else acc + part
            pend = nxt
        ys_ref[...] = acc


def _moe_grouped(xs, meta, p, tm, npt):
    def tile(i, m):
        return (jnp.minimum(i, m[0] - 1), 0)

    def group(i, m):
        return (m[1 + i], 0, 0)

    grid_spec = pltpu.PrefetchScalarGridSpec(
        num_scalar_prefetch=1,
        grid=(npt,),
        in_specs=[pl.BlockSpec((tm, D_MODEL // 2), tile),
                  pl.BlockSpec(p['w_r'].shape, lambda i, m: (0, 0)),
                  pl.BlockSpec(p['b_r'].shape, lambda i, m: (0, 0)),
                  pl.BlockSpec((EXPERTS_PER_GROUP, D_MODEL, D_EXPERT), group),
                  pl.BlockSpec((EXPERTS_PER_GROUP, D_MODEL, D_EXPERT), group),
                  pl.BlockSpec((EXPERTS_PER_GROUP, D_EXPERT, D_MODEL), group)],
        out_specs=pl.BlockSpec((tm, D_MODEL), lambda i, m: (i, 0)),
    )
    return pl.pallas_call(
        _moe_group_kernel,
        grid_spec=grid_spec,
        out_shape=jax.ShapeDtypeStruct((npt * tm, D_MODEL), F32),
        compiler_params=_cparams(("arbitrary",)),
        name="moe_group",
    )(meta, xs, p['w_r'], p['b_r'], p['wg'], p['wu'], p['wd'])


def _gather_kernel(pos_ref, x1_ref, ys_ref, o_ref, buf, sem):
    _row_dma_loops(buf.shape[0], lambda r: pltpu.make_async_copy(
        ys_ref.at[pl.ds(pos_ref[0, 0, r], 1)], buf.at[pl.ds(r, 1)], sem))
    o_ref[...] = x1_ref[...] + buf[...]


def _gather_add(pos, x1, ys, tms):
    n, d = x1.shape
    tms = min(tms, n)
    return pl.pallas_call(
        _gather_kernel,
        grid=(n // tms,),
        in_specs=[pl.BlockSpec((1, 1, tms), lambda i: (i, 0, 0), memory_space=pltpu.SMEM),
                  pl.BlockSpec((tms, d), lambda i: (i, 0)),
                  pl.BlockSpec(memory_space=pl.ANY)],
        out_specs=pl.BlockSpec((tms, d), lambda i: (i, 0)),
        out_shape=jax.ShapeDtypeStruct((n, d), F32),
        scratch_shapes=[pltpu.VMEM((tms, d), F32), pltpu.SemaphoreType.DMA(())],
        compiler_params=_cparams(("arbitrary",)),
        name="moe_gather",
    )(pos.reshape(n // tms, 1, tms), x1, ys)


def _head_pad_cols(w, head_w, lo, hi, dst, width=HEAD_PAD, perm=None):
    rows = w.shape[0]
    wh = w.reshape(rows, -1, head_w)[:, :, lo:hi]
    if perm is not None:
        wh = wh[:, :, perm]
    out = jnp.zeros((rows, wh.shape[1], width), w.dtype)
    return out.at[:, :, dst:dst + (hi - lo)].set(wh).reshape(rows, -1)


def _pack_params(lp, max_seq):
    (norm1_g, w_in, q_a_norm_g, w_uq, kv_a_norm_g, w_ukv, q_norm_g, k_norm_g, mu_shift, w0_f, w_up_f,
     w0_b, w_up_b, a0, a_up, g_up, k_k, k_a, r_k, ln_x_g, ln_x_b, w_out, norm2_g, w_router_group,
     b_router_group, w_router_expert, b_router_expert, w_expert_gate, w_expert_up, w_expert_down) = lp
    half = QK_ROPE // 2
    swap = jnp.concatenate([jnp.arange(half, QK_ROPE), jnp.arange(half)])
    zcol = lambda rows, n: jnp.zeros((rows, n), F32)
    p = {}
    c0 = Q_LORA + KV_LORA
    w_kr = w_in[:, c0:c0 + QK_ROPE]
    u0 = c0 + QK_ROPE
    wu_ = w_in[:, u0:]
    d = D_MODEL
    w_in_p = jnp.concatenate([
        w_in[:, :c0],
        zcol(d, QK_NOPE), w_kr, zcol(d, LANE - QK_HEAD),
        zcol(d, QK_NOPE), w_kr[:, swap], zcol(d, LANE - QK_HEAD),
        wu_[:, :3 * RW_WIDTH + 2 * DECAY_LORA + A_LORA], zcol(d, LANE - A_LORA),
        wu_[:, 3 * RW_WIDTH + 2 * DECAY_LORA + A_LORA:]], axis=1)
    p['w_in'] = w_in_p.astype(BF16)
    mu = mu_shift[None, :]
    p['mu'] = jnp.concatenate([mu[:, :3 * RW_WIDTH + 2 * DECAY_LORA + A_LORA], zcol(1, LANE - A_LORA),
                               mu[:, 3 * RW_WIDTH + 2 * DECAY_LORA + A_LORA:]], axis=1)
    p['g1'] = norm1_g[None, :]
    p['qag'] = q_a_norm_g[None, :]
    p['kvag'] = kv_a_norm_g[None, :]
    p['w_uq'] = jnp.concatenate([
        _head_pad_cols(w_uq, QK_HEAD, 0, QK_HEAD, 0),
        _head_pad_cols(w_uq, QK_HEAD, QK_NOPE, QK_HEAD, QK_NOPE, perm=swap)], axis=1).astype(BF16)
    p['w_ukv'] = jnp.concatenate([
        _head_pad_cols(w_ukv, QK_NOPE + V_HEAD, 0, QK_NOPE, 0),
        _head_pad_cols(w_ukv, QK_NOPE + V_HEAD, QK_NOPE, QK_NOPE + V_HEAD, 0)], axis=1).astype(BF16)

    def gain_rows(g):
        main = jnp.concatenate([g, jnp.zeros((LANE - QK_HEAD,), F32)])
        swp = jnp.concatenate([jnp.zeros((QK_NOPE,), F32), g[QK_NOPE:][swap], jnp.zeros((LANE - QK_HEAD,), F32)])
        return jnp.stack([main, swp])
    kbound = math.sqrt(QK_HEAD) * jnp.max(jnp.abs(k_norm_g))
    qbound = jnp.max(jnp.abs(q_norm_g)) * LOG2E
    p['gq'] = jnp.concatenate([gain_rows(q_norm_g), jnp.full((1, LANE), qbound * kbound, F32)], axis=0)
    p['gk'] = gain_rows(k_norm_g)
    p['fast_softmax'] = 2.0 * qbound * kbound <= FAST_SOFTMAX_MAX_SHIFT
    inv_freq = 1.0 / (ROPE_THETA ** (jnp.arange(half, dtype=F32) / half))
    ang = jnp.arange(max_seq, dtype=F32)[:, None] * inv_freq[None, :]
    cos, sin = jnp.cos(ang), jnp.sin(ang)
    zpad = jnp.zeros((max_seq, LANE - QK_HEAD), F32)
    p['ct'] = jnp.concatenate([jnp.ones((max_seq, QK_NOPE), F32), cos, cos, zpad], axis=1)
    p['st'] = jnp.concatenate([jnp.zeros((max_seq, QK_NOPE), F32), -sin, sin, zpad], axis=1)

    zl = jnp.zeros((DECAY_LORA, RW_WIDTH), F32)
    p['w_dec'] = jnp.concatenate([jnp.concatenate([w_up_f, zl], axis=1),
                                  jnp.concatenate([zl, w_up_b], axis=1)], axis=0).astype(BF16)
    p['w0'] = jnp.concatenate([w0_f, w0_b])[None, :]
    p['a_up'] = jnp.concatenate([a_up, jnp.zeros((LANE - A_LORA, RW_WIDTH), F32)], axis=0).astype(BF16)
    p['a0'] = a0[None, :]
    p['g_up'] = g_up.astype(BF16)
    p['k_k'] = k_k[None, :]
    p['k_a'] = k_a[None, :]
    p['r_k'] = r_k.reshape(1, RW_WIDTH)
    hid = jnp.arange(RW_WIDTH) // RW_HEAD
    p['seg'] = (hid[:, None] == hid[None, :]).astype(BF16)
    ti = jnp.arange(CHUNK)[:, None]
    tj = jnp.arange(CHUNK)[None, :]
    p['sc_tri'] = jnp.stack([tj <= ti, tj >= ti]).astype(BF16)
    ps = (jnp.arange(HGW) % CHUNK)[None, :]
    p['sc_m64'] = jnp.stack([ps < ti, ps <= ti, ps > ti, ps >= ti, ps == ti]).astype(F32)
    blk = jnp.arange(HGW) // RW_HEAD
    same = blk[:, None] == blk[None, :]
    p['sc_bd16'] = same.astype(BF16)
    p['sc_bd32'] = jnp.stack([same, jnp.eye(HGW, dtype=bool)]).astype(F32)
    p['ln_g'] = ln_x_g[None, :]
    p['ln_b'] = ln_x_b[None, :]
    p['w_out'] = w_out.astype(BF16)
    p['g2'] = norm2_g[None, :]
    p['w_r'] = jnp.concatenate([w_router_expert, w_router_group,
                                zcol(d, LANE - N_EXPERTS - N_GROUPS)], axis=1).astype(BF16)
    p['b_r'] = jnp.concatenate([b_router_expert, b_router_group,
                                jnp.zeros((LANE - N_EXPERTS - N_GROUPS,), F32)])[None, :]
    p['wg'] = w_expert_gate.astype(BF16)
    p['wu'] = w_expert_up.astype(BF16)
    p['wd'] = w_expert_down.astype(BF16)
    return p


TM_PROJ = 512
TQ = 1024
TK = 8192
TKS = 2048
TQ_MAX = 1024
TK_MAX = 2048
TKS_MAX = 1024
TM_POST = 512
TM_MOE = 512
TM_ROWS = 512


def _layer(x, p):
    bsz, seq, d = x.shape
    n = bsz * seq
    x2 = x.reshape(n, d)
    q, k, v, r, k2, vv, kkn, kka, lwf, lwb, g, bonus = _proj(x2, seq, p, TM_PROJ)
    oa = _attn(q, k, v, bsz, seq, p['fast_softmax'])
    yf, yb = _rwscan(r, k2, vv, kkn, kka, lwf, lwb, bsz, seq, p)
    x1, hp, gid = _post(x2, oa, yf, yb, bonus, g, p, TM_POST)
    tm = min(TM_MOE, n)
    npt = n // tm + N_GROUPS
    pos, meta = _route_plan(gid, tm, npt)
    xs = _scatter_rows(pos, hp, npt * tm, TM_ROWS)
    ys = _moe_grouped(xs, meta, p, tm, npt)
    out = _gather_add(pos, x1, ys, TM_ROWS)
    return out.reshape(bsz, seq, d)


def kernel(x_prompt, x_sample, norm1_g, w_in, q_a_norm_g, w_uq, kv_a_norm_g, w_ukv, q_norm_g, k_norm_g, mu_shift, w0_f, w_up_f, w0_b, w_up_b, a0, a_up, g_up, k_k, k_a, r_k, ln_x_g, ln_x_b, w_out, norm2_g, w_router_group, b_router_group, w_router_expert, b_router_expert, w_expert_gate, w_expert_up, w_expert_down):
    layer_params = (norm1_g, w_in, q_a_norm_g, w_uq, kv_a_norm_g, w_ukv, q_norm_g, k_norm_g, mu_shift,
                    w0_f, w_up_f, w0_b, w_up_b, a0, a_up, g_up, k_k, k_a, r_k, ln_x_g, ln_x_b, w_out,
                    norm2_g, w_router_group, b_router_group, w_router_expert, b_router_expert,
                    w_expert_gate, w_expert_up, w_expert_down)
    y_prompt, y_sample = x_prompt, x_sample
    max_seq = max(x_prompt.shape[1], x_sample.shape[1])
    for layer in range(norm1_g.shape[0]):
        p = _pack_params([w[layer] for w in layer_params], max_seq)
        y_prompt = _layer(y_prompt, p)
        y_sample = _layer(y_sample, p)
    return (y_prompt, y_sample)
```

```python
import functools
import math

import jax
import jax.numpy as jnp
from jax import lax
from jax.experimental import pallas as pl
from jax.experimental.pallas import tpu as pltpu

D_MODEL = 1024
MLA_HEADS = 8
QK_NOPE = 64
QK_ROPE = 32
QK_HEAD = QK_NOPE + QK_ROPE
V_HEAD = 64
Q_LORA = 384
KV_LORA = 256
ROPE_THETA = 10000.0
RW_HEADS = 8
RW_HEAD = 64
RW_WIDTH = RW_HEADS * RW_HEAD
DECAY_LORA = 64
A_LORA = 64
GATE_LORA = 128
LN_X_EPS = 6.4e-4
MLA_WIDTH = MLA_HEADS * V_HEAD
N_GROUPS = 4
EXPERTS_PER_GROUP = 8
N_EXPERTS = N_GROUPS * EXPERTS_PER_GROUP
D_EXPERT = 256
MOE_EPC = 2
RMS_EPS = 1e-6

LANE = 128
HEAD_PAD = LANE
MLA_PAD = MLA_HEADS * HEAD_PAD
RW_COLS_PAD = 3 * RW_WIDTH + 3 * LANE
MLA_COLS_PAD = Q_LORA + KV_LORA + 2 * LANE
D_IN_PAD = MLA_COLS_PAD + RW_COLS_PAD
CHUNK = 64
HG = 4
HGW = HG * RW_HEAD
VMEM_LIMIT = 56 * 1024 * 1024
LOG2E = 1.4426950408889634
NCH = 8
FAST_SOFTMAX_MAX_SHIFT = 100.0

F32 = jnp.float32
BF16 = jnp.bfloat16


def _dot(a, b):
    return jnp.dot(a, b, preferred_element_type=F32)


def _split_dot(a, b_exact):
    hi = a.astype(BF16)
    lo = (a - hi.astype(F32)).astype(BF16)
    return _dot(hi, b_exact) + _dot(lo, b_exact)


def _cparams(sem):
    return pltpu.CompilerParams(dimension_semantics=sem, vmem_limit_bytes=VMEM_LIMIT)


def _proj_kernel(x_ref, xp_ref, xn_ref, g1_ref, win_ref, qag_ref, wuq_ref, kvag_ref, wukv_ref, gq_ref, gk_ref,
                 ct_ref, st_ref, *rw_refs, tm, seq):
    q_out, k_out, v_out = rw_refs[10:13]
    i = pl.program_id(0)
    x = jnp.concatenate([xp_ref[0], x_ref[...], xn_ref[0]], axis=0)
    h = x * lax.rsqrt(jnp.mean(x * x, axis=-1, keepdims=True) + RMS_EPS) * g1_ref[...]
    ze = _dot(h.astype(BF16), win_ref[...])
    first = (i * tm) % seq == 0
    last = ((i + 1) * tm) % seq == 0

    gq = gq_ref[...]
    gk = gk_ref[...]
    scale = QK_HEAD ** -0.5 * LOG2E
    lane = lax.broadcasted_iota(jnp.int32, (1, LANE), 1)
    ones_col = jnp.where(lane == V_HEAD, 1.0, 0.0).astype(F32)
    shift_col = jnp.where(lane == QK_HEAD, 1.0, 0.0).astype(F32)
    q_shift = shift_col * gq[2:3, 0:1]

    def rows_gen(a, b):
        rs = slice(a, b)
        z = ze[8 + a:8 + b]
        prev_row = ze[7 + a:8 + a, MLA_COLS_PAD:]
        next_row = ze[8 + b:9 + b, MLA_COLS_PAD:]
        if a == 0:
            prev_row = jnp.where(first, 0.0, prev_row)
        if b == tm:
            next_row = jnp.where(last, 0.0, next_row)
        yield from _rw_token_maps(z[:, MLA_COLS_PAD:], prev_row, next_row, rs, *rw_refs[:10], *rw_refs[13:])

        cq = z[:, :Q_LORA]
        cqn = cq * lax.rsqrt(jnp.mean(cq * cq, axis=-1, keepdims=True) + RMS_EPS) * qag_ref[...]
        q2 = _dot(cqn.astype(BF16), wuq_ref[...])
        ckv = z[:, Q_LORA:Q_LORA + KV_LORA]
        ckvn = ckv * lax.rsqrt(jnp.mean(ckv * ckv, axis=-1, keepdims=True) + RMS_EPS) * kvag_ref[...]
        kv2 = _dot(ckvn.astype(BF16), wukv_ref[...])
        yield
        kr_main = z[:, Q_LORA + KV_LORA:Q_LORA + KV_LORA + LANE]
        kr_swap = z[:, Q_LORA + KV_LORA + LANE:MLA_COLS_PAD]
        ct = ct_ref[rs, :]
        st = st_ref[rs, :]
        kr_ssq = jnp.sum(kr_main * kr_main, axis=-1, keepdims=True)
        q_main_f = gq[0:1] * ct
        q_swap_f = gq[1:2] * st
        k_rot = kr_main * (gk[0:1] * ct) + kr_swap * (gk[1:2] * st)
        for hd in range(MLA_HEADS):
            sl = slice(hd * HEAD_PAD, (hd + 1) * HEAD_PAD)
            qm = q2[:, sl]
            qs = q2[:, MLA_PAD + hd * HEAD_PAD:MLA_PAD + (hd + 1) * HEAD_PAD]
            rinv = lax.rsqrt(jnp.sum(qm * qm, axis=-1, keepdims=True) * (1.0 / QK_HEAD) + RMS_EPS)
            qh = (qm * q_main_f + qs * q_swap_f) * (rinv * scale)
            q_out[rs, sl] = (qh - q_shift).astype(BF16)
            kn = kv2[:, sl]
            rinv_k = lax.rsqrt((jnp.sum(kn * kn, axis=-1, keepdims=True) + kr_ssq) * (1.0 / QK_HEAD) + RMS_EPS)
            kh = (kn * gk[0:1] + k_rot) * rinv_k
            k_out[rs, sl] = (kh + shift_col).astype(BF16)
            v_out[rs, sl] = (kv2[:, MLA_PAD + hd * HEAD_PAD:MLA_PAD + (hd + 1) * HEAD_PAD] + ones_col).astype(BF16)

    nsplit = PROJ_SPLIT if tm % (8 * PROJ_SPLIT) == 0 else 1
    rows_per = tm // nsplit
    _interleave([rows_gen(j * rows_per, (j + 1) * rows_per) for j in range(nsplit)])


def _proj(x2, seq, p, tm):
    n = x2.shape[0]
    tm = min(tm, seq)
    nseq_t = seq // tm
    row = lambda i: (i, 0)
    fixed = lambda i: (0, 0)
    pos = lambda i: (i % nseq_t, 0)
    full = lambda a: pl.BlockSpec(a.shape, fixed)
    g8 = tm // 8
    ngrp = n // 8
    x3 = x2.reshape(ngrp, 8, D_MODEL)
    rw_names = ('mu', 'w_dec', 'w0', 'a_up', 'a0', 'g_up', 'k_k', 'k_a', 'r_k', 'seg')
    mla_sds = jax.ShapeDtypeStruct((n, MLA_PAD), BF16)
    rw_sds = jax.ShapeDtypeStruct((n, RW_WIDTH), F32)
    return pl.pallas_call(
        functools.partial(_proj_kernel, tm=tm, seq=seq),
        grid=(n // tm,),
        in_specs=[pl.BlockSpec((tm, D_MODEL), row),
                  pl.BlockSpec((1, 8, D_MODEL), lambda i: (jnp.maximum(i * g8 - 1, 0), 0, 0)),
                  pl.BlockSpec((1, 8, D_MODEL), lambda i: (jnp.minimum((i + 1) * g8, ngrp - 1), 0, 0)),
                  full(p['g1']), full(p['w_in']), full(p['qag']),
                  full(p['w_uq']), full(p['kvag']), full(p['w_ukv']), full(p['gq']), full(p['gk']),
                  pl.BlockSpec((tm, LANE), pos), pl.BlockSpec((tm, LANE), pos)]
                 + [full(p[k]) for k in rw_names],
        out_specs=[pl.BlockSpec((tm, MLA_PAD), row)] * 3 + [pl.BlockSpec((tm, RW_WIDTH), row)] * 9,
        out_shape=[mla_sds] * 3 + [rw_sds] * 9,
        compiler_params=_cparams(("parallel",)),
        name="proj",
    )(x2, x3, x3, p['g1'], p['w_in'], p['qag'], p['w_uq'], p['kvag'], p['w_ukv'], p['gq'], p['gk'],
      p['ct'][:seq], p['st'][:seq], *[p[k] for k in rw_names])


def _attn_kernel(q_ref, k_ref, v_ref, o_ref, m_ref, acc_ref, *, running_max, tks):
    j = pl.program_id(3)

    @pl.when(j == 0)
    def _():
        if running_max:
            m_ref[...] = jnp.full(m_ref.shape, -jnp.inf, F32)
        acc_ref[...] = jnp.zeros(acc_ref.shape, F32)

    heads = [slice(hh * HEAD_PAD, (hh + 1) * HEAD_PAD) for hh in range(2)]
    nsub = k_ref.shape[0] // tks

    def score_pair(sb):
        rows = slice(sb * tks, (sb + 1) * tks)
        return [lax.dot_general(k_ref[rows, sl], q_ref[:, sl], (((1,), (1,)), ((), ())),
                                preferred_element_type=F32) for sl in heads]

    accs = [acc_ref[hh] for hh in range(2)]
    maxes = [m_ref[hh] for hh in range(2)] if running_max else None
    scores = score_pair(0)
    for sb in range(nsub):
        nxt = score_pair(sb + 1) if sb + 1 < nsub else None
        rows = slice(sb * tks, (sb + 1) * tks)
        for hh, sl in enumerate(heads):
            s = scores[hh]
            if running_max:
                m_new = jnp.maximum(maxes[hh], jnp.max(s, axis=0, keepdims=True))
                p = jnp.exp2(s - m_new)
            else:
                p = jnp.exp2(s)
            pv = lax.dot_general(v_ref[rows, sl], p.astype(BF16), (((0,), (0,)), ((), ())),
                                 preferred_element_type=F32)
            if running_max:
                accs[hh] = jnp.exp2(maxes[hh] - m_new) * accs[hh] + pv
                maxes[hh] = m_new
            else:
                accs[hh] = accs[hh] + pv
        scores = nxt
    for hh in range(2):
        acc_ref[hh] = accs[hh]
        if running_max:
            m_ref[hh] = maxes[hh]

    @pl.when(j == pl.num_programs(3) - 1)
    def _():
        outs = []
        for hh in range(2):
            a = acc_ref[hh]
            outs.append(a[:V_HEAD] / a[V_HEAD:V_HEAD + 1])
        o_ref[...] = jnp.concatenate(outs, axis=0).T.astype(BF16)


def _attn(q, k, v, bsz, seq, fast_softmax):
    return lax.cond(fast_softmax,
                    lambda q, k, v: _attn_call(q, k, v, bsz, seq, TQ, TK, TKS, False),
                    lambda q, k, v: _attn_call(q, k, v, bsz, seq, TQ_MAX, TK_MAX, TKS_MAX, True),
                    q, k, v)


def _attn_call(q, k, v, bsz, seq, tq, tk, tks, running_max):
    tq = min(tq, seq)
    tk = min(tk, seq)
    tks = min(tks, tk)
    nq, nk = seq // tq, seq // tk
    n = q.shape[0]
    return pl.pallas_call(
        functools.partial(_attn_kernel, running_max=running_max, tks=tks),
        grid=(bsz, MLA_HEADS // 2, nq, nk),
        in_specs=[pl.BlockSpec((tq, 2 * HEAD_PAD), lambda b, h, i, j: (b * nq + i, h)),
                  pl.BlockSpec((tk, 2 * HEAD_PAD), lambda b, h, i, j: (b * nk + j, h)),
                  pl.BlockSpec((tk, 2 * HEAD_PAD), lambda b, h, i, j: (b * nk + j, h))],
        out_specs=pl.BlockSpec((tq, 2 * V_HEAD), lambda b, h, i, j: (b * nq + i, h)),
        out_shape=jax.ShapeDtypeStruct((n, MLA_WIDTH), BF16),
        scratch_shapes=[pltpu.VMEM((2, 1, tq), F32), pltpu.VMEM((2, HEAD_PAD, tq), F32)],
        compiler_params=_cparams(("parallel", "parallel", "parallel", "arbitrary")),
        name="attn_max" if running_max else "attn",
    )(q, k, v)


def _rw_token_maps(u, prev_row, next_row, rs, mu_ref, wdec_ref, w0_ref, aup_ref, a0_ref, gup_ref,
                   kk_ref, ka_ref, rk_ref, seg_ref,
                   r_out, k_out, v_out, kkn_out, kka_out, lwf_out, lwb_out, g_out, bonus_out):
    tm = u.shape[0]
    sub = lax.broadcasted_iota(jnp.int32, (8, 1), 0)
    prev = pltpu.roll(u, 1, 0)
    prev = jnp.concatenate([jnp.where(sub == 0, prev_row, prev[:8]), prev[8:]], axis=0)
    nxt = pltpu.roll(u, tm - 1, 0)
    nxt = jnp.concatenate([nxt[:tm - 8], jnp.where(sub == 7, next_row, nxt[tm - 8:])], axis=0)
    us = u + (0.5 * (prev + nxt) - u) * mu_ref[...]

    r = us[:, :RW_WIDTH]
    k = us[:, RW_WIDTH:2 * RW_WIDTH]
    v = us[:, 2 * RW_WIDTH:3 * RW_WIDTH]
    xw = us[:, 3 * RW_WIDTH:3 * RW_WIDTH + LANE]
    xa = us[:, 3 * RW_WIDTH + LANE:3 * RW_WIDTH + 2 * LANE]
    xg = us[:, 3 * RW_WIDTH + 2 * LANE:]
    r_out[rs, :] = r
    v_out[rs, :] = v

    wdot = _dot(jnp.tanh(xw).astype(BF16), wdec_ref[...])
    adot = _dot(xa.astype(BF16), aup_ref[...])
    gdot = _dot(jax.nn.sigmoid(xg).astype(BF16), gup_ref[...])
    seg = seg_ref[...]
    kk = k * kk_ref[...]
    kss = _split_dot(kk * kk, seg)
    yield
    lw = jax.nn.sigmoid(w0_ref[...] + wdot) * (-math.exp(-0.5))
    lwf_out[rs, :] = lw[:, :RW_WIDTH]
    lwb_out[rs, :] = lw[:, RW_WIDTH:]
    a = jax.nn.sigmoid(a0_ref[...] + adot)
    g_out[rs, :] = gdot
    kkn = kk * lax.rsqrt(jnp.maximum(kss, 1e-24))
    k2 = k * (1.0 + (a - 1.0) * ka_ref[...])
    k_out[rs, :] = k2
    kkn_out[rs, :] = kkn
    kka_out[rs, :] = kkn * a
    bsum = _split_dot(r * k2 * rk_ref[...], seg)
    yield
    bonus_out[rs, :] = bsum * v


def _bd(x, bd16):
    return jnp.concatenate([x.astype(BF16)] * HG, axis=0) * bd16


def _chunk_intra(r, k, v, kkn, kka, lw, reverse, consts):
    tri, strict_m, incl_m, eye_p, bd16, bd32, eye32 = consts
    lg = _split_dot_left(tri, lw)
    yield
    lgx = lg - lw
    tot = lg[0:1] if reverse else lg[CHUNK - 1:CHUNK]
    gi = jnp.exp(lg)
    ginv = jnp.exp(-lg)
    gend = jnp.exp(tot - lg)
    at = -kkn * jnp.exp(lgx)
    rt = r * gi
    bt = kka * ginv
    kt = k * ginv
    bh = kka * gend
    kh = k * gend

    lhs = jnp.concatenate([at, rt], axis=0).astype(BF16)
    rhs = jnp.concatenate([_bd(bt, bd16), _bd(kt, bd16)], axis=0)
    a_all = lax.dot_general(lhs, rhs, (((1,), (1,)), ((), ())), preferred_element_type=F32)
    yield
    n_ab = a_all[:CHUNK, :HGW] * strict_m
    a_ak = a_all[:CHUNK, HGW:] * strict_m
    a_rb = a_all[CHUNK:, :HGW] * incl_m
    a_rk = a_all[CHUNK:, HGW:] * incl_m

    t = eye_p + n_ab
    nk = _dot(n_ab.astype(BF16), _bd(n_ab, bd16))
    yield
    for _ in range(4):
        both = _dot(jnp.concatenate([nk, t], axis=0).astype(BF16), _bd(nk, bd16))
        yield
        nk = both[:CHUNK]
        t = t + both[CHUNK:]
    tn = _dot(t.astype(BF16), _bd(nk, bd16))
    yield
    t = t + tn

    tb = t.astype(BF16)
    vbd = _bd(v, bd16)
    w = _dot(tb, _bd(at, bd16))
    akv = _dot(a_ak.astype(BF16), vbd)
    yield
    uv = _dot(tb, _bd(akv, bd16))
    arb = a_rb.astype(BF16)
    qpd = _dot(arb, _bd(w, bd16))
    yield
    y_in = _dot(jnp.concatenate([arb, a_rk.astype(BF16)], axis=1),
                jnp.concatenate([_bd(uv, bd16), vbd], axis=0))
    lhs_t = jnp.concatenate([bh, kh], axis=0).astype(BF16)
    rhs_t = jnp.concatenate([jnp.concatenate([w, uv], axis=1),
                             jnp.concatenate([jnp.zeros_like(v), v], axis=1)], axis=0).astype(BF16)
    mc = lax.dot_general(lhs_t, rhs_t, (((0,), (0,)), ((), ())), preferred_element_type=F32)
    yield
    m_bd = mc[:, :HGW] * bd32 + eye32 * jnp.exp(tot)
    c_bd = mc[:, HGW:] * bd32
    return y_in, (rt + qpd).astype(BF16), m_bd.astype(BF16), c_bd


def _interleave(gens):
    results = [None] * len(gens)
    active = list(range(len(gens)))
    while active:
        still = []
        for i in active:
            try:
                next(gens[i])
                still.append(i)
            except StopIteration as stop:
                results[i] = stop.value
        active = still
    return results


def _split_dot_left(b_exact, a):
    hi = a.astype(BF16)
    lo = (a - hi.astype(F32)).astype(BF16)
    return _dot(b_exact, hi) + _dot(b_exact, lo)


def _rwscan_kernel(rf, kf, vf, nf, af, lf, rb, kb, vb, nb, ab, lb, tri_ref, m64_ref, bd16_ref, bd32_ref,
                   yf_out, yb_out, s_ref):
    c = pl.program_id(1)

    @pl.when(c == 0)
    def _():
        s_ref[...] = jnp.zeros(s_ref.shape, F32)

    nch = rf.shape[0] // CHUNK
    bd16 = bd16_ref[...]
    bd32 = bd32_ref[0]
    eye32 = bd32_ref[1]
    eye_p = m64_ref[4]
    ngrp = RW_HEADS // HG
    chains = []
    gens = []
    for d, refs, y_out in ((0, (rf, kf, vf, nf, af, lf), yf_out), (1, (rb, kb, vb, nb, ab, lb), yb_out)):
        reverse = d == 1
        consts = (tri_ref[d], m64_ref[2 * d], m64_ref[2 * d + 1], eye_p, bd16, bd32, eye32)
        order = list(range(nch - 1, -1, -1) if reverse else range(nch))
        for g in range(ngrp):
            chains.append((d, g, y_out, order))
            for j in order:
                ops = [x[j * CHUNK:(j + 1) * CHUNK, g * HGW:(g + 1) * HGW] for x in refs]
                gens.append(_chunk_intra(*ops, reverse, consts))
    parts = _interleave(gens)

    states = [s_ref[d, g] for d, g, _, _ in chains]
    for step in range(nch):
        for ci, (d, g, y_out, order) in enumerate(chains):
            j = order[step]
            y_in, qp, m_bd, c_bd = parts[ci * nch + step]
            sb = states[ci].astype(BF16)
            y_out[j * CHUNK:(j + 1) * CHUNK, g * HGW:(g + 1) * HGW] = y_in + _dot(qp, sb)
            states[ci] = _dot(m_bd, sb) + c_bd
    for ci, (d, g, _, _) in enumerate(chains):
        s_ref[d, g] = states[ci]


def _rwscan(r, k, v, kkn, kka, lwf, lwb, bsz, seq, p):
    n = r.shape[0]
    rows = min(NCH * CHUNK, seq)
    nb = seq // rows
    fwd = pl.BlockSpec((rows, RW_WIDTH), lambda b, c: (b * nb + c, 0))
    bwd = pl.BlockSpec((rows, RW_WIDTH), lambda b, c: (b * nb + nb - 1 - c, 0))
    full = lambda a: pl.BlockSpec(a.shape, lambda b, c: (0,) * a.ndim)
    sds = jax.ShapeDtypeStruct((n, RW_WIDTH), F32)
    return pl.pallas_call(
        _rwscan_kernel,
        grid=(bsz, nb),
        in_specs=[fwd] * 6 + [bwd] * 6 + [full(p['sc_tri']), full(p['sc_m64']), full(p['sc_bd16']),
                                           full(p['sc_bd32'])],
        out_specs=[fwd, bwd],
        out_shape=[sds, sds],
        scratch_shapes=[pltpu.VMEM((2, RW_HEADS // HG, HGW, HGW), F32)],
        compiler_params=_cparams(("parallel", "arbitrary")),
        name="rwscan",
    )(r, k, v, kkn, kka, lwf, r, k, v, kkn, kka, lwb, p['sc_tri'], p['sc_m64'], p['sc_bd16'], p['sc_bd32'])


def _post_kernel(x_ref, oa_ref, yf_ref, yb_ref, bonus_ref, g_ref, lng_ref, lnb_ref, seg_ref, wout_ref,
                 g2_ref, wr_ref, br_ref, x1_out, hp_out, gid_out):
    tm = x_ref.shape[0]
    nsplit = 2 if tm % 16 == 0 else 1
    rows_per = tm // nsplit

    def rows_gen(rs):
        seg = seg_ref[...]
        oa_part = _dot(oa_ref[rs, :], wout_ref[:MLA_WIDTH, :])
        y = yf_ref[rs, :] + yb_ref[rs, :]
        mean = _split_dot(y, seg) * (1.0 / RW_HEAD)
        yield
        dlt = y - mean
        var = _split_dot(dlt * dlt, seg) * (1.0 / RW_HEAD)
        yield
        yn = dlt * lax.rsqrt(var + LN_X_EPS) * lng_ref[...] + lnb_ref[...]
        ob = (yn + bonus_ref[rs, :]) * g_ref[rs, :]
        x1 = x_ref[rs, :] + oa_part + _dot(ob.astype(BF16), wout_ref[MLA_WIDTH:, :])
        yield
        x1_out[rs, :] = x1
        h2 = x1 * lax.rsqrt(jnp.mean(x1 * x1, axis=-1, keepdims=True) + RMS_EPS) * g2_ref[...]
        h2b = h2.astype(BF16)
        bits = lax.bitcast_convert_type(h2b.astype(F32), jnp.uint32)
        hp_out[rs, :] = (bits[:, :D_MODEL // 2] >> 16) | (bits[:, D_MODEL // 2:] & jnp.uint32(0xFFFF0000))
        logits = _dot(h2b, wr_ref[...]) + br_ref[...]
        yield
        _, gidx, _ = _group_choice(logits)
        gid_out[rs, :] = jnp.broadcast_to(gidx, (rows_per, LANE))

    _interleave([rows_gen(slice(j * rows_per, (j + 1) * rows_per)) for j in range(nsplit)])


def _group_choice(logits):
    lane_i = lax.broadcasted_iota(jnp.int32, logits.shape, 1)
    lane = lane_i.astype(F32)
    is_g = (lane_i >= N_EXPERTS) & (lane_i < N_EXPERTS + N_GROUPS)
    gl = jnp.where(is_g, logits, -jnp.inf)
    gmax = jnp.max(gl, axis=-1, keepdims=True)
    gidx = jnp.min(jnp.where(gl == gmax, lane, 1e9), axis=-1, keepdims=True) - N_EXPERTS
    p_g = 1.0 / jnp.sum(jnp.where(is_g, jnp.exp(gl - gmax), 0.0), axis=-1, keepdims=True)
    return p_g, gidx, lane


def _post(x2, oa, yf, yb, bonus, g, p, tm):
    n = x2.shape[0]
    tm = min(tm, n)
    row = lambda i: (i, 0)
    fixed = lambda i: (0, 0)
    full = lambda a: pl.BlockSpec(a.shape, fixed)
    rw = pl.BlockSpec((tm, RW_WIDTH), row)
    return pl.pallas_call(
        _post_kernel,
        grid=(n // tm,),
        in_specs=[pl.BlockSpec((tm, D_MODEL), row), pl.BlockSpec((tm, MLA_WIDTH), row), rw, rw, rw, rw,
                  full(p['ln_g']), full(p['ln_b']), full(p['seg']), full(p['w_out']), full(p['g2']),
                  full(p['w_r']), full(p['b_r'])],
        out_specs=[pl.BlockSpec((tm, D_MODEL), row), pl.BlockSpec((tm, D_MODEL // 2), row),
                   pl.BlockSpec((tm, LANE), row)],
        out_shape=[jax.ShapeDtypeStruct((n, D_MODEL), F32), jax.ShapeDtypeStruct((n, D_MODEL // 2), jnp.uint32),
                   jax.ShapeDtypeStruct((n, LANE), F32)],
        compiler_params=_cparams(("parallel",)),
        name="post",
    )(x2, oa, yf, yb, bonus, g, p['ln_g'], p['ln_b'], p['seg'], p['w_out'], p['g2'], p['w_r'], p['b_r'])


def _route_plan(gid, tm, npt):
    g = gid[:, 0].astype(jnp.int32)
    onehot = (g[:, None] == jnp.arange(N_GROUPS, dtype=jnp.int32)[None, :]).astype(jnp.int32)
    csum = jnp.cumsum(onehot, axis=0)
    rank = jnp.sum(csum * onehot, axis=1) - 1
    ntile = (csum[-1] + tm - 1) // tm
    tile_end = jnp.cumsum(ntile)
    off = (tile_end - ntile) * tm
    pos = jnp.sum(onehot * off[None, :], axis=1) + rank
    n_active = tile_end[-1]
    t = jnp.minimum(jnp.arange(npt, dtype=jnp.int32), n_active - 1)
    tile_gid = jnp.sum((t[:, None] >= tile_end[None, :]).astype(jnp.int32), axis=1)
    meta = jnp.concatenate([n_active[None], tile_gid]).astype(jnp.int32)
    return pos.astype(jnp.int32), meta


def _row_dma_loops(n_rows, make_copy):
    def issue(h, carry):
        make_copy(2 * h).start(priority=0)
        make_copy(2 * h + 1).start(priority=1)
        return carry
    lax.fori_loop(0, n_rows // 2, issue, 0, unroll=4)

    def drain(r, carry):
        make_copy(0).wait()
        return carry
    lax.fori_loop(0, n_rows, drain, 0, unroll=8)


def _scatter_kernel(pos_ref, src_ref, init_ref, dst_ref, sem):
    _row_dma_loops(src_ref.shape[0], lambda r: pltpu.make_async_copy(
        src_ref.at[pl.ds(r, 1)], dst_ref.at[pl.ds(pos_ref[0, 0, r], 1)], sem))


def _scatter_rows(pos, src, n_dst, tms):
    n, w = src.shape
    tms = min(tms, n)
    return pl.pallas_call(
        _scatter_kernel,
        grid=(n // tms,),
        in_specs=[pl.BlockSpec((1, 1, tms), lambda i: (i, 0, 0), memory_space=pltpu.SMEM),
                  pl.BlockSpec((tms, w), lambda i: (i, 0)), pl.BlockSpec(memory_space=pl.ANY)],
        out_specs=pl.BlockSpec(memory_space=pl.ANY),
        out_shape=jax.ShapeDtypeStruct((n_dst, w), src.dtype),
        scratch_shapes=[pltpu.SemaphoreType.DMA(())],
        input_output_aliases={2: 0},
        compiler_params=_cparams(("arbitrary",)),
        name="moe_scatter",
    )(pos.reshape(n // tms, 1, tms), src, jnp.zeros((n_dst, w), src.dtype))


def _moe_group_kernel(meta_ref, xs_ref, wr_ref, br_ref, wg_ref, wu_ref, wd_ref, ys_ref):
    i = pl.program_id(0)

    @pl.when(i >= meta_ref[0])
    def _():
        ys_ref[...] = jnp.zeros(ys_ref.shape, F32)

    @pl.when(i < meta_ref[0])
    def _():
        tm = xs_ref.shape[0]
        word = xs_ref[...]
        h2 = jnp.concatenate([lax.bitcast_convert_type(word << 16, F32),
                              lax.bitcast_convert_type(word & jnp.uint32(0xFFFF0000), F32)],
                             axis=1).astype(BF16)
        def up(c):
            ks = range(c * MOE_EPC, (c + 1) * MOE_EPC)
            return (jnp.concatenate([_dot(h2, wg_ref[k]) for k in ks], axis=1),
                    jnp.concatenate([_dot(h2, wu_ref[k]) for k in ks], axis=1))

        pend = up(0)
        logits = _dot(h2, wr_ref[...]) + br_ref[...]
        p_g, _, lane = _group_choice(logits)
        first = (meta_ref[1 + i] * EXPERTS_PER_GROUP).astype(F32)
        in_grp = (lane >= first) & (lane < first + EXPERTS_PER_GROUP)
        el = jnp.where(in_grp, logits, -jnp.inf)
        v1 = jnp.max(el, axis=-1, keepdims=True)
        i1 = jnp.min(jnp.where(el == v1, lane, 1e9), axis=-1, keepdims=True)
        el2 = jnp.where(lane == i1, -jnp.inf, el)
        v2 = jnp.max(el2, axis=-1, keepdims=True)
        i2 = jnp.min(jnp.where(el2 == v2, lane, 1e9), axis=-1, keepdims=True)
        e2 = jnp.exp(v2 - v1)
        den = 1.0 + e2
        w1 = (1.0 / den) * p_g
        w2 = (e2 / den) * p_g

        acc = None
        nchunk = EXPERTS_PER_GROUP // MOE_EPC
        for c in range(nchunk):
            nxt = up(c + 1) if c + 1 < nchunk else None
            hg, hu = pend
            gate = jnp.concatenate(
                [jnp.broadcast_to(jnp.where(i1 == first + k, w1, 0.0) + jnp.where(i2 == first + k, w2, 0.0),
                                  (tm, D_EXPERT)) for k in range(c * MOE_EPC, (c + 1) * MOE_EPC)], axis=1)
            act = hg * jax.nn.sigmoid(hg) * hu * gate
            wd = wd_ref[c * MOE_EPC:(c + 1) * MOE_EPC].reshape(MOE_EPC * D_EXPERT, D_MODEL)
            part = _dot(act.astype(BF16), wd)
            acc = part if acc is None else acc + part
            pend = nxt
        ys_ref[...] = acc


def _moe_grouped(xs, meta, p, tm, npt):
    def tile(i, m):
        return (jnp.minimum(i, m[0] - 1), 0)

    def group(i, m):
        return (m[1 + i], 0, 0)

    grid_spec = pltpu.PrefetchScalarGridSpec(
        num_scalar_prefetch=1,
        grid=(npt,),
        in_specs=[pl.BlockSpec((tm, D_MODEL // 2), tile),
                  pl.BlockSpec(p['w_r'].shape, lambda i, m: (0, 0)),
                  pl.BlockSpec(p['b_r'].shape, lambda i, m: (0, 0)),
                  pl.BlockSpec((EXPERTS_PER_GROUP, D_MODEL, D_EXPERT), group),
                  pl.BlockSpec((EXPERTS_PER_GROUP, D_MODEL, D_EXPERT), group),
                  pl.BlockSpec((EXPERTS_PER_GROUP, D_EXPERT, D_MODEL), group)],
        out_specs=pl.BlockSpec((tm, D_MODEL), lambda i, m: (i, 0)),
    )
    return pl.pallas_call(
        _moe_group_kernel,
        grid_spec=grid_spec,
        out_shape=jax.ShapeDtypeStruct((npt * tm, D_MODEL), F32),
        compiler_params=_cparams(("arbitrary",)),
        name="moe_group",
    )(meta, xs, p['w_r'], p['b_r'], p['wg'], p['wu'], p['wd'])


def _gather_kernel(pos_ref, x1_ref, ys_ref, o_ref, buf, sem):
    _row_dma_loops(buf.shape[0], lambda r: pltpu.make_async_copy(
        ys_ref.at[pl.ds(pos_ref[0, 0, r], 1)], buf.at[pl.ds(r, 1)], sem))
    o_ref[...] = x1_ref[...] + buf[...]


def _gather_add(pos, x1, ys, tms):
    n, d = x1.shape
    tms = min(tms, n)
    return pl.pallas_call(
        _gather_kernel,
        grid=(n // tms,),
        in_specs=[pl.BlockSpec((1, 1, tms), lambda i: (i, 0, 0), memory_space=pltpu.SMEM),
                  pl.BlockSpec((tms, d), lambda i: (i, 0)),
                  pl.BlockSpec(memory_space=pl.ANY)],
        out_specs=pl.BlockSpec((tms, d), lambda i: (i, 0)),
        out_shape=jax.ShapeDtypeStruct((n, d), F32),
        scratch_shapes=[pltpu.VMEM((tms, d), F32), pltpu.SemaphoreType.DMA(())],
        compiler_params=_cparams(("arbitrary",)),
        name="moe_gather",
    )(pos.reshape(n // tms, 1, tms), x1, ys)


def _head_pad_cols(w, head_w, lo, hi, dst, width=HEAD_PAD, perm=None):
    rows = w.shape[0]
    wh = w.reshape(rows, -1, head_w)[:, :, lo:hi]
    if perm is not None:
        wh = wh[:, :, perm]
    out = jnp.zeros((rows, wh.shape[1], width), w.dtype)
    return out.at[:, :, dst:dst + (hi - lo)].set(wh).reshape(rows, -1)


def _pack_params(lp, max_seq):
    (norm1_g, w_in, q_a_norm_g, w_uq, kv_a_norm_g, w_ukv, q_norm_g, k_norm_g, mu_shift, w0_f, w_up_f,
     w0_b, w_up_b, a0, a_up, g_up, k_k, k_a, r_k, ln_x_g, ln_x_b, w_out, norm2_g, w_router_group,
     b_router_group, w_router_expert, b_router_expert, w_expert_gate, w_expert_up, w_expert_down) = lp
    half = QK_ROPE // 2
    swap = jnp.concatenate([jnp.arange(half, QK_ROPE), jnp.arange(half)])
    zcol = lambda rows, n: jnp.zeros((rows, n), F32)
    p = {}
    c0 = Q_LORA + KV_LORA
    w_kr = w_in[:, c0:c0 + QK_ROPE]
    u0 = c0 + QK_ROPE
    wu_ = w_in[:, u0:]
    d = D_MODEL
    w_in_p = jnp.concatenate([
        w_in[:, :c0],
        zcol(d, QK_NOPE), w_kr, zcol(d, LANE - QK_HEAD),
        zcol(d, QK_NOPE), w_kr[:, swap], zcol(d, LANE - QK_HEAD),
        wu_[:, :3 * RW_WIDTH + 2 * DECAY_LORA + A_LORA], zcol(d, LANE - A_LORA),
        wu_[:, 3 * RW_WIDTH + 2 * DECAY_LORA + A_LORA:]], axis=1)
    p['w_in'] = w_in_p.astype(BF16)
    mu = mu_shift[None, :]
    p['mu'] = jnp.concatenate([mu[:, :3 * RW_WIDTH + 2 * DECAY_LORA + A_LORA], zcol(1, LANE - A_LORA),
                               mu[:, 3 * RW_WIDTH + 2 * DECAY_LORA + A_LORA:]], axis=1)
    p['g1'] = norm1_g[None, :]
    p['qag'] = q_a_norm_g[None, :]
    p['kvag'] = kv_a_norm_g[None, :]
    p['w_uq'] = jnp.concatenate([
        _head_pad_cols(w_uq, QK_HEAD, 0, QK_HEAD, 0),
        _head_pad_cols(w_uq, QK_HEAD, QK_NOPE, QK_HEAD, QK_NOPE, perm=swap)], axis=1).astype(BF16)
    p['w_ukv'] = jnp.concatenate([
        _head_pad_cols(w_ukv, QK_NOPE + V_HEAD, 0, QK_NOPE, 0),
        _head_pad_cols(w_ukv, QK_NOPE + V_HEAD, QK_NOPE, QK_NOPE + V_HEAD, 0)], axis=1).astype(BF16)

    def gain_rows(g):
        main = jnp.concatenate([g, jnp.zeros((LANE - QK_HEAD,), F32)])
        swp = jnp.concatenate([jnp.zeros((QK_NOPE,), F32), g[QK_NOPE:][swap], jnp.zeros((LANE - QK_HEAD,), F32)])
        return jnp.stack([main, swp])
    kbound = math.sqrt(QK_HEAD) * jnp.max(jnp.abs(k_norm_g))
    qbound = jnp.max(jnp.abs(q_norm_g)) * LOG2E
    p['gq'] = jnp.concatenate([gain_rows(q_norm_g), jnp.full((1, LANE), qbound * kbound, F32)], axis=0)
    p['gk'] = gain_rows(k_norm_g)
    p['fast_softmax'] = 2.0 * qbound * kbound <= FAST_SOFTMAX_MAX_SHIFT
    inv_freq = 1.0 / (ROPE_THETA ** (jnp.arange(half, dtype=F32) / half))
    ang = jnp.arange(max_seq, dtype=F32)[:, None] * inv_freq[None, :]
    cos, sin = jnp.cos(ang), jnp.sin(ang)
    zpad = jnp.zeros((max_seq, LANE - QK_HEAD), F32)
    p['ct'] = jnp.concatenate([jnp.ones((max_seq, QK_NOPE), F32), cos, cos, zpad], axis=1)
    p['st'] = jnp.concatenate([jnp.zeros((max_seq, QK_NOPE), F32), -sin, sin, zpad], axis=1)

    zl = jnp.zeros((DECAY_LORA, RW_WIDTH), F32)
    p['w_dec'] = jnp.concatenate([jnp.concatenate([w_up_f, zl], axis=1),
                                  jnp.concatenate([zl, w_up_b], axis=1)], axis=0).astype(BF16)
    p['w0'] = jnp.concatenate([w0_f, w0_b])[None, :]
    p['a_up'] = jnp.concatenate([a_up, jnp.zeros((LANE - A_LORA, RW_WIDTH), F32)], axis=0).astype(BF16)
    p['a0'] = a0[None, :]
    p['g_up'] = g_up.astype(BF16)
    p['k_k'] = k_k[None, :]
    p['k_a'] = k_a[None, :]
    p['r_k'] = r_k.reshape(1, RW_WIDTH)
    hid = jnp.arange(RW_WIDTH) // RW_HEAD
    p['seg'] = (hid[:, None] == hid[None, :]).astype(BF16)
    ti = jnp.arange(CHUNK)[:, None]
    tj = jnp.arange(CHUNK)[None, :]
    p['sc_tri'] = jnp.stack([tj <= ti, tj >= ti]).astype(BF16)
    ps = (jnp.arange(HGW) % CHUNK)[None, :]
    p['sc_m64'] = jnp.stack([ps < ti, ps <= ti, ps > ti, ps >= ti, ps == ti]).astype(F32)
    blk = jnp.arange(HGW) // RW_HEAD
    same = blk[:, None] == blk[None, :]
    p['sc_bd16'] = same.astype(BF16)
    p['sc_bd32'] = jnp.stack([same, jnp.eye(HGW, dtype=bool)]).astype(F32)
    p['ln_g'] = ln_x_g[None, :]
    p['ln_b'] = ln_x_b[None, :]
    p['w_out'] = w_out.astype(BF16)
    p['g2'] = norm2_g[None, :]
    p['w_r'] = jnp.concatenate([w_router_expert, w_router_group,
                                zcol(d, LANE - N_EXPERTS - N_GROUPS)], axis=1).astype(BF16)
    p['b_r'] = jnp.concatenate([b_router_expert, b_router_group,
                                jnp.zeros((LANE - N_EXPERTS - N_GROUPS,), F32)])[None, :]
    p['wg'] = w_expert_gate.astype(BF16)
    p['wu'] = w_expert_up.astype(BF16)
    p['wd'] = w_expert_down.astype(BF16)
    return p


TM_PROJ = 512
PROJ_SPLIT = 2
TQ = 2048
TK = 8192
TKS = 1024
TQ_MAX = 1024
TK_MAX = 2048
TKS_MAX = 1024
TM_POST = 512
TM_MOE = 512
TM_ROWS = 512


def _layer(x, p):
    bsz, seq, d = x.shape
    n = bsz * seq
    x2 = x.reshape(n, d)
    q, k, v, r, k2, vv, kkn, kka, lwf, lwb, g, bonus = _proj(x2, seq, p, TM_PROJ)
    oa = _attn(q, k, v, bsz, seq, p['fast_softmax'])
    yf, yb = _rwscan(r, k2, vv, kkn, kka, lwf, lwb, bsz, seq, p)
    x1, hp, gid = _post(x2, oa, yf, yb, bonus, g, p, TM_POST)
    tm = min(TM_MOE, n)
    npt = n // tm + N_GROUPS
    pos, meta = _route_plan(gid, tm, npt)
    xs = _scatter_rows(pos, hp, npt * tm, TM_ROWS)
    ys = _moe_grouped(xs, meta, p, tm, npt)
    out = _gather_add(pos, x1, ys, TM_ROWS)
    return out.reshape(bsz, seq, d)


def kernel(x_prompt, x_sample, norm1_g, w_in, q_a_norm_g, w_uq, kv_a_norm_g, w_ukv, q_norm_g, k_norm_g, mu_shift, w0_f, w_up_f, w0_b, w_up_b, a0, a_up, g_up, k_k, k_a, r_k, ln_x_g, ln_x_b, w_out, norm2_g, w_router_group, b_router_group, w_router_expert, b_router_expert, w_expert_gate, w_expert_up, w_expert_down):
    layer_params = (norm1_g, w_in, q_a_norm_g, w_uq, kv_a_norm_g, w_ukv, q_norm_g, k_norm_g, mu_shift,
                    w0_f, w_up_f, w0_b, w_up_b, a0, a_up, g_up, k_k, k_a, r_k, ln_x_g, ln_x_b, w_out,
                    norm2_g, w_router_group, b_router_group, w_router_expert, b_router_expert,
                    w_expert_gate, w_expert_up, w_expert_down)
    y_prompt, y_sample = x_prompt, x_sample
    max_seq = max(x_prompt.shape[1], x_sample.shape[1])
    for layer in range(norm1_g.shape[0]):
        p = _pack_params([w[layer] for w in layer_params], max_seq)
        y_prompt = _layer(y_prompt, p)
        y_sample = _layer(y_sample, p)
    return (y_prompt, y_sample)
```

```python
import functools
import math

import jax
import jax.numpy as jnp
from jax import lax
from jax.experimental import pallas as pl
from jax.experimental.pallas import tpu as pltpu

D_MODEL = 1024
MLA_HEADS = 8
QK_NOPE = 64
QK_ROPE = 32
QK_HEAD = QK_NOPE + QK_ROPE
V_HEAD = 64
Q_LORA = 384
KV_LORA = 256
ROPE_THETA = 10000.0
RW_HEADS = 8
RW_HEAD = 64
RW_WIDTH = RW_HEADS * RW_HEAD
DECAY_LORA = 64
A_LORA = 64
GATE_LORA = 128
LN_X_EPS = 6.4e-4
MLA_WIDTH = MLA_HEADS * V_HEAD
N_GROUPS = 4
EXPERTS_PER_GROUP = 8
N_EXPERTS = N_GROUPS * EXPERTS_PER_GROUP
D_EXPERT = 256
MOE_EPC = 2
RMS_EPS = 1e-6

LANE = 128
HEAD_PAD = LANE
MLA_PAD = MLA_HEADS * HEAD_PAD
RW_COLS_PAD = 3 * RW_WIDTH + 3 * LANE
MLA_COLS_PAD = Q_LORA + KV_LORA + 2 * LANE
D_IN_PAD = MLA_COLS_PAD + RW_COLS_PAD
CHUNK = 64
HG = 4
HGW = HG * RW_HEAD
VMEM_LIMIT = 56 * 1024 * 1024
LOG2E = 1.4426950408889634
NCH = 8
FAST_SOFTMAX_MAX_SHIFT = 100.0

F32 = jnp.float32
BF16 = jnp.bfloat16


def _dot(a, b):
    return jnp.dot(a, b, preferred_element_type=F32)


def _split_dot(a, b_exact):
    hi = a.astype(BF16)
    lo = (a - hi.astype(F32)).astype(BF16)
    return _dot(hi, b_exact) + _dot(lo, b_exact)


def _cparams(sem):
    return pltpu.CompilerParams(dimension_semantics=sem, vmem_limit_bytes=VMEM_LIMIT)


def _proj_kernel(x_ref, xp_ref, xn_ref, g1_ref, win_ref, qag_ref, wuq_ref, kvag_ref, wukv_ref, gq_ref, gk_ref,
                 ct_ref, st_ref, *rw_refs, tm, seq):
    q_out, k_out, v_out = rw_refs[10:13]
    i = pl.program_id(0)
    x = jnp.concatenate([xp_ref[0], x_ref[...], xn_ref[0]], axis=0)
    h = x * lax.rsqrt(jnp.mean(x * x, axis=-1, keepdims=True) + RMS_EPS) * g1_ref[...]
    ze = _dot(h.astype(BF16), win_ref[...])
    first = (i * tm) % seq == 0
    last = ((i + 1) * tm) % seq == 0

    gq = gq_ref[...]
    gk = gk_ref[...]
    scale = QK_HEAD ** -0.5 * LOG2E
    lane = lax.broadcasted_iota(jnp.int32, (1, LANE), 1)
    ones_col = jnp.where(lane == V_HEAD, 1.0, 0.0).astype(F32)
    shift_col = jnp.where(lane == QK_HEAD, 1.0, 0.0).astype(F32)
    q_shift = shift_col * gq[2:3, 0:1]

    def rows_gen(a, b):
        rs = slice(a, b)
        z = ze[8 + a:8 + b]
        prev_row = ze[7 + a:8 + a, MLA_COLS_PAD:]
        next_row = ze[8 + b:9 + b, MLA_COLS_PAD:]
        if a == 0:
            prev_row = jnp.where(first, 0.0, prev_row)
        if b == tm:
            next_row = jnp.where(last, 0.0, next_row)
        yield from _rw_token_maps(z[:, MLA_COLS_PAD:], prev_row, next_row, rs, *rw_refs[:10], *rw_refs[13:])

        cq = z[:, :Q_LORA]
        cqn = cq * lax.rsqrt(jnp.mean(cq * cq, axis=-1, keepdims=True) + RMS_EPS) * qag_ref[...]
        q2 = _dot(cqn.astype(BF16), wuq_ref[...])
        ckv = z[:, Q_LORA:Q_LORA + KV_LORA]
        ckvn = ckv * lax.rsqrt(jnp.mean(ckv * ckv, axis=-1, keepdims=True) + RMS_EPS) * kvag_ref[...]
        kv2 = _dot(ckvn.astype(BF16), wukv_ref[...])
        yield
        kr_main = z[:, Q_LORA + KV_LORA:Q_LORA + KV_LORA + LANE]
        kr_swap = z[:, Q_LORA + KV_LORA + LANE:MLA_COLS_PAD]
        ct = ct_ref[rs, :]
        st = st_ref[rs, :]
        kr_ssq = jnp.sum(kr_main * kr_main, axis=-1, keepdims=True)
        q_main_f = gq[0:1] * ct
        q_swap_f = gq[1:2] * st
        k_rot = kr_main * (gk[0:1] * ct) + kr_swap * (gk[1:2] * st)
        for hd in range(MLA_HEADS):
            sl = slice(hd * HEAD_PAD, (hd + 1) * HEAD_PAD)
            qm = q2[:, sl]
            qs = q2[:, MLA_PAD + hd * HEAD_PAD:MLA_PAD + (hd + 1) * HEAD_PAD]
            rinv = lax.rsqrt(jnp.sum(qm * qm, axis=-1, keepdims=True) * (1.0 / QK_HEAD) + RMS_EPS)
            qh = (qm * q_main_f + qs * q_swap_f) * (rinv * scale)
            q_out[rs, sl] = (qh - q_shift).astype(BF16)
            kn = kv2[:, sl]
            rinv_k = lax.rsqrt((jnp.sum(kn * kn, axis=-1, keepdims=True) + kr_ssq) * (1.0 / QK_HEAD) + RMS_EPS)
            kh = (kn * gk[0:1] + k_rot) * rinv_k
            k_out[rs, sl] = (kh + shift_col).astype(BF16)
            v_out[rs, sl] = (kv2[:, MLA_PAD + hd * HEAD_PAD:MLA_PAD + (hd + 1) * HEAD_PAD] + ones_col).astype(BF16)

    nsplit = PROJ_SPLIT if tm % (8 * PROJ_SPLIT) == 0 else 1
    rows_per = tm // nsplit
    _interleave([rows_gen(j * rows_per, (j + 1) * rows_per) for j in range(nsplit)])


def _proj(x2, seq, p, tm):
    n = x2.shape[0]
    tm = min(tm, seq)
    nseq_t = seq // tm
    row = lambda i: (i, 0)
    fixed = lambda i: (0, 0)
    pos = lambda i: (i % nseq_t, 0)
    full = lambda a: pl.BlockSpec(a.shape, fixed)
    g8 = tm // 8
    ngrp = n // 8
    x3 = x2.reshape(ngrp, 8, D_MODEL)
    rw_names = ('mu', 'w_dec', 'w0', 'a_up', 'a0', 'g_up', 'k_k', 'k_a', 'r_k', 'seg')
    mla_sds = jax.ShapeDtypeStruct((n, MLA_PAD), BF16)
    rw_sds = jax.ShapeDtypeStruct((n, RW_WIDTH), F32)
    return pl.pallas_call(
        functools.partial(_proj_kernel, tm=tm, seq=seq),
        grid=(n // tm,),
        in_specs=[pl.BlockSpec((tm, D_MODEL), row),
                  pl.BlockSpec((1, 8, D_MODEL), lambda i: (jnp.maximum(i * g8 - 1, 0), 0, 0)),
                  pl.BlockSpec((1, 8, D_MODEL), lambda i: (jnp.minimum((i + 1) * g8, ngrp - 1), 0, 0)),
                  full(p['g1']), full(p['w_in']), full(p['qag']),
                  full(p['w_uq']), full(p['kvag']), full(p['w_ukv']), full(p['gq']), full(p['gk']),
                  pl.BlockSpec((tm, LANE), pos), pl.BlockSpec((tm, LANE), pos)]
                 + [full(p[k]) for k in rw_names],
        out_specs=[pl.BlockSpec((tm, MLA_PAD), row)] * 3 + [pl.BlockSpec((tm, RW_WIDTH), row)] * 9,
        out_shape=[mla_sds] * 3 + [rw_sds] * 9,
        compiler_params=_cparams(("parallel",)),
        name="proj",
    )(x2, x3, x3, p['g1'], p['w_in'], p['qag'], p['w_uq'], p['kvag'], p['w_ukv'], p['gq'], p['gk'],
      p['ct'][:seq], p['st'][:seq], *[p[k] for k in rw_names])


def _attn_kernel(q_ref, k_ref, v_ref, o_ref, m_ref, acc_ref, *, running_max, tks):
    j = pl.program_id(3)

    @pl.when(j == 0)
    def _():
        if running_max:
            m_ref[...] = jnp.full(m_ref.shape, -jnp.inf, F32)
        acc_ref[...] = jnp.zeros(acc_ref.shape, F32)

    heads = [slice(hh * HEAD_PAD, (hh + 1) * HEAD_PAD) for hh in range(2)]
    nsub = k_ref.shape[0] // tks

    def score_pair(sb):
        rows = slice(sb * tks, (sb + 1) * tks)
        return [lax.dot_general(k_ref[rows, sl], q_ref[:, sl], (((1,), (1,)), ((), ())),
                                preferred_element_type=F32) for sl in heads]

    accs = [acc_ref[hh] for hh in range(2)]
    maxes = [m_ref[hh] for hh in range(2)] if running_max else None
    scores = score_pair(0)
    for sb in range(nsub):
        nxt = score_pair(sb + 1) if sb + 1 < nsub else None
        rows = slice(sb * tks, (sb + 1) * tks)
        for hh, sl in enumerate(heads):
            s = scores[hh]
            if running_max:
                m_new = jnp.maximum(maxes[hh], jnp.max(s, axis=0, keepdims=True))
                p = jnp.exp2(s - m_new)
            else:
                p = jnp.exp2(s)
            pv = lax.dot_general(v_ref[rows, sl], p.astype(BF16), (((0,), (0,)), ((), ())),
                                 preferred_element_type=F32)
            if running_max:
                accs[hh] = jnp.exp2(maxes[hh] - m_new) * accs[hh] + pv
                maxes[hh] = m_new
            else:
                accs[hh] = accs[hh] + pv
        scores = nxt
    for hh in range(2):
        acc_ref[hh] = accs[hh]
        if running_max:
            m_ref[hh] = maxes[hh]

    @pl.when(j == pl.num_programs(3) - 1)
    def _():
        outs = []
        for hh in range(2):
            a = acc_ref[hh]
            outs.append(a[:V_HEAD] / a[V_HEAD:V_HEAD + 1])
        o_ref[...] = jnp.concatenate(outs, axis=0).T.astype(BF16)


def _attn(q, k, v, bsz, seq, fast_softmax):
    return lax.cond(fast_softmax,
                    lambda q, k, v: _attn_call(q, k, v, bsz, seq, TQ, TK, TKS, False),
                    lambda q, k, v: _attn_call(q, k, v, bsz, seq, TQ_MAX, TK_MAX, TKS_MAX, True),
                    q, k, v)


def _attn_call(q, k, v, bsz, seq, tq, tk, tks, running_max):
    tq = min(tq, seq)
    tk = min(tk, seq)
    tks = min(tks, tk)
    nq, nk = seq // tq, seq // tk
    n = q.shape[0]
    return pl.pallas_call(
        functools.partial(_attn_kernel, running_max=running_max, tks=tks),
        grid=(bsz, MLA_HEADS // 2, nq, nk),
        in_specs=[pl.BlockSpec((tq, 2 * HEAD_PAD), lambda b, h, i, j: (b * nq + i, h)),
                  pl.BlockSpec((tk, 2 * HEAD_PAD), lambda b, h, i, j: (b * nk + j, h)),
                  pl.BlockSpec((tk, 2 * HEAD_PAD), lambda b, h, i, j: (b * nk + j, h))],
        out_specs=pl.BlockSpec((tq, 2 * V_HEAD), lambda b, h, i, j: (b * nq + i, h)),
        out_shape=jax.ShapeDtypeStruct((n, MLA_WIDTH), BF16),
        scratch_shapes=[pltpu.VMEM((2, 1, tq), F32), pltpu.VMEM((2, HEAD_PAD, tq), F32)],
        compiler_params=_cparams(("parallel", "parallel", "parallel", "arbitrary")),
        name="attn_max" if running_max else "attn",
    )(q, k, v)


def _rw_token_maps(u, prev_row, next_row, rs, mu_ref, wdec_ref, w0_ref, aup_ref, a0_ref, gup_ref,
                   kk_ref, ka_ref, rk_ref, seg_ref,
                   r_out, k_out, v_out, kkn_out, kka_out, lwf_out, lwb_out, g_out, bonus_out):
    tm = u.shape[0]
    sub = lax.broadcasted_iota(jnp.int32, (8, 1), 0)
    prev = pltpu.roll(u, 1, 0)
    prev = jnp.concatenate([jnp.where(sub == 0, prev_row, prev[:8]), prev[8:]], axis=0)
    nxt = pltpu.roll(u, tm - 1, 0)
    nxt = jnp.concatenate([nxt[:tm - 8], jnp.where(sub == 7, next_row, nxt[tm - 8:])], axis=0)
    us = u + (0.5 * (prev + nxt) - u) * mu_ref[...]

    r = us[:, :RW_WIDTH]
    k = us[:, RW_WIDTH:2 * RW_WIDTH]
    v = us[:, 2 * RW_WIDTH:3 * RW_WIDTH]
    xw = us[:, 3 * RW_WIDTH:3 * RW_WIDTH + LANE]
    xa = us[:, 3 * RW_WIDTH + LANE:3 * RW_WIDTH + 2 * LANE]
    xg = us[:, 3 * RW_WIDTH + 2 * LANE:]
    r_out[rs, :] = r
    v_out[rs, :] = v

    wdot = _dot(jnp.tanh(xw).astype(BF16), wdec_ref[...])
    adot = _dot(xa.astype(BF16), aup_ref[...])
    gdot = _dot(jax.nn.sigmoid(xg).astype(BF16), gup_ref[...])
    seg = seg_ref[...]
    kk = k * kk_ref[...]
    kss = _split_dot(kk * kk, seg)
    yield
    lw = jax.nn.sigmoid(w0_ref[...] + wdot) * (-math.exp(-0.5))
    lwf_out[rs, :] = lw[:, :RW_WIDTH]
    lwb_out[rs, :] = lw[:, RW_WIDTH:]
    a = jax.nn.sigmoid(a0_ref[...] + adot)
    g_out[rs, :] = gdot
    kkn = kk * lax.rsqrt(jnp.maximum(kss, 1e-24))
    k2 = k * (1.0 + (a - 1.0) * ka_ref[...])
    k_out[rs, :] = k2
    kkn_out[rs, :] = kkn
    kka_out[rs, :] = kkn * a
    bsum = _split_dot(r * k2 * rk_ref[...], seg)
    yield
    bonus_out[rs, :] = bsum * v


def _bd(x, bd16):
    return jnp.concatenate([x.astype(BF16)] * HG, axis=0) * bd16


def _chunk_intra(r, k, v, kkn, kka, lw, reverse, consts):
    tri, strict_m, incl_m, eye_p, bd16, bd32, eye32 = consts
    lg = _split_dot_left(tri, lw)
    yield
    lgx = lg - lw
    tot = lg[0:1] if reverse else lg[CHUNK - 1:CHUNK]
    gi = jnp.exp(lg)
    ginv = jnp.exp(-lg)
    gend = jnp.exp(tot - lg)
    at = -kkn * jnp.exp(lgx)
    rt = r * gi
    bt = kka * ginv
    kt = k * ginv
    bh = kka * gend
    kh = k * gend

    lhs = jnp.concatenate([at, rt], axis=0).astype(BF16)
    rhs = jnp.concatenate([_bd(bt, bd16), _bd(kt, bd16)], axis=0)
    a_all = lax.dot_general(lhs, rhs, (((1,), (1,)), ((), ())), preferred_element_type=F32)
    yield
    n_ab = a_all[:CHUNK, :HGW] * strict_m
    a_ak = a_all[:CHUNK, HGW:] * strict_m
    a_rb = a_all[CHUNK:, :HGW] * incl_m
    a_rk = a_all[CHUNK:, HGW:] * incl_m

    t = eye_p + n_ab
    nk = _dot(n_ab.astype(BF16), _bd(n_ab, bd16))
    yield
    for _ in range(4):
        both = _dot(jnp.concatenate([nk, t], axis=0).astype(BF16), _bd(nk, bd16))
        yield
        nk = both[:CHUNK]
        t = t + both[CHUNK:]
    tn = _dot(t.astype(BF16), _bd(nk, bd16))
    yield
    t = t + tn

    tb = t.astype(BF16)
    vbd = _bd(v, bd16)
    w = _dot(tb, _bd(at, bd16))
    akv = _dot(a_ak.astype(BF16), vbd)
    yield
    uv = _dot(tb, _bd(akv, bd16))
    arb = a_rb.astype(BF16)
    qpd = _dot(arb, _bd(w, bd16))
    yield
    y_in = _dot(jnp.concatenate([arb, a_rk.astype(BF16)], axis=1),
                jnp.concatenate([_bd(uv, bd16), vbd], axis=0))
    lhs_t = jnp.concatenate([bh, kh], axis=0).astype(BF16)
    rhs_t = jnp.concatenate([jnp.concatenate([w, uv], axis=1),
                             jnp.concatenate([jnp.zeros_like(v), v], axis=1)], axis=0).astype(BF16)
    mc = lax.dot_general(lhs_t, rhs_t, (((0,), (0,)), ((), ())), preferred_element_type=F32)
    yield
    m_bd = mc[:, :HGW] * bd32 + eye32 * jnp.exp(tot)
    c_bd = mc[:, HGW:] * bd32
    return y_in, (rt + qpd).astype(BF16), m_bd.astype(BF16), c_bd


def _interleave(gens):
    results = [None] * len(gens)
    active = list(range(len(gens)))
    while active:
        still = []
        for i in active:
            try:
                next(gens[i])
                still.append(i)
            except StopIteration as stop:
                results[i] = stop.value
        active = still
    return results


def _split_dot_left(b_exact, a):
    hi = a.astype(BF16)
    lo = (a - hi.astype(F32)).astype(BF16)
    return _dot(b_exact, hi) + _dot(b_exact, lo)


def _rwscan_kernel(rf, kf, vf, nf, af, lf, rb, kb, vb, nb, ab, lb, tri_ref, m64_ref, bd16_ref, bd32_ref,
                   yf_out, yb_out, s_ref):
    c = pl.program_id(1)

    @pl.when(c == 0)
    def _():
        s_ref[...] = jnp.zeros(s_ref.shape, F32)

    nch = rf.shape[0] // CHUNK
    bd16 = bd16_ref[...]
    bd32 = bd32_ref[0]
    eye32 = bd32_ref[1]
    eye_p = m64_ref[4]
    ngrp = RW_HEADS // HG
    chains = []
    gens = []
    for d, refs, y_out in ((0, (rf, kf, vf, nf, af, lf), yf_out), (1, (rb, kb, vb, nb, ab, lb), yb_out)):
        reverse = d == 1
        consts = (tri_ref[d], m64_ref[2 * d], m64_ref[2 * d + 1], eye_p, bd16, bd32, eye32)
        order = list(range(nch - 1, -1, -1) if reverse else range(nch))
        for g in range(ngrp):
            chains.append((d, g, y_out, order))
            for j in order:
                ops = [x[j * CHUNK:(j + 1) * CHUNK, g * HGW:(g + 1) * HGW] for x in refs]
                gens.append(_chunk_intra(*ops, reverse, consts))
    parts = _interleave(gens)

    states = [s_ref[d, g] for d, g, _, _ in chains]
    for step in range(nch):
        for ci, (d, g, y_out, order) in enumerate(chains):
            j = order[step]
            y_in, qp, m_bd, c_bd = parts[ci * nch + step]
            sb = states[ci].astype(BF16)
            y_out[j * CHUNK:(j + 1) * CHUNK, g * HGW:(g + 1) * HGW] = y_in + _dot(qp, sb)
            states[ci] = _dot(m_bd, sb) + c_bd
    for ci, (d, g, _, _) in enumerate(chains):
        s_ref[d, g] = states[ci]


def _rwscan(r, k, v, kkn, kka, lwf, lwb, bsz, seq, p):
    n = r.shape[0]
    rows = min(NCH * CHUNK, seq)
    nb = seq // rows
    fwd = pl.BlockSpec((rows, RW_WIDTH), lambda b, c: (b * nb + c, 0))
    bwd = pl.BlockSpec((rows, RW_WIDTH), lambda b, c: (b * nb + nb - 1 - c, 0))
    full = lambda a: pl.BlockSpec(a.shape, lambda b, c: (0,) * a.ndim)
    sds = jax.ShapeDtypeStruct((n, RW_WIDTH), F32)
    return pl.pallas_call(
        _rwscan_kernel,
        grid=(bsz, nb),
        in_specs=[fwd] * 6 + [bwd] * 6 + [full(p['sc_tri']), full(p['sc_m64']), full(p['sc_bd16']),
                                           full(p['sc_bd32'])],
        out_specs=[fwd, bwd],
        out_shape=[sds, sds],
        scratch_shapes=[pltpu.VMEM((2, RW_HEADS // HG, HGW, HGW), F32)],
        compiler_params=_cparams(("parallel", "arbitrary")),
        name="rwscan",
    )(r, k, v, kkn, kka, lwf, r, k, v, kkn, kka, lwb, p['sc_tri'], p['sc_m64'], p['sc_bd16'], p['sc_bd32'])


def _post_kernel(x_ref, oa_ref, yf_ref, yb_ref, bonus_ref, g_ref, lng_ref, lnb_ref, seg_ref, wout_ref,
                 g2_ref, wr_ref, br_ref, x1_out, hp_out, gid_out):
    tm = x_ref.shape[0]
    nsplit = 2 if tm % 16 == 0 else 1
    rows_per = tm // nsplit

    def rows_gen(rs):
        seg = seg_ref[...]
        oa_part = _dot(oa_ref[rs, :], wout_ref[:MLA_WIDTH, :])
        y = yf_ref[rs, :] + yb_ref[rs, :]
        mean = _split_dot(y, seg) * (1.0 / RW_HEAD)
        yield
        dlt = y - mean
        var = _split_dot(dlt * dlt, seg) * (1.0 / RW_HEAD)
        yield
        yn = dlt * lax.rsqrt(var + LN_X_EPS) * lng_ref[...] + lnb_ref[...]
        ob = (yn + bonus_ref[rs, :]) * g_ref[rs, :]
        x1 = x_ref[rs, :] + oa_part + _dot(ob.astype(BF16), wout_ref[MLA_WIDTH:, :])
        yield
        x1_out[rs, :] = x1
        h2 = x1 * lax.rsqrt(jnp.mean(x1 * x1, axis=-1, keepdims=True) + RMS_EPS) * g2_ref[...]
        h2b = h2.astype(BF16)
        bits = lax.bitcast_convert_type(h2b.astype(F32), jnp.uint32)
        hp_out[rs, :] = (bits[:, :D_MODEL // 2] >> 16) | (bits[:, D_MODEL // 2:] & jnp.uint32(0xFFFF0000))
        logits = _dot(h2b, wr_ref[...]) + br_ref[...]
        yield
        _, gidx, _ = _group_choice(logits)
        gid_out[rs, :] = jnp.broadcast_to(gidx, (rows_per, LANE))

    _interleave([rows_gen(slice(j * rows_per, (j + 1) * rows_per)) for j in range(nsplit)])


def _group_choice(logits):
    lane_i = lax.broadcasted_iota(jnp.int32, logits.shape, 1)
    lane = lane_i.astype(F32)
    is_g = (lane_i >= N_EXPERTS) & (lane_i < N_EXPERTS + N_GROUPS)
    gl = jnp.where(is_g, logits, -jnp.inf)
    gmax = jnp.max(gl, axis=-1, keepdims=True)
    gidx = jnp.min(jnp.where(gl == gmax, lane, 1e9), axis=-1, keepdims=True) - N_EXPERTS
    p_g = 1.0 / jnp.sum(jnp.where(is_g, jnp.exp(gl - gmax), 0.0), axis=-1, keepdims=True)
    return p_g, gidx, lane


def _post(x2, oa, yf, yb, bonus, g, p, tm):
    n = x2.shape[0]
    tm = min(tm, n)
    row = lambda i: (i, 0)
    fixed = lambda i: (0, 0)
    full = lambda a: pl.BlockSpec(a.shape, fixed)
    rw = pl.BlockSpec((tm, RW_WIDTH), row)
    return pl.pallas_call(
        _post_kernel,
        grid=(n // tm,),
        in_specs=[pl.BlockSpec((tm, D_MODEL), row), pl.BlockSpec((tm, MLA_WIDTH), row), rw, rw, rw, rw,
                  full(p['ln_g']), full(p['ln_b']), full(p['seg']), full(p['w_out']), full(p['g2']),
                  full(p['w_r']), full(p['b_r'])],
        out_specs=[pl.BlockSpec((tm, D_MODEL), row), pl.BlockSpec((tm, D_MODEL // 2), row),
                   pl.BlockSpec((tm, LANE), row)],
        out_shape=[jax.ShapeDtypeStruct((n, D_MODEL), F32), jax.ShapeDtypeStruct((n, D_MODEL // 2), jnp.uint32),
                   jax.ShapeDtypeStruct((n, LANE), F32)],
        compiler_params=_cparams(("parallel",)),
        name="post",
    )(x2, oa, yf, yb, bonus, g, p['ln_g'], p['ln_b'], p['seg'], p['w_out'], p['g2'], p['w_r'], p['b_r'])


def _route_plan(gid, tm, npt):
    g = gid[:, 0].astype(jnp.int32)
    onehot = (g[:, None] == jnp.arange(N_GROUPS, dtype=jnp.int32)[None, :]).astype(jnp.int32)
    csum = jnp.cumsum(onehot, axis=0)
    rank = jnp.sum(csum * onehot, axis=1) - 1
    ntile = (csum[-1] + tm - 1) // tm
    tile_end = jnp.cumsum(ntile)
    off = (tile_end - ntile) * tm
    pos = jnp.sum(onehot * off[None, :], axis=1) + rank
    n_active = tile_end[-1]
    t = jnp.minimum(jnp.arange(npt, dtype=jnp.int32), n_active - 1)
    tile_gid = jnp.sum((t[:, None] >= tile_end[None, :]).astype(jnp.int32), axis=1)
    meta = jnp.concatenate([n_active[None], tile_gid]).astype(jnp.int32)
    return pos.astype(jnp.int32), meta


def _row_dma_loops(n_rows, make_copy):
    def issue(h, carry):
        make_copy(2 * h).start(priority=0)
        make_copy(2 * h + 1).start(priority=1)
        return carry
    lax.fori_loop(0, n_rows // 2, issue, 0, unroll=4)

    def drain(r, carry):
        make_copy(0).wait()
        return carry
    lax.fori_loop(0, n_rows, drain, 0, unroll=8)


def _scatter_kernel(pos_ref, src_ref, init_ref, dst_ref, sem):
    _row_dma_loops(src_ref.shape[0], lambda r: pltpu.make_async_copy(
        src_ref.at[pl.ds(r, 1)], dst_ref.at[pl.ds(pos_ref[0, 0, r], 1)], sem))


def _scatter_rows(pos, src, n_dst, tms):
    n, w = src.shape
    tms = min(tms, n)
    return pl.pallas_call(
        _scatter_kernel,
        grid=(n // tms,),
        in_specs=[pl.BlockSpec((1, 1, tms), lambda i: (i, 0, 0), memory_space=pltpu.SMEM),
                  pl.BlockSpec((tms, w), lambda i: (i, 0)), pl.BlockSpec(memory_space=pl.ANY)],
        out_specs=pl.BlockSpec(memory_space=pl.ANY),
        out_shape=jax.ShapeDtypeStruct((n_dst, w), src.dtype),
        scratch_shapes=[pltpu.SemaphoreType.DMA(())],
        input_output_aliases={2: 0},
        compiler_params=_cparams(("arbitrary",)),
        name="moe_scatter",
    )(pos.reshape(n // tms, 1, tms), src, jnp.zeros((n_dst, w), src.dtype))


def _moe_group_kernel(meta_ref, xs_ref, wr_ref, br_ref, wg_ref, wu_ref, wd_ref, ys_ref):
    i = pl.program_id(0)

    @pl.when(i >= meta_ref[0])
    def _():
        ys_ref[...] = jnp.zeros(ys_ref.shape, F32)

    @pl.when(i < meta_ref[0])
    def _():
        tm = xs_ref.shape[0]
        word = xs_ref[...]
        h2 = jnp.concatenate([lax.bitcast_convert_type(word << 16, F32),
                              lax.bitcast_convert_type(word & jnp.uint32(0xFFFF0000), F32)],
                             axis=1).astype(BF16)
        def up(c):
            ks = range(c * MOE_EPC, (c + 1) * MOE_EPC)
            return (jnp.concatenate([_dot(h2, wg_ref[k]) for k in ks], axis=1),
                    jnp.concatenate([_dot(h2, wu_ref[k]) for k in ks], axis=1))

        pend = up(0)
        logits = _dot(h2, wr_ref[...]) + br_ref[...]
        p_g, _, lane = _group_choice(logits)
        first = (meta_ref[1 + i] * EXPERTS_PER_GROUP).astype(F32)
        in_grp = (lane >= first) & (lane < first + EXPERTS_PER_GROUP)
        el = jnp.where(in_grp, logits, -jnp.inf)
        v1 = jnp.max(el, axis=-1, keepdims=True)
        i1 = jnp.min(jnp.where(el == v1, lane, 1e9), axis=-1, keepdims=True)
        el2 = jnp.where(lane == i1, -jnp.inf, el)
        v2 = jnp.max(el2, axis=-1, keepdims=True)
        i2 = jnp.min(jnp.where(el2 == v2, lane, 1e9), axis=-1, keepdims=True)
        e2 = jnp.exp(v2 - v1)
        den = 1.0 + e2
        w1 = (1.0 / den) * p_g
        w2 = (e2 / den) * p_g

        acc = None
        nchunk = EXPERTS_PER_GROUP // MOE_EPC
        for c in range(nchunk):
            nxt = up(c + 1) if c + 1 < nchunk else None
            hg, hu = pend
            gate = jnp.concatenate(
                [jnp.broadcast_to(jnp.where(i1 == first + k, w1, 0.0) + jnp.where(i2 == first + k, w2, 0.0),
                                  (tm, D_EXPERT)) for k in range(c * MOE_EPC, (c + 1) * MOE_EPC)], axis=1)
            act = hg * jax.nn.sigmoid(hg) * hu * gate
            wd = wd_ref[c * MOE_EPC:(c + 1) * MOE_EPC].reshape(MOE_EPC * D_EXPERT, D_MODEL)
            part = _dot(act.astype(BF16), wd)
            acc = part if acc is None else acc + part
            pend = nxt
        ys_ref[...] = acc


def _moe_grouped(xs, meta, p, tm, npt):
    def tile(i, m):
        return (jnp.minimum(i, m[0] - 1), 0)

    def group(i, m):
        return (m[1 + i], 0, 0)

    grid_spec = pltpu.PrefetchScalarGridSpec(
        num_scalar_prefetch=1,
        grid=(npt,),
        in_specs=[pl.BlockSpec((tm, D_MODEL // 2), tile),
                  pl.BlockSpec(p['w_r'].shape, lambda i, m: (0, 0)),
                  pl.BlockSpec(p['b_r'].shape, lambda i, m: (0, 0)),
                  pl.BlockSpec((EXPERTS_PER_GROUP, D_MODEL, D_EXPERT), group),
                  pl.BlockSpec((EXPERTS_PER_GROUP, D_MODEL, D_EXPERT), group),
                  pl.BlockSpec((EXPERTS_PER_GROUP, D_EXPERT, D_MODEL), group)],
        out_specs=pl.BlockSpec((tm, D_MODEL), lambda i, m: (i, 0)),
    )
    return pl.pallas_call(
        _moe_group_kernel,
        grid_spec=grid_spec,
        out_shape=jax.ShapeDtypeStruct((npt * tm, D_MODEL), F32),
        compiler_params=_cparams(("arbitrary",)),
        name="moe_group",
    )(meta, xs, p['w_r'], p['b_r'], p['wg'], p['wu'], p['wd'])


def _gather_kernel(pos_ref, x1_ref, ys_ref, o_ref, buf, sem):
    _row_dma_loops(buf.shape[0], lambda r: pltpu.make_async_copy(
        ys_ref.at[pl.ds(pos_ref[0, 0, r], 1)], buf.at[pl.ds(r, 1)], sem))
    o_ref[...] = x1_ref[...] + buf[...]


def _gather_add(pos, x1, ys, tms):
    n, d = x1.shape
    tms = min(tms, n)
    return pl.pallas_call(
        _gather_kernel,
        grid=(n // tms,),
        in_specs=[pl.BlockSpec((1, 1, tms), lambda i: (i, 0, 0), memory_space=pltpu.SMEM),
                  pl.BlockSpec((tms, d), lambda i: (i, 0)),
                  pl.BlockSpec(memory_space=pl.ANY)],
        out_specs=pl.BlockSpec((tms, d), lambda i: (i, 0)),
        out_shape=jax.ShapeDtypeStruct((n, d), F32),
        scratch_shapes=[pltpu.VMEM((tms, d), F32), pltpu.SemaphoreType.DMA(())],
        compiler_params=_cparams(("arbitrary",)),
        name="moe_gather",
    )(pos.reshape(n // tms, 1, tms), x1, ys)


def _head_pad_cols(w, head_w, lo, hi, dst, width=HEAD_PAD, perm=None):
    rows = w.shape[0]
    wh = w.reshape(rows, -1, head_w)[:, :, lo:hi]
    if perm is not None:
        wh = wh[:, :, perm]
    out = jnp.zeros((rows, wh.shape[1], width), w.dtype)
    return out.at[:, :, dst:dst + (hi - lo)].set(wh).reshape(rows, -1)


def _pack_params(lp, max_seq):
    (norm1_g, w_in, q_a_norm_g, w_uq, kv_a_norm_g, w_ukv, q_norm_g, k_norm_g, mu_shift, w0_f, w_up_f,
     w0_b, w_up_b, a0, a_up, g_up, k_k, k_a, r_k, ln_x_g, ln_x_b, w_out, norm2_g, w_router_group,
     b_router_group, w_router_expert, b_router_expert, w_expert_gate, w_expert_up, w_expert_down) = lp
    half = QK_ROPE // 2
    swap = jnp.concatenate([jnp.arange(half, QK_ROPE), jnp.arange(half)])
    zcol = lambda rows, n: jnp.zeros((rows, n), F32)
    p = {}
    c0 = Q_LORA + KV_LORA
    w_kr = w_in[:, c0:c0 + QK_ROPE]
    u0 = c0 + QK_ROPE
    wu_ = w_in[:, u0:]
    d = D_MODEL
    w_in_p = jnp.concatenate([
        w_in[:, :c0],
        zcol(d, QK_NOPE), w_kr, zcol(d, LANE - QK_HEAD),
        zcol(d, QK_NOPE), w_kr[:, swap], zcol(d, LANE - QK_HEAD),
        wu_[:, :3 * RW_WIDTH + 2 * DECAY_LORA + A_LORA], zcol(d, LANE - A_LORA),
        wu_[:, 3 * RW_WIDTH + 2 * DECAY_LORA + A_LORA:]], axis=1)
    p['w_in'] = w_in_p.astype(BF16)
    mu = mu_shift[None, :]
    p['mu'] = jnp.concatenate([mu[:, :3 * RW_WIDTH + 2 * DECAY_LORA + A_LORA], zcol(1, LANE - A_LORA),
                               mu[:, 3 * RW_WIDTH + 2 * DECAY_LORA + A_LORA:]], axis=1)
    p['g1'] = norm1_g[None, :]
    p['qag'] = q_a_norm_g[None, :]
    p['kvag'] = kv_a_norm_g[None, :]
    p['w_uq'] = jnp.concatenate([
        _head_pad_cols(w_uq, QK_HEAD, 0, QK_HEAD, 0),
        _head_pad_cols(w_uq, QK_HEAD, QK_NOPE, QK_HEAD, QK_NOPE, perm=swap)], axis=1).astype(BF16)
    p['w_ukv'] = jnp.concatenate([
        _head_pad_cols(w_ukv, QK_NOPE + V_HEAD, 0, QK_NOPE, 0),
        _head_pad_cols(w_ukv, QK_NOPE + V_HEAD, QK_NOPE, QK_NOPE + V_HEAD, 0)], axis=1).astype(BF16)

    def gain_rows(g):
        main = jnp.concatenate([g, jnp.zeros((LANE - QK_HEAD,), F32)])
        swp = jnp.concatenate([jnp.zeros((QK_NOPE,), F32), g[QK_NOPE:][swap], jnp.zeros((LANE - QK_HEAD,), F32)])
        return jnp.stack([main, swp])
    kbound = math.sqrt(QK_HEAD) * jnp.max(jnp.abs(k_norm_g))
    qbound = jnp.max(jnp.abs(q_norm_g)) * LOG2E
    p['gq'] = jnp.concatenate([gain_rows(q_norm_g), jnp.full((1, LANE), qbound * kbound, F32)], axis=0)
    p['gk'] = gain_rows(k_norm_g)
    p['fast_softmax'] = 2.0 * qbound * kbound <= FAST_SOFTMAX_MAX_SHIFT
    inv_freq = 1.0 / (ROPE_THETA ** (jnp.arange(half, dtype=F32) / half))
    ang = jnp.arange(max_seq, dtype=F32)[:, None] * inv_freq[None, :]
    cos, sin = jnp.cos(ang), jnp.sin(ang)
    zpad = jnp.zeros((max_seq, LANE - QK_HEAD), F32)
    p['ct'] = jnp.concatenate([jnp.ones((max_seq, QK_NOPE), F32), cos, cos, zpad], axis=1)
    p['st'] = jnp.concatenate([jnp.zeros((max_seq, QK_NOPE), F32), -sin, sin, zpad], axis=1)

    zl = jnp.zeros((DECAY_LORA, RW_WIDTH), F32)
    p['w_dec'] = jnp.concatenate([jnp.concatenate([w_up_f, zl], axis=1),
                                  jnp.concatenate([zl, w_up_b], axis=1)], axis=0).astype(BF16)
    p['w0'] = jnp.concatenate([w0_f, w0_b])[None, :]
    p['a_up'] = jnp.concatenate([a_up, jnp.zeros((LANE - A_LORA, RW_WIDTH), F32)], axis=0).astype(BF16)
    p['a0'] = a0[None, :]
    p['g_up'] = g_up.astype(BF16)
    p['k_k'] = k_k[None, :]
    p['k_a'] = k_a[None, :]
    p['r_k'] = r_k.reshape(1, RW_WIDTH)
    hid = jnp.arange(RW_WIDTH) // RW_HEAD
    p['seg'] = (hid[:, None] == hid[None, :]).astype(BF16)
    ti = jnp.arange(CHUNK)[:, None]
    tj = jnp.arange(CHUNK)[None, :]
    p['sc_tri'] = jnp.stack([tj <= ti, tj >= ti]).astype(BF16)
    ps = (jnp.arange(HGW) % CHUNK)[None, :]
    p['sc_m64'] = jnp.stack([ps < ti, ps <= ti, ps > ti, ps >= ti, ps == ti]).astype(F32)
    blk = jnp.arange(HGW) // RW_HEAD
    same = blk[:, None] == blk[None, :]
    p['sc_bd16'] = same.astype(BF16)
    p['sc_bd32'] = jnp.stack([same, jnp.eye(HGW, dtype=bool)]).astype(F32)
    p['ln_g'] = ln_x_g[None, :]
    p['ln_b'] = ln_x_b[None, :]
    p['w_out'] = w_out.astype(BF16)
    p['g2'] = norm2_g[None, :]
    p['w_r'] = jnp.concatenate([w_router_expert, w_router_group,
                                zcol(d, LANE - N_EXPERTS - N_GROUPS)], axis=1).astype(BF16)
    p['b_r'] = jnp.concatenate([b_router_expert, b_router_group,
                                jnp.zeros((LANE - N_EXPERTS - N_GROUPS,), F32)])[None, :]
    p['wg'] = w_expert_gate.astype(BF16)
    p['wu'] = w_expert_up.astype(BF16)
    p['wd'] = w_expert_down.astype(BF16)
    return p


TM_PROJ = 512
PROJ_SPLIT = 2
TQ = 1024
TK = 8192
TKS = 2048
TQ_MAX = 1024
TK_MAX = 2048
TKS_MAX = 1024
TM_POST = 512
TM_MOE = 512
TM_ROWS = 1024


def _layer(x, p):
    bsz, seq, d = x.shape
    n = bsz * seq
    x2 = x.reshape(n, d)
    q, k, v, r, k2, vv, kkn, kka, lwf, lwb, g, bonus = _proj(x2, seq, p, TM_PROJ)
    oa = _attn(q, k, v, bsz, seq, p['fast_softmax'])
    yf, yb = _rwscan(r, k2, vv, kkn, kka, lwf, lwb, bsz, seq, p)
    x1, hp, gid = _post(x2, oa, yf, yb, bonus, g, p, TM_POST)
    tm = min(TM_MOE, n)
    npt = n // tm + N_GROUPS
    pos, meta = _route_plan(gid, tm, npt)
    xs = _scatter_rows(pos, hp, npt * tm, TM_ROWS)
    ys = _moe_grouped(xs, meta, p, tm, npt)
    out = _gather_add(pos, x1, ys, TM_ROWS)
    return out.reshape(bsz, seq, d)


def kernel(x_prompt, x_sample, norm1_g, w_in, q_a_norm_g, w_uq, kv_a_norm_g, w_ukv, q_norm_g, k_norm_g, mu_shift, w0_f, w_up_f, w0_b, w_up_b, a0, a_up, g_up, k_k, k_a, r_k, ln_x_g, ln_x_b, w_out, norm2_g, w_router_group, b_router_group, w_router_expert, b_router_expert, w_expert_gate, w_expert_up, w_expert_down):
    layer_params = (norm1_g, w_in, q_a_norm_g, w_uq, kv_a_norm_g, w_ukv, q_norm_g, k_norm_g, mu_shift,
                    w0_f, w_up_f, w0_b, w_up_b, a0, a_up, g_up, k_k, k_a, r_k, ln_x_g, ln_x_b, w_out,
                    norm2_g, w_router_group, b_router_group, w_router_expert, b_router_expert,
                    w_expert_gate, w_expert_up, w_expert_down)
    y_prompt, y_sample = x_prompt, x_sample
    max_seq = max(x_prompt.shape[1], x_sample.shape[1])
    for layer in range(norm1_g.shape[0]):
        p = _pack_params([w[layer] for w in layer_params], max_seq)
        y_prompt = _layer(y_prompt, p)
        y_sample = _layer(y_sample, p)
    return (y_prompt, y_sample)
```

```python
import functools
import math

import jax
import jax.numpy as jnp
from jax import lax
from jax.experimental import pallas as pl
from jax.experimental.pallas import tpu as pltpu

D_MODEL = 1024
MLA_HEADS = 8
QK_NOPE = 64
QK_ROPE = 32
QK_HEAD = QK_NOPE + QK_ROPE
V_HEAD = 64
Q_LORA = 384
KV_LORA = 256
ROPE_THETA = 10000.0
RW_HEADS = 8
RW_HEAD = 64
RW_WIDTH = RW_HEADS * RW_HEAD
DECAY_LORA = 64
A_LORA = 64
GATE_LORA = 128
LN_X_EPS = 6.4e-4
MLA_WIDTH = MLA_HEADS * V_HEAD
N_GROUPS = 4
EXPERTS_PER_GROUP = 8
N_EXPERTS = N_GROUPS * EXPERTS_PER_GROUP
D_EXPERT = 256
MOE_EPC = 2
RMS_EPS = 1e-6

LANE = 128
HEAD_PAD = LANE
MLA_PAD = MLA_HEADS * HEAD_PAD
RW_COLS_PAD = 3 * RW_WIDTH + 3 * LANE
MLA_COLS_PAD = Q_LORA + KV_LORA + 2 * LANE
D_IN_PAD = MLA_COLS_PAD + RW_COLS_PAD
CHUNK = 64
HG = 4
HGW = HG * RW_HEAD
VMEM_LIMIT = 56 * 1024 * 1024
LOG2E = 1.4426950408889634
NCH = 8
FAST_SOFTMAX_MAX_SHIFT = 100.0

F32 = jnp.float32
BF16 = jnp.bfloat16


def _dot(a, b):
    return jnp.dot(a, b, preferred_element_type=F32)


def _split_dot(a, b_exact):
    hi = a.astype(BF16)
    lo = (a - hi.astype(F32)).astype(BF16)
    return _dot(hi, b_exact) + _dot(lo, b_exact)


def _cparams(sem):
    return pltpu.CompilerParams(dimension_semantics=sem, vmem_limit_bytes=VMEM_LIMIT)


def _proj_kernel(x_ref, xp_ref, xn_ref, g1_ref, win_ref, qag_ref, wuq_ref, kvag_ref, wukv_ref, gq_ref, gk_ref,
                 ct_ref, st_ref, *rw_refs, tm, seq):
    q_out, k_out, v_out = rw_refs[10:13]
    i = pl.program_id(0)
    x = jnp.concatenate([xp_ref[0], x_ref[...], xn_ref[0]], axis=0)
    h = x * lax.rsqrt(jnp.mean(x * x, axis=-1, keepdims=True) + RMS_EPS) * g1_ref[...]
    ze = _dot(h.astype(BF16), win_ref[...])
    first = (i * tm) % seq == 0
    last = ((i + 1) * tm) % seq == 0

    gq = gq_ref[...]
    gk = gk_ref[...]
    scale = QK_HEAD ** -0.5 * LOG2E
    lane = lax.broadcasted_iota(jnp.int32, (1, LANE), 1)
    ones_col = jnp.where(lane == V_HEAD, 1.0, 0.0).astype(F32)
    shift_col = jnp.where(lane == QK_HEAD, 1.0, 0.0).astype(F32)
    q_shift = shift_col * gq[2:3, 0:1]

    def rows_gen(a, b):
        rs = slice(a, b)
        z = ze[8 + a:8 + b]
        prev_row = ze[7 + a:8 + a, MLA_COLS_PAD:]
        next_row = ze[8 + b:9 + b, MLA_COLS_PAD:]
        if a == 0:
            prev_row = jnp.where(first, 0.0, prev_row)
        if b == tm:
            next_row = jnp.where(last, 0.0, next_row)
        yield from _rw_token_maps(z[:, MLA_COLS_PAD:], prev_row, next_row, rs, *rw_refs[:10], *rw_refs[13:])

        cq = z[:, :Q_LORA]
        cqn = cq * lax.rsqrt(jnp.mean(cq * cq, axis=-1, keepdims=True) + RMS_EPS) * qag_ref[...]
        q2 = _dot(cqn.astype(BF16), wuq_ref[...])
        ckv = z[:, Q_LORA:Q_LORA + KV_LORA]
        ckvn = ckv * lax.rsqrt(jnp.mean(ckv * ckv, axis=-1, keepdims=True) + RMS_EPS) * kvag_ref[...]
        kv2 = _dot(ckvn.astype(BF16), wukv_ref[...])
        yield
        kr_main = z[:, Q_LORA + KV_LORA:Q_LORA + KV_LORA + LANE]
        kr_swap = z[:, Q_LORA + KV_LORA + LANE:MLA_COLS_PAD]
        ct = ct_ref[rs, :]
        st = st_ref[rs, :]
        kr_ssq = jnp.sum(kr_main * kr_main, axis=-1, keepdims=True)
        q_main_f = gq[0:1] * ct
        q_swap_f = gq[1:2] * st
        k_rot = kr_main * (gk[0:1] * ct) + kr_swap * (gk[1:2] * st)
        for hd in range(MLA_HEADS):
            sl = slice(hd * HEAD_PAD, (hd + 1) * HEAD_PAD)
            qm = q2[:, sl]
            qs = q2[:, MLA_PAD + hd * HEAD_PAD:MLA_PAD + (hd + 1) * HEAD_PAD]
            rinv = lax.rsqrt(jnp.sum(qm * qm, axis=-1, keepdims=True) * (1.0 / QK_HEAD) + RMS_EPS)
            qh = (qm * q_main_f + qs * q_swap_f) * (rinv * scale)
            q_out[rs, sl] = (qh - q_shift).astype(BF16)
            kn = kv2[:, sl]
            rinv_k = lax.rsqrt((jnp.sum(kn * kn, axis=-1, keepdims=True) + kr_ssq) * (1.0 / QK_HEAD) + RMS_EPS)
            kh = (kn * gk[0:1] + k_rot) * rinv_k
            k_out[rs, sl] = (kh + shift_col).astype(BF16)
            v_out[rs, sl] = (kv2[:, MLA_PAD + hd * HEAD_PAD:MLA_PAD + (hd + 1) * HEAD_PAD] + ones_col).astype(BF16)

    nsplit = PROJ_SPLIT if tm % (8 * PROJ_SPLIT) == 0 else 1
    rows_per = tm // nsplit
    _interleave([rows_gen(j * rows_per, (j + 1) * rows_per) for j in range(nsplit)])


def _proj(x2, seq, p, tm):
    n = x2.shape[0]
    tm = min(tm, seq)
    nseq_t = seq // tm
    row = lambda i: (i, 0)
    fixed = lambda i: (0, 0)
    pos = lambda i: (i % nseq_t, 0)
    full = lambda a: pl.BlockSpec(a.shape, fixed)
    g8 = tm // 8
    ngrp = n // 8
    x3 = x2.reshape(ngrp, 8, D_MODEL)
    rw_names = ('mu', 'w_dec', 'w0', 'a_up', 'a0', 'g_up', 'k_k', 'k_a', 'r_k', 'seg')
    mla_sds = jax.ShapeDtypeStruct((n, MLA_PAD), BF16)
    rw_sds = jax.ShapeDtypeStruct((n, RW_WIDTH), F32)
    return pl.pallas_call(
        functools.partial(_proj_kernel, tm=tm, seq=seq),
        grid=(n // tm,),
        in_specs=[pl.BlockSpec((tm, D_MODEL), row),
                  pl.BlockSpec((1, 8, D_MODEL), lambda i: (jnp.maximum(i * g8 - 1, 0), 0, 0)),
                  pl.BlockSpec((1, 8, D_MODEL), lambda i: (jnp.minimum((i + 1) * g8, ngrp - 1), 0, 0)),
                  full(p['g1']), full(p['w_in']), full(p['qag']),
                  full(p['w_uq']), full(p['kvag']), full(p['w_ukv']), full(p['gq']), full(p['gk']),
                  pl.BlockSpec((tm, LANE), pos), pl.BlockSpec((tm, LANE), pos)]
                 + [full(p[k]) for k in rw_names],
        out_specs=[pl.BlockSpec((tm, MLA_PAD), row)] * 3 + [pl.BlockSpec((tm, RW_WIDTH), row)] * 9,
        out_shape=[mla_sds] * 3 + [rw_sds] * 9,
        compiler_params=_cparams(("parallel",)),
        name="proj",
    )(x2, x3, x3, p['g1'], p['w_in'], p['qag'], p['w_uq'], p['kvag'], p['w_ukv'], p['gq'], p['gk'],
      p['ct'][:seq], p['st'][:seq], *[p[k] for k in rw_names])


def _attn_kernel(q_ref, k_ref, v_ref, o_ref, m_ref, acc_ref, *, running_max, tks):
    j = pl.program_id(3)

    @pl.when(j == 0)
    def _():
        if running_max:
            m_ref[...] = jnp.full(m_ref.shape, -jnp.inf, F32)
        acc_ref[...] = jnp.zeros(acc_ref.shape, F32)

    heads = [slice(hh * HEAD_PAD, (hh + 1) * HEAD_PAD) for hh in range(2)]
    nsub = k_ref.shape[0] // tks

    def score_pair(sb):
        rows = slice(sb * tks, (sb + 1) * tks)
        return [lax.dot_general(k_ref[rows, sl], q_ref[:, sl], (((1,), (1,)), ((), ())),
                                preferred_element_type=F32) for sl in heads]

    accs = [acc_ref[hh] for hh in range(2)]
    maxes = [m_ref[hh] for hh in range(2)] if running_max else None
    scores = score_pair(0)
    for sb in range(nsub):
        nxt = score_pair(sb + 1) if sb + 1 < nsub else None
        rows = slice(sb * tks, (sb + 1) * tks)
        for hh, sl in enumerate(heads):
            s = scores[hh]
            if running_max:
                m_new = jnp.maximum(maxes[hh], jnp.max(s, axis=0, keepdims=True))
                p = jnp.exp2(s - m_new)
            else:
                p = jnp.exp2(s)
            pv = lax.dot_general(v_ref[rows, sl], p.astype(BF16), (((0,), (0,)), ((), ())),
                                 preferred_element_type=F32)
            if running_max:
                accs[hh] = jnp.exp2(maxes[hh] - m_new) * accs[hh] + pv
                maxes[hh] = m_new
            else:
                accs[hh] = accs[hh] + pv
        scores = nxt
    for hh in range(2):
        acc_ref[hh] = accs[hh]
        if running_max:
            m_ref[hh] = maxes[hh]

    @pl.when(j == pl.num_programs(3) - 1)
    def _():
        outs = []
        for hh in range(2):
            a = acc_ref[hh]
            outs.append(a[:V_HEAD] / a[V_HEAD:V_HEAD + 1])
        o_ref[...] = jnp.concatenate(outs, axis=0).T.astype(BF16)


def _attn(q, k, v, bsz, seq, fast_softmax):
    return lax.cond(fast_softmax,
                    lambda q, k, v: _attn_call(q, k, v, bsz, seq, TQ, TK, TKS, False),
                    lambda q, k, v: _attn_call(q, k, v, bsz, seq, TQ_MAX, TK_MAX, TKS_MAX, True),
                    q, k, v)


def _attn_call(q, k, v, bsz, seq, tq, tk, tks, running_max):
    tq = min(tq, seq)
    tk = min(tk, seq)
    tks = min(tks, tk)
    nq, nk = seq // tq, seq // tk
    n = q.shape[0]
    return pl.pallas_call(
        functools.partial(_attn_kernel, running_max=running_max, tks=tks),
        grid=(bsz, MLA_HEADS // 2, nq, nk),
        in_specs=[pl.BlockSpec((tq, 2 * HEAD_PAD), lambda b, h, i, j: (b * nq + i, h)),
                  pl.BlockSpec((tk, 2 * HEAD_PAD), lambda b, h, i, j: (b * nk + j, h)),
                  pl.BlockSpec((tk, 2 * HEAD_PAD), lambda b, h, i, j: (b * nk + j, h))],
        out_specs=pl.BlockSpec((tq, 2 * V_HEAD), lambda b, h, i, j: (b * nq + i, h)),
        out_shape=jax.ShapeDtypeStruct((n, MLA_WIDTH), BF16),
        scratch_shapes=[pltpu.VMEM((2, 1, tq), F32), pltpu.VMEM((2, HEAD_PAD, tq), F32)],
        compiler_params=_cparams(("parallel", "parallel", "parallel", "arbitrary")),
        name="attn_max" if running_max else "attn",
    )(q, k, v)


def _rw_token_maps(u, prev_row, next_row, rs, mu_ref, wdec_ref, w0_ref, aup_ref, a0_ref, gup_ref,
                   kk_ref, ka_ref, rk_ref, seg_ref,
                   r_out, k_out, v_out, kkn_out, kka_out, lwf_out, lwb_out, g_out, bonus_out):
    tm = u.shape[0]
    sub = lax.broadcasted_iota(jnp.int32, (8, 1), 0)
    prev = pltpu.roll(u, 1, 0)
    prev = jnp.concatenate([jnp.where(sub == 0, prev_row, prev[:8]), prev[8:]], axis=0)
    nxt = pltpu.roll(u, tm - 1, 0)
    nxt = jnp.concatenate([nxt[:tm - 8], jnp.where(sub == 7, next_row, nxt[tm - 8:])], axis=0)
    us = u + (0.5 * (prev + nxt) - u) * mu_ref[...]

    r = us[:, :RW_WIDTH]
    k = us[:, RW_WIDTH:2 * RW_WIDTH]
    v = us[:, 2 * RW_WIDTH:3 * RW_WIDTH]
    xw = us[:, 3 * RW_WIDTH:3 * RW_WIDTH + LANE]
    xa = us[:, 3 * RW_WIDTH + LANE:3 * RW_WIDTH + 2 * LANE]
    xg = us[:, 3 * RW_WIDTH + 2 * LANE:]
    r_out[rs, :] = r
    v_out[rs, :] = v

    wdot = _dot(jnp.tanh(xw).astype(BF16), wdec_ref[...])
    adot = _dot(xa.astype(BF16), aup_ref[...])
    gdot = _dot(jax.nn.sigmoid(xg).astype(BF16), gup_ref[...])
    seg = seg_ref[...]
    kk = k * kk_ref[...]
    kss = _split_dot(kk * kk, seg)
    yield
    lw = jax.nn.sigmoid(w0_ref[...] + wdot) * (-math.exp(-0.5))
    lwf_out[rs, :] = lw[:, :RW_WIDTH]
    lwb_out[rs, :] = lw[:, RW_WIDTH:]
    a = jax.nn.sigmoid(a0_ref[...] + adot)
    g_out[rs, :] = gdot
    kkn = kk * lax.rsqrt(jnp.maximum(kss, 1e-24))
    k2 = k * (1.0 + (a - 1.0) * ka_ref[...])
    k_out[rs, :] = k2
    kkn_out[rs, :] = kkn
    kka_out[rs, :] = kkn * a
    bsum = _split_dot(r * k2 * rk_ref[...], seg)
    yield
    bonus_out[rs, :] = bsum * v


def _bd(x, bd16):
    return jnp.concatenate([x.astype(BF16)] * HG, axis=0) * bd16


def _chunk_intra(r, k, v, kkn, kka, lw, reverse, consts):
    tri, strict_m, incl_m, eye_p, bd16, bd32, eye32 = consts
    lg = _split_dot_left(tri, lw)
    yield
    lgx = lg - lw
    tot = lg[0:1] if reverse else lg[CHUNK - 1:CHUNK]
    gi = jnp.exp(lg)
    ginv = jnp.exp(-lg)
    gend = jnp.exp(tot - lg)
    at = -kkn * jnp.exp(lgx)
    rt = r * gi
    bt = kka * ginv
    kt = k * ginv
    bh = kka * gend
    kh = k * gend

    lhs = jnp.concatenate([at, rt], axis=0).astype(BF16)
    rhs = jnp.concatenate([_bd(bt, bd16), _bd(kt, bd16)], axis=0)
    a_all = lax.dot_general(lhs, rhs, (((1,), (1,)), ((), ())), preferred_element_type=F32)
    yield
    n_ab = a_all[:CHUNK, :HGW] * strict_m
    a_ak = a_all[:CHUNK, HGW:] * strict_m
    a_rb = a_all[CHUNK:, :HGW] * incl_m
    a_rk = a_all[CHUNK:, HGW:] * incl_m

    t = eye_p + n_ab
    nk = _dot(n_ab.astype(BF16), _bd(n_ab, bd16))
    yield
    for _ in range(4):
        both = _dot(jnp.concatenate([nk, t], axis=0).astype(BF16), _bd(nk, bd16))
        yield
        nk = both[:CHUNK]
        t = t + both[CHUNK:]
    tn = _dot(t.astype(BF16), _bd(nk, bd16))
    yield
    t = t + tn

    tb = t.astype(BF16)
    vbd = _bd(v, bd16)
    w = _dot(tb, _bd(at, bd16))
    akv = _dot(a_ak.astype(BF16), vbd)
    yield
    uv = _dot(tb, _bd(akv, bd16))
    arb = a_rb.astype(BF16)
    qpd = _dot(arb, _bd(w, bd16))
    yield
    y_in = _dot(jnp.concatenate([arb, a_rk.astype(BF16)], axis=1),
                jnp.concatenate([_bd(uv, bd16), vbd], axis=0))
    lhs_t = jnp.concatenate([bh, kh], axis=0).astype(BF16)
    rhs_t = jnp.concatenate([jnp.concatenate([w, uv], axis=1),
                             jnp.concatenate([jnp.zeros_like(v), v], axis=1)], axis=0).astype(BF16)
    mc = lax.dot_general(lhs_t, rhs_t, (((0,), (0,)), ((), ())), preferred_element_type=F32)
    yield
    m_bd = mc[:, :HGW] * bd32 + eye32 * jnp.exp(tot)
    c_bd = mc[:, HGW:] * bd32
    return y_in, (rt + qpd).astype(BF16), m_bd.astype(BF16), c_bd


def _interleave(gens):
    results = [None] * len(gens)
    active = list(range(len(gens)))
    while active:
        still = []
        for i in active:
            try:
                next(gens[i])
                still.append(i)
            except StopIteration as stop:
                results[i] = stop.value
        active = still
    return results


def _split_dot_left(b_exact, a):
    hi = a.astype(BF16)
    lo = (a - hi.astype(F32)).astype(BF16)
    return _dot(b_exact, hi) + _dot(b_exact, lo)


def _rwscan_kernel(rf, kf, vf, nf, af, lf, rb, kb, vb, nb, ab, lb, tri_ref, m64_ref, bd16_ref, bd32_ref,
                   yf_out, yb_out, s_ref):
    c = pl.program_id(1)

    @pl.when(c == 0)
    def _():
        s_ref[...] = jnp.zeros(s_ref.shape, F32)

    nch = rf.shape[0] // CHUNK
    bd16 = bd16_ref[...]
    bd32 = bd32_ref[0]
    eye32 = bd32_ref[1]
    eye_p = m64_ref[4]
    ngrp = RW_HEADS // HG
    chains = []
    gens = []
    for d, refs, y_out in ((0, (rf, kf, vf, nf, af, lf), yf_out), (1, (rb, kb, vb, nb, ab, lb), yb_out)):
        reverse = d == 1
        consts = (tri_ref[d], m64_ref[2 * d], m64_ref[2 * d + 1], eye_p, bd16, bd32, eye32)
        order = list(range(nch - 1, -1, -1) if reverse else range(nch))
        for g in range(ngrp):
            chains.append((d, g, y_out, order))
            for j in order:
                ops = [x[j * CHUNK:(j + 1) * CHUNK, g * HGW:(g + 1) * HGW] for x in refs]
                gens.append(_chunk_intra(*ops, reverse, consts))
    parts = _interleave(gens)

    states = [s_ref[d, g] for d, g, _, _ in chains]
    for step in range(nch):
        for ci, (d, g, y_out, order) in enumerate(chains):
            j = order[step]
            y_in, qp, m_bd, c_bd = parts[ci * nch + step]
            sb = states[ci].astype(BF16)
            y_out[j * CHUNK:(j + 1) * CHUNK, g * HGW:(g + 1) * HGW] = y_in + _dot(qp, sb)
            states[ci] = _dot(m_bd, sb) + c_bd
    for ci, (d, g, _, _) in enumerate(chains):
        s_ref[d, g] = states[ci]


def _rwscan(r, k, v, kkn, kka, lwf, lwb, bsz, seq, p):
    n = r.shape[0]
    rows = min(NCH * CHUNK, seq)
    nb = seq // rows
    fwd = pl.BlockSpec((rows, RW_WIDTH), lambda b, c: (b * nb + c, 0))
    bwd = pl.BlockSpec((rows, RW_WIDTH), lambda b, c: (b * nb + nb - 1 - c, 0))
    full = lambda a: pl.BlockSpec(a.shape, lambda b, c: (0,) * a.ndim)
    sds = jax.ShapeDtypeStruct((n, RW_WIDTH), F32)
    return pl.pallas_call(
        _rwscan_kernel,
        grid=(bsz, nb),
        in_specs=[fwd] * 6 + [bwd] * 6 + [full(p['sc_tri']), full(p['sc_m64']), full(p['sc_bd16']),
                                           full(p['sc_bd32'])],
        out_specs=[fwd, bwd],
        out_shape=[sds, sds],
        scratch_shapes=[pltpu.VMEM((2, RW_HEADS // HG, HGW, HGW), F32)],
        compiler_params=_cparams(("parallel", "arbitrary")),
        name="rwscan",
    )(r, k, v, kkn, kka, lwf, r, k, v, kkn, kka, lwb, p['sc_tri'], p['sc_m64'], p['sc_bd16'], p['sc_bd32'])


def _post_kernel(x_ref, oa_ref, yf_ref, yb_ref, bonus_ref, g_ref, lng_ref, lnb_ref, seg_ref, wout_ref,
                 g2_ref, wr_ref, br_ref, x1_out, hp_out, gid_out):
    tm = x_ref.shape[0]
    nsplit = 2 if tm % 16 == 0 else 1
    rows_per = tm // nsplit

    def rows_gen(rs):
        seg = seg_ref[...]
        oa_part = _dot(oa_ref[rs, :], wout_ref[:MLA_WIDTH, :])
        y = yf_ref[rs, :] + yb_ref[rs, :]
        mean = _split_dot(y, seg) * (1.0 / RW_HEAD)
        yield
        dlt = y - mean
        var = _split_dot(dlt * dlt, seg) * (1.0 / RW_HEAD)
        yield
        yn = dlt * lax.rsqrt(var + LN_X_EPS) * lng_ref[...] + lnb_ref[...]
        ob = (yn + bonus_ref[rs, :]) * g_ref[rs, :]
        x1 = x_ref[rs, :] + oa_part + _dot(ob.astype(BF16), wout_ref[MLA_WIDTH:, :])
        yield
        x1_out[rs, :] = x1
        h2 = x1 * lax.rsqrt(jnp.mean(x1 * x1, axis=-1, keepdims=True) + RMS_EPS) * g2_ref[...]
        h2b = h2.astype(BF16)
        bits = lax.bitcast_convert_type(h2b.astype(F32), jnp.uint32)
        hp_out[rs, :] = (bits[:, :D_MODEL // 2] >> 16) | (bits[:, D_MODEL // 2:] & jnp.uint32(0xFFFF0000))
        logits = _dot(h2b, wr_ref[...]) + br_ref[...]
        yield
        _, gidx, _ = _group_choice(logits)
        gid_out[rs, :] = jnp.broadcast_to(gidx, (rows_per, LANE))

    _interleave([rows_gen(slice(j * rows_per, (j + 1) * rows_per)) for j in range(nsplit)])


def _group_choice(logits):
    lane_i = lax.broadcasted_iota(jnp.int32, logits.shape, 1)
    lane = lane_i.astype(F32)
    is_g = (lane_i >= N_EXPERTS) & (lane_i < N_EXPERTS + N_GROUPS)
    gl = jnp.where(is_g, logits, -jnp.inf)
    gmax = jnp.max(gl, axis=-1, keepdims=True)
    gidx = jnp.min(jnp.where(gl == gmax, lane, 1e9), axis=-1, keepdims=True) - N_EXPERTS
    p_g = 1.0 / jnp.sum(jnp.where(is_g, jnp.exp(gl - gmax), 0.0), axis=-1, keepdims=True)
    return p_g, gidx, lane


def _post(x2, oa, yf, yb, bonus, g, p, tm):
    n = x2.shape[0]
    tm = min(tm, n)
    row = lambda i: (i, 0)
    fixed = lambda i: (0, 0)
    full = lambda a: pl.BlockSpec(a.shape, fixed)
    rw = pl.BlockSpec((tm, RW_WIDTH), row)
    return pl.pallas_call(
        _post_kernel,
        grid=(n // tm,),
        in_specs=[pl.BlockSpec((tm, D_MODEL), row), pl.BlockSpec((tm, MLA_WIDTH), row), rw, rw, rw, rw,
                  full(p['ln_g']), full(p['ln_b']), full(p['seg']), full(p['w_out']), full(p['g2']),
                  full(p['w_r']), full(p['b_r'])],
        out_specs=[pl.BlockSpec((tm, D_MODEL), row), pl.BlockSpec((tm, D_MODEL // 2), row),
                   pl.BlockSpec((tm, LANE), row)],
        out_shape=[jax.ShapeDtypeStruct((n, D_MODEL), F32), jax.ShapeDtypeStruct((n, D_MODEL // 2), jnp.uint32),
                   jax.ShapeDtypeStruct((n, LANE), F32)],
        compiler_params=_cparams(("parallel",)),
        name="post",
    )(x2, oa, yf, yb, bonus, g, p['ln_g'], p['ln_b'], p['seg'], p['w_out'], p['g2'], p['w_r'], p['b_r'])


def _route_plan(gid, tm, npt):
    g = gid[:, 0].astype(jnp.int32)
    onehot = (g[:, None] == jnp.arange(N_GROUPS, dtype=jnp.int32)[None, :]).astype(jnp.int32)
    csum = jnp.cumsum(onehot, axis=0)
    rank = jnp.sum(csum * onehot, axis=1) - 1
    ntile = (csum[-1] + tm - 1) // tm
    tile_end = jnp.cumsum(ntile)
    off = (tile_end - ntile) * tm
    pos = jnp.sum(onehot * off[None, :], axis=1) + rank
    n_active = tile_end[-1]
    t = jnp.minimum(jnp.arange(npt, dtype=jnp.int32), n_active - 1)
    tile_gid = jnp.sum((t[:, None] >= tile_end[None, :]).astype(jnp.int32), axis=1)
    meta = jnp.concatenate([n_active[None], tile_gid]).astype(jnp.int32)
    return pos.astype(jnp.int32), meta


def _row_dma_loops(n_rows, make_copy):
    def issue(h, carry):
        make_copy(2 * h).start(priority=0)
        make_copy(2 * h + 1).start(priority=1)
        return carry
    lax.fori_loop(0, n_rows // 2, issue, 0, unroll=4)

    def drain(r, carry):
        make_copy(0).wait()
        return carry
    lax.fori_loop(0, n_rows, drain, 0, unroll=8)


def _scatter_kernel(pos_ref, src_ref, init_ref, dst_ref, sem):
    _row_dma_loops(src_ref.shape[0], lambda r: pltpu.make_async_copy(
        src_ref.at[pl.ds(r, 1)], dst_ref.at[pl.ds(pos_ref[0, 0, r], 1)], sem))


def _scatter_rows(pos, src, n_dst, tms):
    n, w = src.shape
    tms = min(tms, n)
    return pl.pallas_call(
        _scatter_kernel,
        grid=(n // tms,),
        in_specs=[pl.BlockSpec((1, 1, tms), lambda i: (i, 0, 0), memory_space=pltpu.SMEM),
                  pl.BlockSpec((tms, w), lambda i: (i, 0)), pl.BlockSpec(memory_space=pl.ANY)],
        out_specs=pl.BlockSpec(memory_space=pl.ANY),
        out_shape=jax.ShapeDtypeStruct((n_dst, w), src.dtype),
        scratch_shapes=[pltpu.SemaphoreType.DMA(())],
        input_output_aliases={2: 0},
        compiler_params=_cparams(("arbitrary",)),
        name="moe_scatter",
    )(pos.reshape(n // tms, 1, tms), src, jnp.zeros((n_dst, w), src.dtype))


def _moe_group_kernel(meta_ref, xs_ref, wr_ref, br_ref, wg_ref, wu_ref, wd_ref, ys_ref):
    i = pl.program_id(0)

    @pl.when(i >= meta_ref[0])
    def _():
        ys_ref[...] = jnp.zeros(ys_ref.shape, F32)

    @pl.when(i < meta_ref[0])
    def _():
        tm = xs_ref.shape[0]
        word = xs_ref[...]
        h2 = jnp.concatenate([lax.bitcast_convert_type(word << 16, F32),
                              lax.bitcast_convert_type(word & jnp.uint32(0xFFFF0000), F32)],
                             axis=1).astype(BF16)
        def up(c):
            ks = range(c * MOE_EPC, (c + 1) * MOE_EPC)
            return (jnp.concatenate([_dot(h2, wg_ref[k]) for k in ks], axis=1),
                    jnp.concatenate([_dot(h2, wu_ref[k]) for k in ks], axis=1))

        pend = up(0)
        logits = _dot(h2, wr_ref[...]) + br_ref[...]
        p_g, _, lane = _group_choice(logits)
        first = (meta_ref[1 + i] * EXPERTS_PER_GROUP).astype(F32)
        in_grp = (lane >= first) & (lane < first + EXPERTS_PER_GROUP)
        el = jnp.where(in_grp, logits, -jnp.inf)
        v1 = jnp.max(el, axis=-1, keepdims=True)
        i1 = jnp.min(jnp.where(el == v1, lane, 1e9), axis=-1, keepdims=True)
        el2 = jnp.where(lane == i1, -jnp.inf, el)
        v2 = jnp.max(el2, axis=-1, keepdims=True)
        i2 = jnp.min(jnp.where(el2 == v2, lane, 1e9), axis=-1, keepdims=True)
        e2 = jnp.exp(v2 - v1)
        den = 1.0 + e2
        w1 = (1.0 / den) * p_g
        w2 = (e2 / den) * p_g

        acc = None
        nchunk = EXPERTS_PER_GROUP // MOE_EPC
        for c in range(nchunk):
            nxt = up(c + 1) if c + 1 < nchunk else None
            hg, hu = pend
            gate = jnp.concatenate(
                [jnp.broadcast_to(jnp.where(i1 == first + k, w1, 0.0) + jnp.where(i2 == first + k, w2, 0.0),
                                  (tm, D_EXPERT)) for k in range(c * MOE_EPC, (c + 1) * MOE_EPC)], axis=1)
            act = hg * jax.nn.sigmoid(hg) * hu * gate
            wd = wd_ref[c * MOE_EPC:(c + 1) * MOE_EPC].reshape(MOE_EPC * D_EXPERT, D_MODEL)
            part = _dot(act.astype(BF16), wd)
            acc = part if acc is None else acc + part
            pend = nxt
        ys_ref[...] = acc


def _moe_grouped(xs, meta, p, tm, npt):
    def tile(i, m):
        return (jnp.minimum(i, m[0] - 1), 0)

    def group(i, m):
        return (m[1 + i], 0, 0)

    grid_spec = pltpu.PrefetchScalarGridSpec(
        num_scalar_prefetch=1,
        grid=(npt,),
        in_specs=[pl.BlockSpec((tm, D_MODEL // 2), tile),
                  pl.BlockSpec(p['w_r'].shape, lambda i, m: (0, 0)),
                  pl.BlockSpec(p['b_r'].shape, lambda i, m: (0, 0)),
                  pl.BlockSpec((EXPERTS_PER_GROUP, D_MODEL, D_EXPERT), group),
                  pl.BlockSpec((EXPERTS_PER_GROUP, D_MODEL, D_EXPERT), group),
                  pl.BlockSpec((EXPERTS_PER_GROUP, D_EXPERT, D_MODEL), group)],
        out_specs=pl.BlockSpec((tm, D_MODEL), lambda i, m: (i, 0)),
    )
    return pl.pallas_call(
        _moe_group_kernel,
        grid_spec=grid_spec,
        out_shape=jax.ShapeDtypeStruct((npt * tm, D_MODEL), F32),
        compiler_params=_cparams(("arbitrary",)),
        name="moe_group",
    )(meta, xs, p['w_r'], p['b_r'], p['wg'], p['wu'], p['wd'])


def _gather_kernel(pos_ref, x1_ref, ys_ref, o_ref, buf, sem):
    _row_dma_loops(buf.shape[0], lambda r: pltpu.make_async_copy(
        ys_ref.at[pl.ds(pos_ref[0, 0, r], 1)], buf.at[pl.ds(r, 1)], sem))
    o_ref[...] = x1_ref[...] + buf[...]


def _gather_add(pos, x1, ys, tms):
    n, d = x1.shape
    tms = min(tms, n)
    return pl.pallas_call(
        _gather_kernel,
        grid=(n // tms,),
        in_specs=[pl.BlockSpec((1, 1, tms), lambda i: (i, 0, 0), memory_space=pltpu.SMEM),
                  pl.BlockSpec((tms, d), lambda i: (i, 0)),
                  pl.BlockSpec(memory_space=pl.ANY)],
        out_specs=pl.BlockSpec((tms, d), lambda i: (i, 0)),
        out_shape=jax.ShapeDtypeStruct((n, d), F32),
        scratch_shapes=[pltpu.VMEM((tms, d), F32), pltpu.SemaphoreType.DMA(())],
        compiler_params=_cparams(("arbitrary",)),
        name="moe_gather",
    )(pos.reshape(n // tms, 1, tms), x1, ys)


def _head_pad_cols(w, head_w, lo, hi, dst, width=HEAD_PAD, perm=None):
    rows = w.shape[0]
    wh = w.reshape(rows, -1, head_w)[:, :, lo:hi]
    if perm is not None:
        wh = wh[:, :, perm]
    out = jnp.zeros((rows, wh.shape[1], width), w.dtype)
    return out.at[:, :, dst:dst + (hi - lo)].set(wh).reshape(rows, -1)


def _pack_params(lp, max_seq):
    (norm1_g, w_in, q_a_norm_g, w_uq, kv_a_norm_g, w_ukv, q_norm_g, k_norm_g, mu_shift, w0_f, w_up_f,
     w0_b, w_up_b, a0, a_up, g_up, k_k, k_a, r_k, ln_x_g, ln_x_b, w_out, norm2_g, w_router_group,
     b_router_group, w_router_expert, b_router_expert, w_expert_gate, w_expert_up, w_expert_down) = lp
    half = QK_ROPE // 2
    swap = jnp.concatenate([jnp.arange(half, QK_ROPE), jnp.arange(half)])
    zcol = lambda rows, n: jnp.zeros((rows, n), F32)
    p = {}
    c0 = Q_LORA + KV_LORA
    w_kr = w_in[:, c0:c0 + QK_ROPE]
    u0 = c0 + QK_ROPE
    wu_ = w_in[:, u0:]
    d = D_MODEL
    w_in_p = jnp.concatenate([
        w_in[:, :c0],
        zcol(d, QK_NOPE), w_kr, zcol(d, LANE - QK_HEAD),
        zcol(d, QK_NOPE), w_kr[:, swap], zcol(d, LANE - QK_HEAD),
        wu_[:, :3 * RW_WIDTH + 2 * DECAY_LORA + A_LORA], zcol(d, LANE - A_LORA),
        wu_[:, 3 * RW_WIDTH + 2 * DECAY_LORA + A_LORA:]], axis=1)
    p['w_in'] = w_in_p.astype(BF16)
    mu = mu_shift[None, :]
    p['mu'] = jnp.concatenate([mu[:, :3 * RW_WIDTH + 2 * DECAY_LORA + A_LORA], zcol(1, LANE - A_LORA),
                               mu[:, 3 * RW_WIDTH + 2 * DECAY_LORA + A_LORA:]], axis=1)
    p['g1'] = norm1_g[None, :]
    p['qag'] = q_a_norm_g[None, :]
    p['kvag'] = kv_a_norm_g[None, :]
    p['w_uq'] = jnp.concatenate([
        _head_pad_cols(w_uq, QK_HEAD, 0, QK_HEAD, 0),
        _head_pad_cols(w_uq, QK_HEAD, QK_NOPE, QK_HEAD, QK_NOPE, perm=swap)], axis=1).astype(BF16)
    p['w_ukv'] = jnp.concatenate([
        _head_pad_cols(w_ukv, QK_NOPE + V_HEAD, 0, QK_NOPE, 0),
        _head_pad_cols(w_ukv, QK_NOPE + V_HEAD, QK_NOPE, QK_NOPE + V_HEAD, 0)], axis=1).astype(BF16)

    def gain_rows(g):
        main = jnp.concatenate([g, jnp.zeros((LANE - QK_HEAD,), F32)])
        swp = jnp.concatenate([jnp.zeros((QK_NOPE,), F32), g[QK_NOPE:][swap], jnp.zeros((LANE - QK_HEAD,), F32)])
        return jnp.stack([main, swp])
    kbound = math.sqrt(QK_HEAD) * jnp.max(jnp.abs(k_norm_g))
    qbound = jnp.max(jnp.abs(q_norm_g)) * LOG2E
    p['gq'] = jnp.concatenate([gain_rows(q_norm_g), jnp.full((1, LANE), qbound * kbound, F32)], axis=0)
    p['gk'] = gain_rows(k_norm_g)
    p['fast_softmax'] = 2.0 * qbound * kbound <= FAST_SOFTMAX_MAX_SHIFT
    inv_freq = 1.0 / (ROPE_THETA ** (jnp.arange(half, dtype=F32) / half))
    ang = jnp.arange(max_seq, dtype=F32)[:, None] * inv_freq[None, :]
    cos, sin = jnp.cos(ang), jnp.sin(ang)
    zpad = jnp.zeros((max_seq, LANE - QK_HEAD), F32)
    p['ct'] = jnp.concatenate([jnp.ones((max_seq, QK_NOPE), F32), cos, cos, zpad], axis=1)
    p['st'] = jnp.concatenate([jnp.zeros((max_seq, QK_NOPE), F32), -sin, sin, zpad], axis=1)

    zl = jnp.zeros((DECAY_LORA, RW_WIDTH), F32)
    p['w_dec'] = jnp.concatenate([jnp.concatenate([w_up_f, zl], axis=1),
                                  jnp.concatenate([zl, w_up_b], axis=1)], axis=0).astype(BF16)
    p['w0'] = jnp.concatenate([w0_f, w0_b])[None, :]
    p['a_up'] = jnp.concatenate([a_up, jnp.zeros((LANE - A_LORA, RW_WIDTH), F32)], axis=0).astype(BF16)
    p['a0'] = a0[None, :]
    p['g_up'] = g_up.astype(BF16)
    p['k_k'] = k_k[None, :]
    p['k_a'] = k_a[None, :]
    p['r_k'] = r_k.reshape(1, RW_WIDTH)
    hid = jnp.arange(RW_WIDTH) // RW_HEAD
    p['seg'] = (hid[:, None] == hid[None, :]).astype(BF16)
    ti = jnp.arange(CHUNK)[:, None]
    tj = jnp.arange(CHUNK)[None, :]
    p['sc_tri'] = jnp.stack([tj <= ti, tj >= ti]).astype(BF16)
    ps = (jnp.arange(HGW) % CHUNK)[None, :]
    p['sc_m64'] = jnp.stack([ps < ti, ps <= ti, ps > ti, ps >= ti, ps == ti]).astype(F32)
    blk = jnp.arange(HGW) // RW_HEAD
    same = blk[:, None] == blk[None, :]
    p['sc_bd16'] = same.astype(BF16)
    p['sc_bd32'] = jnp.stack([same, jnp.eye(HGW, dtype=bool)]).astype(F32)
    p['ln_g'] = ln_x_g[None, :]
    p['ln_b'] = ln_x_b[None, :]
    p['w_out'] = w_out.astype(BF16)
    p['g2'] = norm2_g[None, :]
    p['w_r'] = jnp.concatenate([w_router_expert, w_router_group,
                                zcol(d, LANE - N_EXPERTS - N_GROUPS)], axis=1).astype(BF16)
    p['b_r'] = jnp.concatenate([b_router_expert, b_router_group,
                                jnp.zeros((LANE - N_EXPERTS - N_GROUPS,), F32)])[None, :]
    p['wg'] = w_expert_gate.astype(BF16)
    p['wu'] = w_expert_up.astype(BF16)
    p['wd'] = w_expert_down.astype(BF16)
    return p


TM_PROJ = 512
PROJ_SPLIT = 2
TQ = 1024
TK = 8192
TKS = 2048
TQ_MAX = 1024
TK_MAX = 2048
TKS_MAX = 1024
TM_POST = 512
TM_MOE = 512
TM_ROWS = 2048


def _layer(x, p):
    bsz, seq, d = x.shape
    n = bsz * seq
    x2 = x.reshape(n, d)
    q, k, v, r, k2, vv, kkn, kka, lwf, lwb, g, bonus = _proj(x2, seq, p, TM_PROJ)
    oa = _attn(q, k, v, bsz, seq, p['fast_softmax'])
    yf, yb = _rwscan(r, k2, vv, kkn, kka, lwf, lwb, bsz, seq, p)
    x1, hp, gid = _post(x2, oa, yf, yb, bonus, g, p, TM_POST)
    tm = min(TM_MOE, n)
    npt = n // tm + N_GROUPS
    pos, meta = _route_plan(gid, tm, npt)
    xs = _scatter_rows(pos, hp, npt * tm, TM_ROWS)
    ys = _moe_grouped(xs, meta, p, tm, npt)
    out = _gather_add(pos, x1, ys, TM_ROWS)
    return out.reshape(bsz, seq, d)


def kernel(x_prompt, x_sample, norm1_g, w_in, q_a_norm_g, w_uq, kv_a_norm_g, w_ukv, q_norm_g, k_norm_g, mu_shift, w0_f, w_up_f, w0_b, w_up_b, a0, a_up, g_up, k_k, k_a, r_k, ln_x_g, ln_x_b, w_out, norm2_g, w_router_group, b_router_group, w_router_expert, b_router_expert, w_expert_gate, w_expert_up, w_expert_down):
    layer_params = (norm1_g, w_in, q_a_norm_g, w_uq, kv_a_norm_g, w_ukv, q_norm_g, k_norm_g, mu_shift,
                    w0_f, w_up_f, w0_b, w_up_b, a0, a_up, g_up, k_k, k_a, r_k, ln_x_g, ln_x_b, w_out,
                    norm2_g, w_router_group, b_router_group, w_router_expert, b_router_expert,
                    w_expert_gate, w_expert_up, w_expert_down)
    y_prompt, y_sample = x_prompt, x_sample
    max_seq = max(x_prompt.shape[1], x_sample.shape[1])
    for layer in range(norm1_g.shape[0]):
        p = _pack_params([w[layer] for w in layer_params], max_seq)
        y_prompt = _layer(y_prompt, p)
        y_sample = _layer(y_sample, p)
    return (y_prompt, y_sample)
```

```python
import functools
import math

import jax
import jax.numpy as jnp
from jax import lax
from jax.experimental import pallas as pl
from jax.experimental.pallas import tpu as pltpu

D_MODEL = 1024
MLA_HEADS = 8
QK_NOPE = 64
QK_ROPE = 32
QK_HEAD = QK_NOPE + QK_ROPE
V_HEAD = 64
Q_LORA = 384
KV_LORA = 256
ROPE_THETA = 10000.0
RW_HEADS = 8
RW_HEAD = 64
RW_WIDTH = RW_HEADS * RW_HEAD
DECAY_LORA = 64
A_LORA = 64
GATE_LORA = 128
LN_X_EPS = 6.4e-4
MLA_WIDTH = MLA_HEADS * V_HEAD
N_GROUPS = 4
EXPERTS_PER_GROUP = 8
N_EXPERTS = N_GROUPS * EXPERTS_PER_GROUP
D_EXPERT = 256
MOE_EPC = 2
RMS_EPS = 1e-6

LANE = 128
HEAD_PAD = LANE
MLA_PAD = MLA_HEADS * HEAD_PAD
RW_COLS_PAD = 3 * RW_WIDTH + 3 * LANE
MLA_COLS_PAD = Q_LORA + KV_LORA + 2 * LANE
D_IN_PAD = MLA_COLS_PAD + RW_COLS_PAD
CHUNK = 64
HG = 4
HGW = HG * RW_HEAD
VMEM_LIMIT = 56 * 1024 * 1024
LOG2E = 1.4426950408889634
NCH = 8
FAST_SOFTMAX_MAX_SHIFT = 100.0

F32 = jnp.float32
BF16 = jnp.bfloat16


def _dot(a, b):
    return jnp.dot(a, b, preferred_element_type=F32)


def _split_dot(a, b_exact):
    hi = a.astype(BF16)
    lo = (a - hi.astype(F32)).astype(BF16)
    return _dot(hi, b_exact) + _dot(lo, b_exact)


def _cparams(sem):
    return pltpu.CompilerParams(dimension_semantics=sem, vmem_limit_bytes=VMEM_LIMIT)


def _proj_kernel(x_ref, xp_ref, xn_ref, g1_ref, win_ref, qag_ref, wuq_ref, kvag_ref, wukv_ref, gq_ref, gk_ref,
                 ct_ref, st_ref, *rw_refs, tm, seq):
    q_out, k_out, v_out = rw_refs[10:13]
    i = pl.program_id(0)
    x = jnp.concatenate([xp_ref[0], x_ref[...], xn_ref[0]], axis=0)
    h = x * lax.rsqrt(jnp.mean(x * x, axis=-1, keepdims=True) + RMS_EPS) * g1_ref[...]
    ze = _dot(h.astype(BF16), win_ref[...])
    first = (i * tm) % seq == 0
    last = ((i + 1) * tm) % seq == 0

    gq = gq_ref[...]
    gk = gk_ref[...]
    scale = QK_HEAD ** -0.5 * LOG2E
    lane = lax.broadcasted_iota(jnp.int32, (1, LANE), 1)
    ones_col = jnp.where(lane == V_HEAD, 1.0, 0.0).astype(F32)
    shift_col = jnp.where(lane == QK_HEAD, 1.0, 0.0).astype(F32)
    q_shift = shift_col * gq[2:3, 0:1]

    def rows_gen(a, b):
        rs = slice(a, b)
        z = ze[8 + a:8 + b]
        prev_row = ze[7 + a:8 + a, MLA_COLS_PAD:]
        next_row = ze[8 + b:9 + b, MLA_COLS_PAD:]
        if a == 0:
            prev_row = jnp.where(first, 0.0, prev_row)
        if b == tm:
            next_row = jnp.where(last, 0.0, next_row)
        yield from _rw_token_maps(z[:, MLA_COLS_PAD:], prev_row, next_row, rs, *rw_refs[:10], *rw_refs[13:])

        cq = z[:, :Q_LORA]
        cqn = cq * lax.rsqrt(jnp.mean(cq * cq, axis=-1, keepdims=True) + RMS_EPS) * qag_ref[...]
        q2 = _dot(cqn.astype(BF16), wuq_ref[...])
        ckv = z[:, Q_LORA:Q_LORA + KV_LORA]
        ckvn = ckv * lax.rsqrt(jnp.mean(ckv * ckv, axis=-1, keepdims=True) + RMS_EPS) * kvag_ref[...]
        kv2 = _dot(ckvn.astype(BF16), wukv_ref[...])
        yield
        kr_main = z[:, Q_LORA + KV_LORA:Q_LORA + KV_LORA + LANE]
        kr_swap = z[:, Q_LORA + KV_LORA + LANE:MLA_COLS_PAD]
        ct = ct_ref[rs, :]
        st = st_ref[rs, :]
        kr_ssq = jnp.sum(kr_main * kr_main, axis=-1, keepdims=True)
        q_main_f = gq[0:1] * ct
        q_swap_f = gq[1:2] * st
        k_rot = kr_main * (gk[0:1] * ct) + kr_swap * (gk[1:2] * st)
        for hd in range(MLA_HEADS):
            sl = slice(hd * HEAD_PAD, (hd + 1) * HEAD_PAD)
            qm = q2[:, sl]
            qs = q2[:, MLA_PAD + hd * HEAD_PAD:MLA_PAD + (hd + 1) * HEAD_PAD]
            rinv = lax.rsqrt(jnp.sum(qm * qm, axis=-1, keepdims=True) * (1.0 / QK_HEAD) + RMS_EPS)
            qh = (qm * q_main_f + qs * q_swap_f) * (rinv * scale)
            q_out[rs, sl] = (qh - q_shift).astype(BF16)
            kn = kv2[:, sl]
            rinv_k = lax.rsqrt((jnp.sum(kn * kn, axis=-1, keepdims=True) + kr_ssq) * (1.0 / QK_HEAD) + RMS_EPS)
            kh = (kn * gk[0:1] + k_rot) * rinv_k
            k_out[rs, sl] = (kh + shift_col).astype(BF16)
            v_out[rs, sl] = (kv2[:, MLA_PAD + hd * HEAD_PAD:MLA_PAD + (hd + 1) * HEAD_PAD] + ones_col).astype(BF16)

    nsplit = PROJ_SPLIT if tm % (8 * PROJ_SPLIT) == 0 else 1
    rows_per = tm // nsplit
    _interleave([rows_gen(j * rows_per, (j + 1) * rows_per) for j in range(nsplit)])


def _proj(x2, seq, p, tm):
    n = x2.shape[0]
    tm = min(tm, seq)
    nseq_t = seq // tm
    row = lambda i: (i, 0)
    fixed = lambda i: (0, 0)
    pos = lambda i: (i % nseq_t, 0)
    full = lambda a: pl.BlockSpec(a.shape, fixed)
    g8 = tm // 8
    ngrp = n // 8
    x3 = x2.reshape(ngrp, 8, D_MODEL)
    rw_names = ('mu', 'w_dec', 'w0', 'a_up', 'a0', 'g_up', 'k_k', 'k_a', 'r_k', 'seg')
    mla_sds = jax.ShapeDtypeStruct((n, MLA_PAD), BF16)
    rw_sds = jax.ShapeDtypeStruct((n, RW_WIDTH), F32)
    return pl.pallas_call(
        functools.partial(_proj_kernel, tm=tm, seq=seq),
        grid=(n // tm,),
        in_specs=[pl.BlockSpec((tm, D_MODEL), row),
                  pl.BlockSpec((1, 8, D_MODEL), lambda i: (jnp.maximum(i * g8 - 1, 0), 0, 0)),
                  pl.BlockSpec((1, 8, D_MODEL), lambda i: (jnp.minimum((i + 1) * g8, ngrp - 1), 0, 0)),
                  full(p['g1']), full(p['w_in']), full(p['qag']),
                  full(p['w_uq']), full(p['kvag']), full(p['w_ukv']), full(p['gq']), full(p['gk']),
                  pl.BlockSpec((tm, LANE), pos), pl.BlockSpec((tm, LANE), pos)]
                 + [full(p[k]) for k in rw_names],
        out_specs=[pl.BlockSpec((tm, MLA_PAD), row)] * 3 + [pl.BlockSpec((tm, RW_WIDTH), row)] * 9,
        out_shape=[mla_sds] * 3 + [rw_sds] * 9,
        compiler_params=_cparams(("parallel",)),
        name="proj",
    )(x2, x3, x3, p['g1'], p['w_in'], p['qag'], p['w_uq'], p['kvag'], p['w_ukv'], p['gq'], p['gk'],
      p['ct'][:seq], p['st'][:seq], *[p[k] for k in rw_names])


def _attn_kernel(q_ref, k_ref, v_ref, o_ref, m_ref, acc_ref, *, running_max, tks):
    j = pl.program_id(3)

    @pl.when(j == 0)
    def _():
        if running_max:
            m_ref[...] = jnp.full(m_ref.shape, -jnp.inf, F32)
        acc_ref[...] = jnp.zeros(acc_ref.shape, F32)

    heads = [slice(hh * HEAD_PAD, (hh + 1) * HEAD_PAD) for hh in range(2)]
    nsub = k_ref.shape[0] // tks

    def score_pair(sb):
        rows = slice(sb * tks, (sb + 1) * tks)
        return [lax.dot_general(k_ref[rows, sl], q_ref[:, sl], (((1,), (1,)), ((), ())),
                                preferred_element_type=F32) for sl in heads]

    accs = [acc_ref[hh] for hh in range(2)]
    maxes = [m_ref[hh] for hh in range(2)] if running_max else None
    scores = score_pair(0)
    for sb in range(nsub):
        nxt = score_pair(sb + 1) if sb + 1 < nsub else None
        rows = slice(sb * tks, (sb + 1) * tks)
        for hh, sl in enumerate(heads):
            s = scores[hh]
            if running_max:
                m_new = jnp.maximum(maxes[hh], jnp.max(s, axis=0, keepdims=True))
                p = jnp.exp2(s - m_new)
            else:
                p = jnp.exp2(s)
            pv = lax.dot_general(v_ref[rows, sl], p.astype(BF16), (((0,), (0,)), ((), ())),
                                 preferred_element_type=F32)
            if running_max:
                accs[hh] = jnp.exp2(maxes[hh] - m_new) * accs[hh] + pv
                maxes[hh] = m_new
            else:
                accs[hh] = accs[hh] + pv
        scores = nxt
    for hh in range(2):
        acc_ref[hh] = accs[hh]
        if running_max:
            m_ref[hh] = maxes[hh]

    @pl.when(j == pl.num_programs(3) - 1)
    def _():
        outs = []
        for hh in range(2):
            a = acc_ref[hh]
            outs.append(a[:V_HEAD] / a[V_HEAD:V_HEAD + 1])
        o_ref[...] = jnp.concatenate(outs, axis=0).T.astype(BF16)


def _attn(q, k, v, bsz, seq, fast_softmax):
    return lax.cond(fast_softmax,
                    lambda q, k, v: _attn_call(q, k, v, bsz, seq, TQ, TK, TKS, False),
                    lambda q, k, v: _attn_call(q, k, v, bsz, seq, TQ_MAX, TK_MAX, TKS_MAX, True),
                    q, k, v)


def _attn_call(q, k, v, bsz, seq, tq, tk, tks, running_max):
    tq = min(tq, seq)
    tk = min(tk, seq)
    tks = min(tks, tk)
    nq, nk = seq // tq, seq // tk
    n = q.shape[0]
    return pl.pallas_call(
        functools.partial(_attn_kernel, running_max=running_max, tks=tks),
        grid=(bsz, MLA_HEADS // 2, nq, nk),
        in_specs=[pl.BlockSpec((tq, 2 * HEAD_PAD), lambda b, h, i, j: (b * nq + i, h)),
                  pl.BlockSpec((tk, 2 * HEAD_PAD), lambda b, h, i, j: (b * nk + j, h)),
                  pl.BlockSpec((tk, 2 * HEAD_PAD), lambda b, h, i, j: (b * nk + j, h))],
        out_specs=pl.BlockSpec((tq, 2 * V_HEAD), lambda b, h, i, j: (b * nq + i, h)),
        out_shape=jax.ShapeDtypeStruct((n, MLA_WIDTH), BF16),
        scratch_shapes=[pltpu.VMEM((2, 1, tq), F32), pltpu.VMEM((2, HEAD_PAD, tq), F32)],
        compiler_params=_cparams(("parallel", "parallel", "parallel", "arbitrary")),
        name="attn_max" if running_max else "attn",
    )(q, k, v)


def _rw_token_maps(u, prev_row, next_row, rs, mu_ref, wdec_ref, w0_ref, aup_ref, a0_ref, gup_ref,
                   kk_ref, ka_ref, rk_ref, seg_ref,
                   r_out, k_out, v_out, kkn_out, kka_out, lwf_out, lwb_out, g_out, bonus_out):
    tm = u.shape[0]
    sub = lax.broadcasted_iota(jnp.int32, (8, 1), 0)
    prev = pltpu.roll(u, 1, 0)
    prev = jnp.concatenate([jnp.where(sub == 0, prev_row, prev[:8]), prev[8:]], axis=0)
    nxt = pltpu.roll(u, tm - 1, 0)
    nxt = jnp.concatenate([nxt[:tm - 8], jnp.where(sub == 7, next_row, nxt[tm - 8:])], axis=0)
    us = u + (0.5 * (prev + nxt) - u) * mu_ref[...]

    r = us[:, :RW_WIDTH]
    k = us[:, RW_WIDTH:2 * RW_WIDTH]
    v = us[:, 2 * RW_WIDTH:3 * RW_WIDTH]
    xw = us[:, 3 * RW_WIDTH:3 * RW_WIDTH + LANE]
    xa = us[:, 3 * RW_WIDTH + LANE:3 * RW_WIDTH + 2 * LANE]
    xg = us[:, 3 * RW_WIDTH + 2 * LANE:]
    r_out[rs, :] = r
    v_out[rs, :] = v

    wdot = _dot(jnp.tanh(xw).astype(BF16), wdec_ref[...])
    adot = _dot(xa.astype(BF16), aup_ref[...])
    gdot = _dot(jax.nn.sigmoid(xg).astype(BF16), gup_ref[...])
    seg = seg_ref[...]
    kk = k * kk_ref[...]
    kss = _split_dot(kk * kk, seg)
    yield
    lw = jax.nn.sigmoid(w0_ref[...] + wdot) * (-math.exp(-0.5))
    lwf_out[rs, :] = lw[:, :RW_WIDTH]
    lwb_out[rs, :] = lw[:, RW_WIDTH:]
    a = jax.nn.sigmoid(a0_ref[...] + adot)
    g_out[rs, :] = gdot
    kkn = kk * lax.rsqrt(jnp.maximum(kss, 1e-24))
    k2 = k * (1.0 + (a - 1.0) * ka_ref[...])
    k_out[rs, :] = k2
    kkn_out[rs, :] = kkn
    kka_out[rs, :] = kkn * a
    bsum = _split_dot(r * k2 * rk_ref[...], seg)
    yield
    bonus_out[rs, :] = bsum * v


def _bd(x, bd16):
    return jnp.concatenate([x.astype(BF16)] * HG, axis=0) * bd16


def _chunk_intra(r, k, v, kkn, kka, lw, reverse, consts):
    tri, strict_m, incl_m, eye_p, bd16, bd32, eye32 = consts
    lg = _split_dot_left(tri, lw)
    yield
    lgx = lg - lw
    tot = lg[0:1] if reverse else lg[CHUNK - 1:CHUNK]
    gi = jnp.exp(lg)
    ginv = jnp.exp(-lg)
    gend = jnp.exp(tot - lg)
    at = -kkn * jnp.exp(lgx)
    rt = r * gi
    bt = kka * ginv
    kt = k * ginv
    bh = kka * gend
    kh = k * gend

    lhs = jnp.concatenate([at, rt], axis=0).astype(BF16)
    rhs = jnp.concatenate([_bd(bt, bd16), _bd(kt, bd16)], axis=0)
    a_all = lax.dot_general(lhs, rhs, (((1,), (1,)), ((), ())), preferred_element_type=F32)
    yield
    n_ab = a_all[:CHUNK, :HGW] * strict_m
    a_ak = a_all[:CHUNK, HGW:] * strict_m
    a_rb = a_all[CHUNK:, :HGW] * incl_m
    a_rk = a_all[CHUNK:, HGW:] * incl_m

    t = eye_p + n_ab
    nk = _dot(n_ab.astype(BF16), _bd(n_ab, bd16))
    yield
    for _ in range(4):
        both = _dot(jnp.concatenate([nk, t], axis=0).astype(BF16), _bd(nk, bd16))
        yield
        nk = both[:CHUNK]
        t = t + both[CHUNK:]
    tn = _dot(t.astype(BF16), _bd(nk, bd16))
    yield
    t = t + tn

    tb = t.astype(BF16)
    vbd = _bd(v, bd16)
    w = _dot(tb, _bd(at, bd16))
    akv = _dot(a_ak.astype(BF16), vbd)
    yield
    uv = _dot(tb, _bd(akv, bd16))
    arb = a_rb.astype(BF16)
    qpd = _dot(arb, _bd(w, bd16))
    yield
    y_in = _dot(jnp.concatenate([arb, a_rk.astype(BF16)], axis=1),
                jnp.concatenate([_bd(uv, bd16), vbd], axis=0))
    lhs_t = jnp.concatenate([bh, kh], axis=0).astype(BF16)
    rhs_t = jnp.concatenate([jnp.concatenate([w, uv], axis=1),
                             jnp.concatenate([jnp.zeros_like(v), v], axis=1)], axis=0).astype(BF16)
    mc = lax.dot_general(lhs_t, rhs_t, (((0,), (0,)), ((), ())), preferred_element_type=F32)
    yield
    m_bd = mc[:, :HGW] * bd32 + eye32 * jnp.exp(tot)
    c_bd = mc[:, HGW:] * bd32
    return y_in, (rt + qpd).astype(BF16), m_bd.astype(BF16), c_bd


def _interleave(gens):
    results = [None] * len(gens)
    active = list(range(len(gens)))
    while active:
        still = []
        for i in active:
            try:
                next(gens[i])
                still.append(i)
            except StopIteration as stop:
                results[i] = stop.value
        active = still
    return results


def _split_dot_left(b_exact, a):
    hi = a.astype(BF16)
    lo = (a - hi.astype(F32)).astype(BF16)
    return _dot(b_exact, hi) + _dot(b_exact, lo)


def _rwscan_kernel(rf, kf, vf, nf, af, lf, rb, kb, vb, nb, ab, lb, tri_ref, m64_ref, bd16_ref, bd32_ref,
                   yf_out, yb_out, s_ref):
    c = pl.program_id(1)

    @pl.when(c == 0)
    def _():
        s_ref[...] = jnp.zeros(s_ref.shape, F32)

    nch = rf.shape[0] // CHUNK
    bd16 = bd16_ref[...]
    bd32 = bd32_ref[0]
    eye32 = bd32_ref[1]
    eye_p = m64_ref[4]
    ngrp = RW_HEADS // HG
    chains = []
    gens = []
    for d, refs, y_out in ((0, (rf, kf, vf, nf, af, lf), yf_out), (1, (rb, kb, vb, nb, ab, lb), yb_out)):
        reverse = d == 1
        consts = (tri_ref[d], m64_ref[2 * d], m64_ref[2 * d + 1], eye_p, bd16, bd32, eye32)
        order = list(range(nch - 1, -1, -1) if reverse else range(nch))
        for g in range(ngrp):
            chains.append((d, g, y_out, order))
            for j in order:
                ops = [x[j * CHUNK:(j + 1) * CHUNK, g * HGW:(g + 1) * HGW] for x in refs]
                gens.append(_chunk_intra(*ops, reverse, consts))
    parts = _interleave(gens)

    states = [s_ref[d, g] for d, g, _, _ in chains]
    for step in range(nch):
        for ci, (d, g, y_out, order) in enumerate(chains):
            j = order[step]
            y_in, qp, m_bd, c_bd = parts[ci * nch + step]
            sb = states[ci].astype(BF16)
            y_out[j * CHUNK:(j + 1) * CHUNK, g * HGW:(g + 1) * HGW] = y_in + _dot(qp, sb)
            states[ci] = _dot(m_bd, sb) + c_bd
    for ci, (d, g, _, _) in enumerate(chains):
        s_ref[d, g] = states[ci]


def _rwscan(r, k, v, kkn, kka, lwf, lwb, bsz, seq, p):
    n = r.shape[0]
    rows = min(NCH * CHUNK, seq)
    nb = seq // rows
    fwd = pl.BlockSpec((rows, RW_WIDTH), lambda b, c: (b * nb + c, 0))
    bwd = pl.BlockSpec((rows, RW_WIDTH), lambda b, c: (b * nb + nb - 1 - c, 0))
    full = lambda a: pl.BlockSpec(a.shape, lambda b, c: (0,) * a.ndim)
    sds = jax.ShapeDtypeStruct((n, RW_WIDTH), F32)
    return pl.pallas_call(
        _rwscan_kernel,
        grid=(bsz, nb),
        in_specs=[fwd] * 6 + [bwd] * 6 + [full(p['sc_tri']), full(p['sc_m64']), full(p['sc_bd16']),
                                           full(p['sc_bd32'])],
        out_specs=[fwd, bwd],
        out_shape=[sds, sds],
        scratch_shapes=[pltpu.VMEM((2, RW_HEADS // HG, HGW, HGW), F32)],
        compiler_params=_cparams(("parallel", "arbitrary")),
        name="rwscan",
    )(r, k, v, kkn, kka, lwf, r, k, v, kkn, kka, lwb, p['sc_tri'], p['sc_m64'], p['sc_bd16'], p['sc_bd32'])


def _post_kernel(x_ref, oa_ref, yf_ref, yb_ref, bonus_ref, g_ref, lng_ref, lnb_ref, seg_ref, wout_ref,
                 g2_ref, wr_ref, br_ref, x1_out, hp_out, gid_out):
    tm = x_ref.shape[0]
    nsplit = 2 if tm % 16 == 0 else 1
    rows_per = tm // nsplit

    def rows_gen(rs):
        seg = seg_ref[...]
        oa_part = _dot(oa_ref[rs, :], wout_ref[:MLA_WIDTH, :])
        y = yf_ref[rs, :] + yb_ref[rs, :]
        mean = _split_dot(y, seg) * (1.0 / RW_HEAD)
        yield
        dlt = y - mean
        var = _split_dot(dlt * dlt, seg) * (1.0 / RW_HEAD)
        yield
        yn = dlt * lax.rsqrt(var + LN_X_EPS) * lng_ref[...] + lnb_ref[...]
        ob = (yn + bonus_ref[rs, :]) * g_ref[rs, :]
        x1 = x_ref[rs, :] + oa_part + _dot(ob.astype(BF16), wout_ref[MLA_WIDTH:, :])
        yield
        x1_out[rs, :] = x1
        h2 = x1 * lax.rsqrt(jnp.mean(x1 * x1, axis=-1, keepdims=True) + RMS_EPS) * g2_ref[...]
        h2b = h2.astype(BF16)
        bits = lax.bitcast_convert_type(h2b.astype(F32), jnp.uint32)
        hp_out[rs, :] = (bits[:, :D_MODEL // 2] >> 16) | (bits[:, D_MODEL // 2:] & jnp.uint32(0xFFFF0000))
        logits = _dot(h2b, wr_ref[...]) + br_ref[...]
        yield
        _, gidx, _ = _group_choice(logits)
        gid_out[rs, :] = jnp.broadcast_to(gidx, (rows_per, LANE))

    _interleave([rows_gen(slice(j * rows_per, (j + 1) * rows_per)) for j in range(nsplit)])


def _group_choice(logits):
    lane_i = lax.broadcasted_iota(jnp.int32, logits.shape, 1)
    lane = lane_i.astype(F32)
    is_g = (lane_i >= N_EXPERTS) & (lane_i < N_EXPERTS + N_GROUPS)
    gl = jnp.where(is_g, logits, -jnp.inf)
    gmax = jnp.max(gl, axis=-1, keepdims=True)
    gidx = jnp.min(jnp.where(gl == gmax, lane, 1e9), axis=-1, keepdims=True) - N_EXPERTS
    p_g = 1.0 / jnp.sum(jnp.where(is_g, jnp.exp(gl - gmax), 0.0), axis=-1, keepdims=True)
    return p_g, gidx, lane


def _post(x2, oa, yf, yb, bonus, g, p, tm):
    n = x2.shape[0]
    tm = min(tm, n)
    row = lambda i: (i, 0)
    fixed = lambda i: (0, 0)
    full = lambda a: pl.BlockSpec(a.shape, fixed)
    rw = pl.BlockSpec((tm, RW_WIDTH), row)
    return pl.pallas_call(
        _post_kernel,
        grid=(n // tm,),
        in_specs=[pl.BlockSpec((tm, D_MODEL), row), pl.BlockSpec((tm, MLA_WIDTH), row), rw, rw, rw, rw,
                  full(p['ln_g']), full(p['ln_b']), full(p['seg']), full(p['w_out']), full(p['g2']),
                  full(p['w_r']), full(p['b_r'])],
        out_specs=[pl.BlockSpec((tm, D_MODEL), row), pl.BlockSpec((tm, D_MODEL // 2), row),
                   pl.BlockSpec((tm, LANE), row)],
        out_shape=[jax.ShapeDtypeStruct((n, D_MODEL), F32), jax.ShapeDtypeStruct((n, D_MODEL // 2), jnp.uint32),
                   jax.ShapeDtypeStruct((n, LANE), F32)],
        compiler_params=_cparams(("parallel",)),
        name="post",
    )(x2, oa, yf, yb, bonus, g, p['ln_g'], p['ln_b'], p['seg'], p['w_out'], p['g2'], p['w_r'], p['b_r'])


def _route_plan(gid, tm, npt):
    g = gid[:, 0].astype(jnp.int32)
    onehot = (g[:, None] == jnp.arange(N_GROUPS, dtype=jnp.int32)[None, :]).astype(jnp.int32)
    csum = jnp.cumsum(onehot, axis=0)
    rank = jnp.sum(csum * onehot, axis=1) - 1
    ntile = (csum[-1] + tm - 1) // tm
    tile_end = jnp.cumsum(ntile)
    off = (tile_end - ntile) * tm
    pos = jnp.sum(onehot * off[None, :], axis=1) + rank
    n_active = tile_end[-1]
    t = jnp.minimum(jnp.arange(npt, dtype=jnp.int32), n_active - 1)
    tile_gid = jnp.sum((t[:, None] >= tile_end[None, :]).astype(jnp.int32), axis=1)
    meta = jnp.concatenate([n_active[None], tile_gid]).astype(jnp.int32)
    return pos.astype(jnp.int32), meta


def _row_dma_loops(n_rows, make_copy, all_rows_copy):
    def issue(h, carry):
        make_copy(2 * h).start(priority=0)
        make_copy(2 * h + 1).start(priority=1)
        return carry
    lax.fori_loop(0, n_rows // 2, issue, 0, unroll=4)
    all_rows_copy.wait()


def _scatter_kernel(pos_ref, src_ref, init_ref, dst_ref, sem):
    n_rows = src_ref.shape[0]
    _row_dma_loops(n_rows, lambda r: pltpu.make_async_copy(
        src_ref.at[pl.ds(r, 1)], dst_ref.at[pl.ds(pos_ref[0, 0, r], 1)], sem),
        pltpu.make_async_copy(src_ref, dst_ref.at[pl.ds(0, n_rows)], sem))


def _scatter_rows(pos, src, n_dst, tms):
    n, w = src.shape
    tms = min(tms, n)
    return pl.pallas_call(
        _scatter_kernel,
        grid=(n // tms,),
        in_specs=[pl.BlockSpec((1, 1, tms), lambda i: (i, 0, 0), memory_space=pltpu.SMEM),
                  pl.BlockSpec((tms, w), lambda i: (i, 0)), pl.BlockSpec(memory_space=pl.ANY)],
        out_specs=pl.BlockSpec(memory_space=pl.ANY),
        out_shape=jax.ShapeDtypeStruct((n_dst, w), src.dtype),
        scratch_shapes=[pltpu.SemaphoreType.DMA(())],
        input_output_aliases={2: 0},
        compiler_params=_cparams(("arbitrary",)),
        name="moe_scatter",
    )(pos.reshape(n // tms, 1, tms), src, jnp.zeros((n_dst, w), src.dtype))


def _moe_group_kernel(meta_ref, xs_ref, wr_ref, br_ref, wg_ref, wu_ref, wd_ref, ys_ref):
    i = pl.program_id(0)

    @pl.when(i >= meta_ref[0])
    def _():
        ys_ref[...] = jnp.zeros(ys_ref.shape, F32)

    @pl.when(i < meta_ref[0])
    def _():
        tm = xs_ref.shape[0]
        word = xs_ref[...]
        h2 = jnp.concatenate([lax.bitcast_convert_type(word << 16, F32),
                              lax.bitcast_convert_type(word & jnp.uint32(0xFFFF0000), F32)],
                             axis=1).astype(BF16)
        def up(c):
            ks = range(c * MOE_EPC, (c + 1) * MOE_EPC)
            return (jnp.concatenate([_dot(h2, wg_ref[k]) for k in ks], axis=1),
                    jnp.concatenate([_dot(h2, wu_ref[k]) for k in ks], axis=1))

        pend = up(0)
        logits = _dot(h2, wr_ref[...]) + br_ref[...]
        p_g, _, lane = _group_choice(logits)
        first = (meta_ref[1 + i] * EXPERTS_PER_GROUP).astype(F32)
        in_grp = (lane >= first) & (lane < first + EXPERTS_PER_GROUP)
        el = jnp.where(in_grp, logits, -jnp.inf)
        v1 = jnp.max(el, axis=-1, keepdims=True)
        i1 = jnp.min(jnp.where(el == v1, lane, 1e9), axis=-1, keepdims=True)
        el2 = jnp.where(lane == i1, -jnp.inf, el)
        v2 = jnp.max(el2, axis=-1, keepdims=True)
        i2 = jnp.min(jnp.where(el2 == v2, lane, 1e9), axis=-1, keepdims=True)
        e2 = jnp.exp(v2 - v1)
        den = 1.0 + e2
        w1 = (1.0 / den) * p_g
        w2 = (e2 / den) * p_g

        acc = None
        nchunk = EXPERTS_PER_GROUP // MOE_EPC
        for c in range(nchunk):
            nxt = up(c + 1) if c + 1 < nchunk else None
            hg, hu = pend
            gate = jnp.concatenate(
                [jnp.broadcast_to(jnp.where(i1 == first + k, w1, 0.0) + jnp.where(i2 == first + k, w2, 0.0),
                                  (tm, D_EXPERT)) for k in range(c * MOE_EPC, (c + 1) * MOE_EPC)], axis=1)
            act = hg * jax.nn.sigmoid(hg) * hu * gate
            wd = wd_ref[c * MOE_EPC:(c + 1) * MOE_EPC].reshape(MOE_EPC * D_EXPERT, D_MODEL)
            part = _dot(act.astype(BF16), wd)
            acc = part if acc is None else acc + part
            pend = nxt
        ys_ref[...] = acc


def _moe_grouped(xs, meta, p, tm, npt):
    def tile(i, m):
        return (jnp.minimum(i, m[0] - 1), 0)

    def group(i, m):
        return (m[1 + i], 0, 0)

    grid_spec = pltpu.PrefetchScalarGridSpec(
        num_scalar_prefetch=1,
        grid=(npt,),
        in_specs=[pl.BlockSpec((tm, D_MODEL // 2), tile),
                  pl.BlockSpec(p['w_r'].shape, lambda i, m: (0, 0)),
                  pl.BlockSpec(p['b_r'].shape, lambda i, m: (0, 0)),
                  pl.BlockSpec((EXPERTS_PER_GROUP, D_MODEL, D_EXPERT), group),
                  pl.BlockSpec((EXPERTS_PER_GROUP, D_MODEL, D_EXPERT), group),
                  pl.BlockSpec((EXPERTS_PER_GROUP, D_EXPERT, D_MODEL), group)],
        out_specs=pl.BlockSpec((tm, D_MODEL), lambda i, m: (i, 0)),
    )
    return pl.pallas_call(
        _moe_group_kernel,
        grid_spec=grid_spec,
        out_shape=jax.ShapeDtypeStruct((npt * tm, D_MODEL), F32),
        compiler_params=_cparams(("arbitrary",)),
        name="moe_group",
    )(meta, xs, p['w_r'], p['b_r'], p['wg'], p['wu'], p['wd'])


def _gather_kernel(pos_ref, x1_ref, ys_ref, o_ref, buf, sem):
    n_rows = buf.shape[0]
    _row_dma_loops(n_rows, lambda r: pltpu.make_async_copy(
        ys_ref.at[pl.ds(pos_ref[0, 0, r], 1)], buf.at[pl.ds(r, 1)], sem),
        pltpu.make_async_copy(ys_ref.at[pl.ds(0, n_rows)], buf, sem))
    o_ref[...] = x1_ref[...] + buf[...]


def _gather_add(pos, x1, ys, tms):
    n, d = x1.shape
    tms = min(tms, n)
    return pl.pallas_call(
        _gather_kernel,
        grid=(n // tms,),
        in_specs=[pl.BlockSpec((1, 1, tms), lambda i: (i, 0, 0), memory_space=pltpu.SMEM),
                  pl.BlockSpec((tms, d), lambda i: (i, 0)),
                  pl.BlockSpec(memory_space=pl.ANY)],
        out_specs=pl.BlockSpec((tms, d), lambda i: (i, 0)),
        out_shape=jax.ShapeDtypeStruct((n, d), F32),
        scratch_shapes=[pltpu.VMEM((tms, d), F32), pltpu.SemaphoreType.DMA(())],
        compiler_params=_cparams(("arbitrary",)),
        name="moe_gather",
    )(pos.reshape(n // tms, 1, tms), x1, ys)


def _head_pad_cols(w, head_w, lo, hi, dst, width=HEAD_PAD, perm=None):
    rows = w.shape[0]
    wh = w.reshape(rows, -1, head_w)[:, :, lo:hi]
    if perm is not None:
        wh = wh[:, :, perm]
    out = jnp.zeros((rows, wh.shape[1], width), w.dtype)
    return out.at[:, :, dst:dst + (hi - lo)].set(wh).reshape(rows, -1)


def _pack_params(lp, max_seq):
    (norm1_g, w_in, q_a_norm_g, w_uq, kv_a_norm_g, w_ukv, q_norm_g, k_norm_g, mu_shift, w0_f, w_up_f,
     w0_b, w_up_b, a0, a_up, g_up, k_k, k_a, r_k, ln_x_g, ln_x_b, w_out, norm2_g, w_router_group,
     b_router_group, w_router_expert, b_router_expert, w_expert_gate, w_expert_up, w_expert_down) = lp
    half = QK_ROPE // 2
    swap = jnp.concatenate([jnp.arange(half, QK_ROPE), jnp.arange(half)])
    zcol = lambda rows, n: jnp.zeros((rows, n), F32)
    p = {}
    c0 = Q_LORA + KV_LORA
    w_kr = w_in[:, c0:c0 + QK_ROPE]
    u0 = c0 + QK_ROPE
    wu_ = w_in[:, u0:]
    d = D_MODEL
    w_in_p = jnp.concatenate([
        w_in[:, :c0],
        zcol(d, QK_NOPE), w_kr, zcol(d, LANE - QK_HEAD),
        zcol(d, QK_NOPE), w_kr[:, swap], zcol(d, LANE - QK_HEAD),
        wu_[:, :3 * RW_WIDTH + 2 * DECAY_LORA + A_LORA], zcol(d, LANE - A_LORA),
        wu_[:, 3 * RW_WIDTH + 2 * DECAY_LORA + A_LORA:]], axis=1)
    p['w_in'] = w_in_p.astype(BF16)
    mu = mu_shift[None, :]
    p['mu'] = jnp.concatenate([mu[:, :3 * RW_WIDTH + 2 * DECAY_LORA + A_LORA], zcol(1, LANE - A_LORA),
                               mu[:, 3 * RW_WIDTH + 2 * DECAY_LORA + A_LORA:]], axis=1)
    p['g1'] = norm1_g[None, :]
    p['qag'] = q_a_norm_g[None, :]
    p['kvag'] = kv_a_norm_g[None, :]
    p['w_uq'] = jnp.concatenate([
        _head_pad_cols(w_uq, QK_HEAD, 0, QK_HEAD, 0),
        _head_pad_cols(w_uq, QK_HEAD, QK_NOPE, QK_HEAD, QK_NOPE, perm=swap)], axis=1).astype(BF16)
    p['w_ukv'] = jnp.concatenate([
        _head_pad_cols(w_ukv, QK_NOPE + V_HEAD, 0, QK_NOPE, 0),
        _head_pad_cols(w_ukv, QK_NOPE + V_HEAD, QK_NOPE, QK_NOPE + V_HEAD, 0)], axis=1).astype(BF16)

    def gain_rows(g):
        main = jnp.concatenate([g, jnp.zeros((LANE - QK_HEAD,), F32)])
        swp = jnp.concatenate([jnp.zeros((QK_NOPE,), F32), g[QK_NOPE:][swap], jnp.zeros((LANE - QK_HEAD,), F32)])
        return jnp.stack([main, swp])
    kbound = math.sqrt(QK_HEAD) * jnp.max(jnp.abs(k_norm_g))
    qbound = jnp.max(jnp.abs(q_norm_g)) * LOG2E
    p['gq'] = jnp.concatenate([gain_rows(q_norm_g), jnp.full((1, LANE), qbound * kbound, F32)], axis=0)
    p['gk'] = gain_rows(k_norm_g)
    p['fast_softmax'] = 2.0 * qbound * kbound <= FAST_SOFTMAX_MAX_SHIFT
    inv_freq = 1.0 / (ROPE_THETA ** (jnp.arange(half, dtype=F32) / half))
    ang = jnp.arange(max_seq, dtype=F32)[:, None] * inv_freq[None, :]
    cos, sin = jnp.cos(ang), jnp.sin(ang)
    zpad = jnp.zeros((max_seq, LANE - QK_HEAD), F32)
    p['ct'] = jnp.concatenate([jnp.ones((max_seq, QK_NOPE), F32), cos, cos, zpad], axis=1)
    p['st'] = jnp.concatenate([jnp.zeros((max_seq, QK_NOPE), F32), -sin, sin, zpad], axis=1)

    zl = jnp.zeros((DECAY_LORA, RW_WIDTH), F32)
    p['w_dec'] = jnp.concatenate([jnp.concatenate([w_up_f, zl], axis=1),
                                  jnp.concatenate([zl, w_up_b], axis=1)], axis=0).astype(BF16)
    p['w0'] = jnp.concatenate([w0_f, w0_b])[None, :]
    p['a_up'] = jnp.concatenate([a_up, jnp.zeros((LANE - A_LORA, RW_WIDTH), F32)], axis=0).astype(BF16)
    p['a0'] = a0[None, :]
    p['g_up'] = g_up.astype(BF16)
    p['k_k'] = k_k[None, :]
    p['k_a'] = k_a[None, :]
    p['r_k'] = r_k.reshape(1, RW_WIDTH)
    hid = jnp.arange(RW_WIDTH) // RW_HEAD
    p['seg'] = (hid[:, None] == hid[None, :]).astype(BF16)
    ti = jnp.arange(CHUNK)[:, None]
    tj = jnp.arange(CHUNK)[None, :]
    p['sc_tri'] = jnp.stack([tj <= ti, tj >= ti]).astype(BF16)
    ps = (jnp.arange(HGW) % CHUNK)[None, :]
    p['sc_m64'] = jnp.stack([ps < ti, ps <= ti, ps > ti, ps >= ti, ps == ti]).astype(F32)
    blk = jnp.arange(HGW) // RW_HEAD
    same = blk[:, None] == blk[None, :]
    p['sc_bd16'] = same.astype(BF16)
    p['sc_bd32'] = jnp.stack([same, jnp.eye(HGW, dtype=bool)]).astype(F32)
    p['ln_g'] = ln_x_g[None, :]
    p['ln_b'] = ln_x_b[None, :]
    p['w_out'] = w_out.astype(BF16)
    p['g2'] = norm2_g[None, :]
    p['w_r'] = jnp.concatenate([w_router_expert, w_router_group,
                                zcol(d, LANE - N_EXPERTS - N_GROUPS)], axis=1).astype(BF16)
    p['b_r'] = jnp.concatenate([b_router_expert, b_router_group,
                                jnp.zeros((LANE - N_EXPERTS - N_GROUPS,), F32)])[None, :]
    p['wg'] = w_expert_gate.astype(BF16)
    p['wu'] = w_expert_up.astype(BF16)
    p['wd'] = w_expert_down.astype(BF16)
    return p


TM_PROJ = 512
PROJ_SPLIT = 2
TQ = 1024
TK = 8192
TKS = 2048
TQ_MAX = 1024
TK_MAX = 2048
TKS_MAX = 1024
TM_POST = 512
TM_MOE = 512
TM_ROWS = 1024


def _layer(x, p):
    bsz, seq, d = x.shape
    n = bsz * seq
    x2 = x.reshape(n, d)
    q, k, v, r, k2, vv, kkn, kka, lwf, lwb, g, bonus = _proj(x2, seq, p, TM_PROJ)
    oa = _attn(q, k, v, bsz, seq, p['fast_softmax'])
    yf, yb = _rwscan(r, k2, vv, kkn, kka, lwf, lwb, bsz, seq, p)
    x1, hp, gid = _post(x2, oa, yf, yb, bonus, g, p, TM_POST)
    tm = min(TM_MOE, n)
    npt = n // tm + N_GROUPS
    pos, meta = _route_plan(gid, tm, npt)
    xs = _scatter_rows(pos, hp, npt * tm, TM_ROWS)
    ys = _moe_grouped(xs, meta, p, tm, npt)
    out = _gather_add(pos, x1, ys, TM_ROWS)
    return out.reshape(bsz, seq, d)


def kernel(x_prompt, x_sample, norm1_g, w_in, q_a_norm_g, w_uq, kv_a_norm_g, w_ukv, q_norm_g, k_norm_g, mu_shift, w0_f, w_up_f, w0_b, w_up_b, a0, a_up, g_up, k_k, k_a, r_k, ln_x_g, ln_x_b, w_out, norm2_g, w_router_group, b_router_group, w_router_expert, b_router_expert, w_expert_gate, w_expert_up, w_expert_down):
    layer_params = (norm1_g, w_in, q_a_norm_g, w_uq, kv_a_norm_g, w_ukv, q_norm_g, k_norm_g, mu_shift,
                    w0_f, w_up_f, w0_b, w_up_b, a0, a_up, g_up, k_k, k_a, r_k, ln_x_g, ln_x_b, w_out,
                    norm2_g, w_router_group, b_router_group, w_router_expert, b_router_expert,
                    w_expert_gate, w_expert_up, w_expert_down)
    y_prompt, y_sample = x_prompt, x_sample
    max_seq = max(x_prompt.shape[1], x_sample.shape[1])
    for layer in range(norm1_g.shape[0]):
        p = _pack_params([w[layer] for w in layer_params], max_seq)
        y_prompt = _layer(y_prompt, p)
        y_sample = _layer(y_sample, p)
    return (y_prompt, y_sample)
```

```python
import functools
import math

import jax
import jax.numpy as jnp
from jax import lax
from jax.experimental import pallas as pl
from jax.experimental.pallas import tpu as pltpu

D_MODEL = 1024
MLA_HEADS = 8
QK_NOPE = 64
QK_ROPE = 32
QK_HEAD = QK_NOPE + QK_ROPE
V_HEAD = 64
Q_LORA = 384
KV_LORA = 256
ROPE_THETA = 10000.0
RW_HEADS = 8
RW_HEAD = 64
RW_WIDTH = RW_HEADS * RW_HEAD
DECAY_LORA = 64
A_LORA = 64
GATE_LORA = 128
LN_X_EPS = 6.4e-4
MLA_WIDTH = MLA_HEADS * V_HEAD
N_GROUPS = 4
EXPERTS_PER_GROUP = 8
N_EXPERTS = N_GROUPS * EXPERTS_PER_GROUP
D_EXPERT = 256
MOE_EPC = 2
RMS_EPS = 1e-6

LANE = 128
HEAD_PAD = LANE
MLA_PAD = MLA_HEADS * HEAD_PAD
RW_COLS_PAD = 3 * RW_WIDTH + 3 * LANE
MLA_COLS_PAD = Q_LORA + KV_LORA + 2 * LANE
D_IN_PAD = MLA_COLS_PAD + RW_COLS_PAD
CHUNK = 64
HG = 4
HGW = HG * RW_HEAD
VMEM_LIMIT = 56 * 1024 * 1024
LOG2E = 1.4426950408889634
NCH = 8
FAST_SOFTMAX_MAX_SHIFT = 100.0

F32 = jnp.float32
BF16 = jnp.bfloat16


def _dot(a, b):
    return jnp.dot(a, b, preferred_element_type=F32)


def _split_dot(a, b_exact):
    hi = a.astype(BF16)
    lo = (a - hi.astype(F32)).astype(BF16)
    return _dot(hi, b_exact) + _dot(lo, b_exact)


def _cparams(sem):
    return pltpu.CompilerParams(dimension_semantics=sem, vmem_limit_bytes=VMEM_LIMIT)


def _proj_kernel(x_ref, xp_ref, xn_ref, g1_ref, win_ref, qag_ref, wuq_ref, kvag_ref, wukv_ref, gq_ref, gk_ref,
                 ct_ref, st_ref, *rw_refs, tm, seq):
    q_out, k_out, v_out = rw_refs[10:13]
    i = pl.program_id(0)
    x = jnp.concatenate([xp_ref[0], x_ref[...], xn_ref[0]], axis=0)
    h = x * lax.rsqrt(jnp.mean(x * x, axis=-1, keepdims=True) + RMS_EPS) * g1_ref[...]
    ze = _dot(h.astype(BF16), win_ref[...])
    first = (i * tm) % seq == 0
    last = ((i + 1) * tm) % seq == 0

    gq = gq_ref[...]
    gk = gk_ref[...]
    scale = QK_HEAD ** -0.5 * LOG2E
    lane = lax.broadcasted_iota(jnp.int32, (1, LANE), 1)
    ones_col = jnp.where(lane == V_HEAD, 1.0, 0.0).astype(F32)
    shift_col = jnp.where(lane == QK_HEAD, 1.0, 0.0).astype(F32)
    q_shift = shift_col * gq[2:3, 0:1]

    def rows_gen(a, b):
        rs = slice(a, b)
        z = ze[8 + a:8 + b]
        prev_row = ze[7 + a:8 + a, MLA_COLS_PAD:]
        next_row = ze[8 + b:9 + b, MLA_COLS_PAD:]
        if a == 0:
            prev_row = jnp.where(first, 0.0, prev_row)
        if b == tm:
            next_row = jnp.where(last, 0.0, next_row)
        yield from _rw_token_maps(z[:, MLA_COLS_PAD:], prev_row, next_row, rs, *rw_refs[:10], *rw_refs[13:])

        cq = z[:, :Q_LORA]
        cqn = cq * lax.rsqrt(jnp.mean(cq * cq, axis=-1, keepdims=True) + RMS_EPS) * qag_ref[...]
        q2 = _dot(cqn.astype(BF16), wuq_ref[...])
        ckv = z[:, Q_LORA:Q_LORA + KV_LORA]
        ckvn = ckv * lax.rsqrt(jnp.mean(ckv * ckv, axis=-1, keepdims=True) + RMS_EPS) * kvag_ref[...]
        kv2 = _dot(ckvn.astype(BF16), wukv_ref[...])
        yield
        kr_main = z[:, Q_LORA + KV_LORA:Q_LORA + KV_LORA + LANE]
        kr_swap = z[:, Q_LORA + KV_LORA + LANE:MLA_COLS_PAD]
        ct = ct_ref[rs, :]
        st = st_ref[rs, :]
        kr_ssq = jnp.sum(kr_main * kr_main, axis=-1, keepdims=True)
        q_main_f = gq[0:1] * ct
        q_swap_f = gq[1:2] * st
        k_rot = kr_main * (gk[0:1] * ct) + kr_swap * (gk[1:2] * st)
        for hd in range(MLA_HEADS):
            sl = slice(hd * HEAD_PAD, (hd + 1) * HEAD_PAD)
            qm = q2[:, sl]
            qs = q2[:, MLA_PAD + hd * HEAD_PAD:MLA_PAD + (hd + 1) * HEAD_PAD]
            rinv = lax.rsqrt(jnp.sum(qm * qm, axis=-1, keepdims=True) * (1.0 / QK_HEAD) + RMS_EPS)
            qh = (qm * q_main_f + qs * q_swap_f) * (rinv * scale)
            q_out[rs, sl] = (qh - q_shift).astype(BF16)
            kn = kv2[:, sl]
            rinv_k = lax.rsqrt((jnp.sum(kn * kn, axis=-1, keepdims=True) + kr_ssq) * (1.0 / QK_HEAD) + RMS_EPS)
            kh = (kn * gk[0:1] + k_rot) * rinv_k
            k_out[rs, sl] = (kh + shift_col).astype(BF16)
            v_out[rs, sl] = (kv2[:, MLA_PAD + hd * HEAD_PAD:MLA_PAD + (hd + 1) * HEAD_PAD] + ones_col).astype(BF16)

    nsplit = PROJ_SPLIT if tm % (8 * PROJ_SPLIT) == 0 else 1
    rows_per = tm // nsplit
    _interleave([rows_gen(j * rows_per, (j + 1) * rows_per) for j in range(nsplit)])


def _proj(x2, seq, p, tm):
    n = x2.shape[0]
    tm = min(tm, seq)
    nseq_t = seq // tm
    row = lambda i: (i, 0)
    fixed = lambda i: (0, 0)
    pos = lambda i: (i % nseq_t, 0)
    full = lambda a: pl.BlockSpec(a.shape, fixed)
    g8 = tm // 8
    ngrp = n // 8
    x3 = x2.reshape(ngrp, 8, D_MODEL)
    rw_names = ('mu', 'w_dec', 'w0', 'a_up', 'a0', 'g_up', 'k_k', 'k_a', 'r_k', 'seg')
    mla_sds = jax.ShapeDtypeStruct((n, MLA_PAD), BF16)
    rw_sds = jax.ShapeDtypeStruct((n, RW_WIDTH), F32)
    return pl.pallas_call(
        functools.partial(_proj_kernel, tm=tm, seq=seq),
        grid=(n // tm,),
        in_specs=[pl.BlockSpec((tm, D_MODEL), row),
                  pl.BlockSpec((1, 8, D_MODEL), lambda i: (jnp.maximum(i * g8 - 1, 0), 0, 0)),
                  pl.BlockSpec((1, 8, D_MODEL), lambda i: (jnp.minimum((i + 1) * g8, ngrp - 1), 0, 0)),
                  full(p['g1']), full(p['w_in']), full(p['qag']),
                  full(p['w_uq']), full(p['kvag']), full(p['w_ukv']), full(p['gq']), full(p['gk']),
                  pl.BlockSpec((tm, LANE), pos), pl.BlockSpec((tm, LANE), pos)]
                 + [full(p[k]) for k in rw_names],
        out_specs=[pl.BlockSpec((tm, MLA_PAD), row)] * 3 + [pl.BlockSpec((tm, RW_WIDTH), row)] * 9,
        out_shape=[mla_sds] * 3 + [jax.ShapeDtypeStruct((n, RW_WIDTH), BF16)] * 5 + [rw_sds] * 4,
        compiler_params=_cparams(("parallel",)),
        name="proj",
    )(x2, x3, x3, p['g1'], p['w_in'], p['qag'], p['w_uq'], p['kvag'], p['w_ukv'], p['gq'], p['gk'],
      p['ct'][:seq], p['st'][:seq], *[p[k] for k in rw_names])


def _attn_kernel(q_ref, k_ref, v_ref, o_ref, m_ref, acc_ref, *, running_max, tks):
    j = pl.program_id(3)

    @pl.when(j == 0)
    def _():
        if running_max:
            m_ref[...] = jnp.full(m_ref.shape, -jnp.inf, F32)
        acc_ref[...] = jnp.zeros(acc_ref.shape, F32)

    heads = [slice(hh * HEAD_PAD, (hh + 1) * HEAD_PAD) for hh in range(2)]
    nsub = k_ref.shape[0] // tks

    def score_pair(sb):
        rows = slice(sb * tks, (sb + 1) * tks)
        return [lax.dot_general(k_ref[rows, sl], q_ref[:, sl], (((1,), (1,)), ((), ())),
                                preferred_element_type=F32) for sl in heads]

    accs = [acc_ref[hh] for hh in range(2)]
    maxes = [m_ref[hh] for hh in range(2)] if running_max else None
    scores = score_pair(0)
    for sb in range(nsub):
        nxt = score_pair(sb + 1) if sb + 1 < nsub else None
        rows = slice(sb * tks, (sb + 1) * tks)
        for hh, sl in enumerate(heads):
            s = scores[hh]
            if running_max:
                m_new = jnp.maximum(maxes[hh], jnp.max(s, axis=0, keepdims=True))
                p = jnp.exp2(s - m_new)
            else:
                p = jnp.exp2(s)
            pv = lax.dot_general(v_ref[rows, sl], p.astype(BF16), (((0,), (0,)), ((), ())),
                                 preferred_element_type=F32)
            if running_max:
                accs[hh] = jnp.exp2(maxes[hh] - m_new) * accs[hh] + pv
                maxes[hh] = m_new
            else:
                accs[hh] = accs[hh] + pv
        scores = nxt
    for hh in range(2):
        acc_ref[hh] = accs[hh]
        if running_max:
            m_ref[hh] = maxes[hh]

    @pl.when(j == pl.num_programs(3) - 1)
    def _():
        outs = []
        for hh in range(2):
            a = acc_ref[hh]
            outs.append(a[:V_HEAD] / a[V_HEAD:V_HEAD + 1])
        o_ref[...] = jnp.concatenate(outs, axis=0).T.astype(BF16)


def _attn(q, k, v, bsz, seq, fast_softmax):
    return lax.cond(fast_softmax,
                    lambda q, k, v: _attn_call(q, k, v, bsz, seq, TQ, TK, TKS, False),
                    lambda q, k, v: _attn_call(q, k, v, bsz, seq, TQ_MAX, TK_MAX, TKS_MAX, True),
                    q, k, v)


def _attn_call(q, k, v, bsz, seq, tq, tk, tks, running_max):
    tq = min(tq, seq)
    tk = min(tk, seq)
    tks = min(tks, tk)
    nq, nk = seq // tq, seq // tk
    n = q.shape[0]
    return pl.pallas_call(
        functools.partial(_attn_kernel, running_max=running_max, tks=tks),
        grid=(bsz, MLA_HEADS // 2, nq, nk),
        in_specs=[pl.BlockSpec((tq, 2 * HEAD_PAD), lambda b, h, i, j: (b * nq + i, h)),
                  pl.BlockSpec((tk, 2 * HEAD_PAD), lambda b, h, i, j: (b * nk + j, h)),
                  pl.BlockSpec((tk, 2 * HEAD_PAD), lambda b, h, i, j: (b * nk + j, h))],
        out_specs=pl.BlockSpec((tq, 2 * V_HEAD), lambda b, h, i, j: (b * nq + i, h)),
        out_shape=jax.ShapeDtypeStruct((n, MLA_WIDTH), BF16),
        scratch_shapes=[pltpu.VMEM((2, 1, tq), F32), pltpu.VMEM((2, HEAD_PAD, tq), F32)],
        compiler_params=_cparams(("parallel", "parallel", "parallel", "arbitrary")),
        name="attn_max" if running_max else "attn",
    )(q, k, v)


def _rw_token_maps(u, prev_row, next_row, rs, mu_ref, wdec_ref, w0_ref, aup_ref, a0_ref, gup_ref,
                   kk_ref, ka_ref, rk_ref, seg_ref,
                   r_out, k_out, v_out, kkn_out, kka_out, lwf_out, lwb_out, g_out, bonus_out):
    tm = u.shape[0]
    sub = lax.broadcasted_iota(jnp.int32, (8, 1), 0)
    prev = pltpu.roll(u, 1, 0)
    prev = jnp.concatenate([jnp.where(sub == 0, prev_row, prev[:8]), prev[8:]], axis=0)
    nxt = pltpu.roll(u, tm - 1, 0)
    nxt = jnp.concatenate([nxt[:tm - 8], jnp.where(sub == 7, next_row, nxt[tm - 8:])], axis=0)
    us = u + (0.5 * (prev + nxt) - u) * mu_ref[...]

    r = us[:, :RW_WIDTH]
    k = us[:, RW_WIDTH:2 * RW_WIDTH]
    v = us[:, 2 * RW_WIDTH:3 * RW_WIDTH]
    xw = us[:, 3 * RW_WIDTH:3 * RW_WIDTH + LANE]
    xa = us[:, 3 * RW_WIDTH + LANE:3 * RW_WIDTH + 2 * LANE]
    xg = us[:, 3 * RW_WIDTH + 2 * LANE:]
    r_out[rs, :] = r.astype(BF16)
    v_out[rs, :] = v.astype(BF16)

    wdot = _dot(jnp.tanh(xw).astype(BF16), wdec_ref[...])
    adot = _dot(xa.astype(BF16), aup_ref[...])
    gdot = _dot(jax.nn.sigmoid(xg).astype(BF16), gup_ref[...])
    seg = seg_ref[...]
    kk = k * kk_ref[...]
    kss = _split_dot(kk * kk, seg)
    yield
    lw = jax.nn.sigmoid(w0_ref[...] + wdot) * (-math.exp(-0.5))
    lwf_out[rs, :] = lw[:, :RW_WIDTH]
    lwb_out[rs, :] = lw[:, RW_WIDTH:]
    a = jax.nn.sigmoid(a0_ref[...] + adot)
    g_out[rs, :] = gdot
    kkn = kk * lax.rsqrt(jnp.maximum(kss, 1e-24))
    k2 = k * (1.0 + (a - 1.0) * ka_ref[...])
    k_out[rs, :] = k2.astype(BF16)
    kkn_out[rs, :] = kkn.astype(BF16)
    kka_out[rs, :] = (kkn * a).astype(BF16)
    bsum = _split_dot(r * k2 * rk_ref[...], seg)
    yield
    bonus_out[rs, :] = bsum * v


def _bd(x, bd16):
    return jnp.concatenate([x.astype(BF16)] * HG, axis=0) * bd16


def _chunk_intra(r, k, v, kkn, kka, lw, reverse, consts):
    tri, strict_m, incl_m, eye_p, bd16, bd32, eye32 = consts
    lg = _split_dot_left(tri, lw)
    yield
    lgx = lg - lw
    tot = lg[0:1] if reverse else lg[CHUNK - 1:CHUNK]
    gi = jnp.exp(lg)
    ginv = jnp.exp(-lg)
    gend = jnp.exp(tot - lg)
    at = -kkn * jnp.exp(lgx)
    rt = r * gi
    bt = kka * ginv
    kt = k * ginv
    bh = kka * gend
    kh = k * gend

    lhs = jnp.concatenate([at, rt], axis=0).astype(BF16)
    rhs = jnp.concatenate([_bd(bt, bd16), _bd(kt, bd16)], axis=0)
    a_all = lax.dot_general(lhs, rhs, (((1,), (1,)), ((), ())), preferred_element_type=F32)
    yield
    n_ab = a_all[:CHUNK, :HGW] * strict_m
    a_ak = a_all[:CHUNK, HGW:] * strict_m
    a_rb = a_all[CHUNK:, :HGW] * incl_m
    a_rk = a_all[CHUNK:, HGW:] * incl_m

    t = eye_p + n_ab
    nk = _dot(n_ab.astype(BF16), _bd(n_ab, bd16))
    yield
    for _ in range(4):
        both = _dot(jnp.concatenate([nk, t], axis=0).astype(BF16), _bd(nk, bd16))
        yield
        nk = both[:CHUNK]
        t = t + both[CHUNK:]
    tn = _dot(t.astype(BF16), _bd(nk, bd16))
    yield
    t = t + tn

    tb = t.astype(BF16)
    vbd = _bd(v, bd16)
    w = _dot(tb, _bd(at, bd16))
    akv = _dot(a_ak.astype(BF16), vbd)
    yield
    uv = _dot(tb, _bd(akv, bd16))
    arb = a_rb.astype(BF16)
    qpd = _dot(arb, _bd(w, bd16))
    yield
    y_in = _dot(jnp.concatenate([arb, a_rk.astype(BF16)], axis=1),
                jnp.concatenate([_bd(uv, bd16), vbd], axis=0))
    lhs_t = jnp.concatenate([bh, kh], axis=0).astype(BF16)
    rhs_t = jnp.concatenate([jnp.concatenate([w, uv], axis=1),
                             jnp.concatenate([jnp.zeros_like(v), v], axis=1)], axis=0).astype(BF16)
    mc = lax.dot_general(lhs_t, rhs_t, (((0,), (0,)), ((), ())), preferred_element_type=F32)
    yield
    m_bd = mc[:, :HGW] * bd32 + eye32 * jnp.exp(tot)
    c_bd = mc[:, HGW:] * bd32
    return y_in, (rt + qpd).astype(BF16), m_bd.astype(BF16), c_bd


def _interleave(gens):
    results = [None] * len(gens)
    active = list(range(len(gens)))
    while active:
        still = []
        for i in active:
            try:
                next(gens[i])
                still.append(i)
            except StopIteration as stop:
                results[i] = stop.value
        active = still
    return results


def _split_dot_left(b_exact, a):
    hi = a.astype(BF16)
    lo = (a - hi.astype(F32)).astype(BF16)
    return _dot(b_exact, hi) + _dot(b_exact, lo)


def _rwscan_kernel(rf, kf, vf, nf, af, lf, rb, kb, vb, nb, ab, lb, tri_ref, m64_ref, bd16_ref, bd32_ref,
                   yf_out, yb_out, s_ref):
    c = pl.program_id(1)

    @pl.when(c == 0)
    def _():
        s_ref[...] = jnp.zeros(s_ref.shape, F32)

    nch = rf.shape[0] // CHUNK
    bd16 = bd16_ref[...]
    bd32 = bd32_ref[0]
    eye32 = bd32_ref[1]
    eye_p = m64_ref[4]
    ngrp = RW_HEADS // HG
    chains = []
    gens = []
    for d, refs, y_out in ((0, (rf, kf, vf, nf, af, lf), yf_out), (1, (rb, kb, vb, nb, ab, lb), yb_out)):
        reverse = d == 1
        consts = (tri_ref[d], m64_ref[2 * d], m64_ref[2 * d + 1], eye_p, bd16, bd32, eye32)
        order = list(range(nch - 1, -1, -1) if reverse else range(nch))
        for g in range(ngrp):
            chains.append((d, g, y_out, order))
            for j in order:
                ops = [x[j * CHUNK:(j + 1) * CHUNK, g * HGW:(g + 1) * HGW].astype(F32) for x in refs]
                gens.append(_chunk_intra(*ops, reverse, consts))
    parts = _interleave(gens)

    states = [s_ref[d, g] for d, g, _, _ in chains]
    for step in range(nch):
        for ci, (d, g, y_out, order) in enumerate(chains):
            j = order[step]
            y_in, qp, m_bd, c_bd = parts[ci * nch + step]
            sb = states[ci].astype(BF16)
            y_out[j * CHUNK:(j + 1) * CHUNK, g * HGW:(g + 1) * HGW] = y_in + _dot(qp, sb)
            states[ci] = _dot(m_bd, sb) + c_bd
    for ci, (d, g, _, _) in enumerate(chains):
        s_ref[d, g] = states[ci]


def _rwscan(r, k, v, kkn, kka, lwf, lwb, bsz, seq, p):
    n = r.shape[0]
    rows = min(NCH * CHUNK, seq)
    nb = seq // rows
    fwd = pl.BlockSpec((rows, RW_WIDTH), lambda b, c: (b * nb + c, 0))
    bwd = pl.BlockSpec((rows, RW_WIDTH), lambda b, c: (b * nb + nb - 1 - c, 0))
    full = lambda a: pl.BlockSpec(a.shape, lambda b, c: (0,) * a.ndim)
    sds = jax.ShapeDtypeStruct((n, RW_WIDTH), F32)
    return pl.pallas_call(
        _rwscan_kernel,
        grid=(bsz, nb),
        in_specs=[fwd] * 6 + [bwd] * 6 + [full(p['sc_tri']), full(p['sc_m64']), full(p['sc_bd16']),
                                           full(p['sc_bd32'])],
        out_specs=[fwd, bwd],
        out_shape=[sds, sds],
        scratch_shapes=[pltpu.VMEM((2, RW_HEADS // HG, HGW, HGW), F32)],
        compiler_params=_cparams(("parallel", "arbitrary")),
        name="rwscan",
    )(r, k, v, kkn, kka, lwf, r, k, v, kkn, kka, lwb, p['sc_tri'], p['sc_m64'], p['sc_bd16'], p['sc_bd32'])


def _post_kernel(x_ref, oa_ref, yf_ref, yb_ref, bonus_ref, g_ref, lng_ref, lnb_ref, seg_ref, wout_ref,
                 g2_ref, wr_ref, br_ref, x1_out, hp_out, gid_out):
    tm = x_ref.shape[0]
    nsplit = 2 if tm % 16 == 0 else 1
    rows_per = tm // nsplit

    def rows_gen(rs):
        seg = seg_ref[...]
        oa_part = _dot(oa_ref[rs, :], wout_ref[:MLA_WIDTH, :])
        y = yf_ref[rs, :] + yb_ref[rs, :]
        mean = _split_dot(y, seg) * (1.0 / RW_HEAD)
        yield
        dlt = y - mean
        var = _split_dot(dlt * dlt, seg) * (1.0 / RW_HEAD)
        yield
        yn = dlt * lax.rsqrt(var + LN_X_EPS) * lng_ref[...] + lnb_ref[...]
        ob = (yn + bonus_ref[rs, :]) * g_ref[rs, :]
        x1 = x_ref[rs, :] + oa_part + _dot(ob.astype(BF16), wout_ref[MLA_WIDTH:, :])
        yield
        x1_out[rs, :] = x1
        h2 = x1 * lax.rsqrt(jnp.mean(x1 * x1, axis=-1, keepdims=True) + RMS_EPS) * g2_ref[...]
        h2b = h2.astype(BF16)
        bits = lax.bitcast_convert_type(h2b.astype(F32), jnp.uint32)
        hp_out[rs, :] = (bits[:, :D_MODEL // 2] >> 16) | (bits[:, D_MODEL // 2:] & jnp.uint32(0xFFFF0000))
        logits = _dot(h2b, wr_ref[...]) + br_ref[...]
        yield
        _, gidx, _ = _group_choice(logits)
        gid_out[rs, :] = jnp.broadcast_to(gidx, (rows_per, LANE))

    _interleave([rows_gen(slice(j * rows_per, (j + 1) * rows_per)) for j in range(nsplit)])


def _group_choice(logits):
    lane_i = lax.broadcasted_iota(jnp.int32, logits.shape, 1)
    lane = lane_i.astype(F32)
    is_g = (lane_i >= N_EXPERTS) & (lane_i < N_EXPERTS + N_GROUPS)
    gl = jnp.where(is_g, logits, -jnp.inf)
    gmax = jnp.max(gl, axis=-1, keepdims=True)
    gidx = jnp.min(jnp.where(gl == gmax, lane, 1e9), axis=-1, keepdims=True) - N_EXPERTS
    p_g = 1.0 / jnp.sum(jnp.where(is_g, jnp.exp(gl - gmax), 0.0), axis=-1, keepdims=True)
    return p_g, gidx, lane


def _post(x2, oa, yf, yb, bonus, g, p, tm):
    n = x2.shape[0]
    tm = min(tm, n)
    row = lambda i: (i, 0)
    fixed = lambda i: (0, 0)
    full = lambda a: pl.BlockSpec(a.shape, fixed)
    rw = pl.BlockSpec((tm, RW_WIDTH), row)
    return pl.pallas_call(
        _post_kernel,
        grid=(n // tm,),
        in_specs=[pl.BlockSpec((tm, D_MODEL), row), pl.BlockSpec((tm, MLA_WIDTH), row), rw, rw, rw, rw,
                  full(p['ln_g']), full(p['ln_b']), full(p['seg']), full(p['w_out']), full(p['g2']),
                  full(p['w_r']), full(p['b_r'])],
        out_specs=[pl.BlockSpec((tm, D_MODEL), row), pl.BlockSpec((tm, D_MODEL // 2), row),
                   pl.BlockSpec((tm, LANE), row)],
        out_shape=[jax.ShapeDtypeStruct((n, D_MODEL), F32), jax.ShapeDtypeStruct((n, D_MODEL // 2), jnp.uint32),
                   jax.ShapeDtypeStruct((n, LANE), F32)],
        compiler_params=_cparams(("parallel",)),
        name="post",
    )(x2, oa, yf, yb, bonus, g, p['ln_g'], p['ln_b'], p['seg'], p['w_out'], p['g2'], p['w_r'], p['b_r'])


def _route_plan(gid, tm, npt):
    g = gid[:, 0].astype(jnp.int32)
    onehot = (g[:, None] == jnp.arange(N_GROUPS, dtype=jnp.int32)[None, :]).astype(jnp.int32)
    csum = jnp.cumsum(onehot, axis=0)
    rank = jnp.sum(csum * onehot, axis=1) - 1
    ntile = (csum[-1] + tm - 1) // tm
    tile_end = jnp.cumsum(ntile)
    off = (tile_end - ntile) * tm
    pos = jnp.sum(onehot * off[None, :], axis=1) + rank
    n_active = tile_end[-1]
    t = jnp.minimum(jnp.arange(npt, dtype=jnp.int32), n_active - 1)
    tile_gid = jnp.sum((t[:, None] >= tile_end[None, :]).astype(jnp.int32), axis=1)
    meta = jnp.concatenate([n_active[None], tile_gid]).astype(jnp.int32)
    return pos.astype(jnp.int32), meta


def _row_dma_loops(n_rows, make_copy, all_rows_copy):
    def issue(h, carry):
        make_copy(2 * h).start(priority=0)
        make_copy(2 * h + 1).start(priority=1)
        return carry
    lax.fori_loop(0, n_rows // 2, issue, 0, unroll=4)
    all_rows_copy.wait()


def _scatter_kernel(pos_ref, src_ref, init_ref, dst_ref, sem):
    n_rows = src_ref.shape[0]
    _row_dma_loops(n_rows, lambda r: pltpu.make_async_copy(
        src_ref.at[pl.ds(r, 1)], dst_ref.at[pl.ds(pos_ref[0, 0, r], 1)], sem),
        pltpu.make_async_copy(src_ref, dst_ref.at[pl.ds(0, n_rows)], sem))


def _scatter_rows(pos, src, n_dst, tms):
    n, w = src.shape
    tms = min(tms, n)
    return pl.pallas_call(
        _scatter_kernel,
        grid=(n // tms,),
        in_specs=[pl.BlockSpec((1, 1, tms), lambda i: (i, 0, 0), memory_space=pltpu.SMEM),
                  pl.BlockSpec((tms, w), lambda i: (i, 0)), pl.BlockSpec(memory_space=pl.ANY)],
        out_specs=pl.BlockSpec(memory_space=pl.ANY),
        out_shape=jax.ShapeDtypeStruct((n_dst, w), src.dtype),
        scratch_shapes=[pltpu.SemaphoreType.DMA(())],
        input_output_aliases={2: 0},
        compiler_params=_cparams(("arbitrary",)),
        name="moe_scatter",
    )(pos.reshape(n // tms, 1, tms), src, jnp.zeros((n_dst, w), src.dtype))


def _moe_group_kernel(meta_ref, xs_ref, wr_ref, br_ref, wg_ref, wu_ref, wd_ref, ys_ref):
    i = pl.program_id(0)

    @pl.when(i >= meta_ref[0])
    def _():
        ys_ref[...] = jnp.zeros(ys_ref.shape, F32)

    @pl.when(i < meta_ref[0])
    def _():
        tm = xs_ref.shape[0]
        word = xs_ref[...]
        h2 = jnp.concatenate([lax.bitcast_convert_type(word << 16, F32),
                              lax.bitcast_convert_type(word & jnp.uint32(0xFFFF0000), F32)],
                             axis=1).astype(BF16)
        def up(c):
            ks = range(c * MOE_EPC, (c + 1) * MOE_EPC)
            return (jnp.concatenate([_dot(h2, wg_ref[k]) for k in ks], axis=1),
                    jnp.concatenate([_dot(h2, wu_ref[k]) for k in ks], axis=1))

        pend = up(0)
        logits = _dot(h2, wr_ref[...]) + br_ref[...]
        p_g, _, lane = _group_choice(logits)
        first = (meta_ref[1 + i] * EXPERTS_PER_GROUP).astype(F32)
        in_grp = (lane >= first) & (lane < first + EXPERTS_PER_GROUP)
        el = jnp.where(in_grp, logits, -jnp.inf)
        v1 = jnp.max(el, axis=-1, keepdims=True)
        i1 = jnp.min(jnp.where(el == v1, lane, 1e9), axis=-1, keepdims=True)
        el2 = jnp.where(lane == i1, -jnp.inf, el)
        v2 = jnp.max(el2, axis=-1, keepdims=True)
        i2 = jnp.min(jnp.where(el2 == v2, lane, 1e9), axis=-1, keepdims=True)
        e2 = jnp.exp(v2 - v1)
        den = 1.0 + e2
        w1 = (1.0 / den) * p_g
        w2 = (e2 / den) * p_g

        acc = None
        nchunk = EXPERTS_PER_GROUP // MOE_EPC
        for c in range(nchunk):
            nxt = up(c + 1) if c + 1 < nchunk else None
            hg, hu = pend
            gate = jnp.concatenate(
                [jnp.broadcast_to(jnp.where(i1 == first + k, w1, 0.0) + jnp.where(i2 == first + k, w2, 0.0),
                                  (tm, D_EXPERT)) for k in range(c * MOE_EPC, (c + 1) * MOE_EPC)], axis=1)
            act = hg * jax.nn.sigmoid(hg) * hu * gate
            wd = wd_ref[c * MOE_EPC:(c + 1) * MOE_EPC].reshape(MOE_EPC * D_EXPERT, D_MODEL)
            part = _dot(act.astype(BF16), wd)
            acc = part if acc is None else acc + part
            pend = nxt
        ys_ref[...] = acc


def _moe_grouped(xs, meta, p, tm, npt):
    def tile(i, m):
        return (jnp.minimum(i, m[0] - 1), 0)

    def group(i, m):
        return (m[1 + i], 0, 0)

    grid_spec = pltpu.PrefetchScalarGridSpec(
        num_scalar_prefetch=1,
        grid=(npt,),
        in_specs=[pl.BlockSpec((tm, D_MODEL // 2), tile),
                  pl.BlockSpec(p['w_r'].shape, lambda i, m: (0, 0)),
                  pl.BlockSpec(p['b_r'].shape, lambda i, m: (0, 0)),
                  pl.BlockSpec((EXPERTS_PER_GROUP, D_MODEL, D_EXPERT), group),
                  pl.BlockSpec((EXPERTS_PER_GROUP, D_MODEL, D_EXPERT), group),
                  pl.BlockSpec((EXPERTS_PER_GROUP, D_EXPERT, D_MODEL), group)],
        out_specs=pl.BlockSpec((tm, D_MODEL), lambda i, m: (i, 0)),
    )
    return pl.pallas_call(
        _moe_group_kernel,
        grid_spec=grid_spec,
        out_shape=jax.ShapeDtypeStruct((npt * tm, D_MODEL), F32),
        compiler_params=_cparams(("arbitrary",)),
        name="moe_group",
    )(meta, xs, p['w_r'], p['b_r'], p['wg'], p['wu'], p['wd'])


def _gather_kernel(pos_ref, x1_ref, ys_ref, o_ref, buf, sem):
    n_rows = buf.shape[0]
    _row_dma_loops(n_rows, lambda r: pltpu.make_async_copy(
        ys_ref.at[pl.ds(pos_ref[0, 0, r], 1)], buf.at[pl.ds(r, 1)], sem),
        pltpu.make_async_copy(ys_ref.at[pl.ds(0, n_rows)], buf, sem))
    o_ref[...] = x1_ref[...] + buf[...]


def _gather_add(pos, x1, ys, tms):
    n, d = x1.shape
    tms = min(tms, n)
    return pl.pallas_call(
        _gather_kernel,
        grid=(n // tms,),
        in_specs=[pl.BlockSpec((1, 1, tms), lambda i: (i, 0, 0), memory_space=pltpu.SMEM),
                  pl.BlockSpec((tms, d), lambda i: (i, 0)),
                  pl.BlockSpec(memory_space=pl.ANY)],
        out_specs=pl.BlockSpec((tms, d), lambda i: (i, 0)),
        out_shape=jax.ShapeDtypeStruct((n, d), F32),
        scratch_shapes=[pltpu.VMEM((tms, d), F32), pltpu.SemaphoreType.DMA(())],
        compiler_params=_cparams(("arbitrary",)),
        name="moe_gather",
    )(pos.reshape(n // tms, 1, tms), x1, ys)


def _head_pad_cols(w, head_w, lo, hi, dst, width=HEAD_PAD, perm=None):
    rows = w.shape[0]
    wh = w.reshape(rows, -1, head_w)[:, :, lo:hi]
    if perm is not None:
        wh = wh[:, :, perm]
    out = jnp.zeros((rows, wh.shape[1], width), w.dtype)
    return out.at[:, :, dst:dst + (hi - lo)].set(wh).reshape(rows, -1)


def _pack_params(lp, max_seq):
    (norm1_g, w_in, q_a_norm_g, w_uq, kv_a_norm_g, w_ukv, q_norm_g, k_norm_g, mu_shift, w0_f, w_up_f,
     w0_b, w_up_b, a0, a_up, g_up, k_k, k_a, r_k, ln_x_g, ln_x_b, w_out, norm2_g, w_router_group,
     b_router_group, w_router_expert, b_router_expert, w_expert_gate, w_expert_up, w_expert_down) = lp
    half = QK_ROPE // 2
    swap = jnp.concatenate([jnp.arange(half, QK_ROPE), jnp.arange(half)])
    zcol = lambda rows, n: jnp.zeros((rows, n), F32)
    p = {}
    c0 = Q_LORA + KV_LORA
    w_kr = w_in[:, c0:c0 + QK_ROPE]
    u0 = c0 + QK_ROPE
    wu_ = w_in[:, u0:]
    d = D_MODEL
    w_in_p = jnp.concatenate([
        w_in[:, :c0],
        zcol(d, QK_NOPE), w_kr, zcol(d, LANE - QK_HEAD),
        zcol(d, QK_NOPE), w_kr[:, swap], zcol(d, LANE - QK_HEAD),
        wu_[:, :3 * RW_WIDTH + 2 * DECAY_LORA + A_LORA], zcol(d, LANE - A_LORA),
        wu_[:, 3 * RW_WIDTH + 2 * DECAY_LORA + A_LORA:]], axis=1)
    p['w_in'] = w_in_p.astype(BF16)
    mu = mu_shift[None, :]
    p['mu'] = jnp.concatenate([mu[:, :3 * RW_WIDTH + 2 * DECAY_LORA + A_LORA], zcol(1, LANE - A_LORA),
                               mu[:, 3 * RW_WIDTH + 2 * DECAY_LORA + A_LORA:]], axis=1)
    p['g1'] = norm1_g[None, :]
    p['qag'] = q_a_norm_g[None, :]
    p['kvag'] = kv_a_norm_g[None, :]
    p['w_uq'] = jnp.concatenate([
        _head_pad_cols(w_uq, QK_HEAD, 0, QK_HEAD, 0),
        _head_pad_cols(w_uq, QK_HEAD, QK_NOPE, QK_HEAD, QK_NOPE, perm=swap)], axis=1).astype(BF16)
    p['w_ukv'] = jnp.concatenate([
        _head_pad_cols(w_ukv, QK_NOPE + V_HEAD, 0, QK_NOPE, 0),
        _head_pad_cols(w_ukv, QK_NOPE + V_HEAD, QK_NOPE, QK_NOPE + V_HEAD, 0)], axis=1).astype(BF16)

    def gain_rows(g):
        main = jnp.concatenate([g, jnp.zeros((LANE - QK_HEAD,), F32)])
        swp = jnp.concatenate([jnp.zeros((QK_NOPE,), F32), g[QK_NOPE:][swap], jnp.zeros((LANE - QK_HEAD,), F32)])
        return jnp.stack([main, swp])
    kbound = math.sqrt(QK_HEAD) * jnp.max(jnp.abs(k_norm_g))
    qbound = jnp.max(jnp.abs(q_norm_g)) * LOG2E
    p['gq'] = jnp.concatenate([gain_rows(q_norm_g), jnp.full((1, LANE), qbound * kbound, F32)], axis=0)
    p['gk'] = gain_rows(k_norm_g)
    p['fast_softmax'] = 2.0 * qbound * kbound <= FAST_SOFTMAX_MAX_SHIFT
    inv_freq = 1.0 / (ROPE_THETA ** (jnp.arange(half, dtype=F32) / half))
    ang = jnp.arange(max_seq, dtype=F32)[:, None] * inv_freq[None, :]
    cos, sin = jnp.cos(ang), jnp.sin(ang)
    zpad = jnp.zeros((max_seq, LANE - QK_HEAD), F32)
    p['ct'] = jnp.concatenate([jnp.ones((max_seq, QK_NOPE), F32), cos, cos, zpad], axis=1)
    p['st'] = jnp.concatenate([jnp.zeros((max_seq, QK_NOPE), F32), -sin, sin, zpad], axis=1)

    zl = jnp.zeros((DECAY_LORA, RW_WIDTH), F32)
    p['w_dec'] = jnp.concatenate([jnp.concatenate([w_up_f, zl], axis=1),
                                  jnp.concatenate([zl, w_up_b], axis=1)], axis=0).astype(BF16)
    p['w0'] = jnp.concatenate([w0_f, w0_b])[None, :]
    p['a_up'] = jnp.concatenate([a_up, jnp.zeros((LANE - A_LORA, RW_WIDTH), F32)], axis=0).astype(BF16)
    p['a0'] = a0[None, :]
    p['g_up'] = g_up.astype(BF16)
    p['k_k'] = k_k[None, :]
    p['k_a'] = k_a[None, :]
    p['r_k'] = r_k.reshape(1, RW_WIDTH)
    hid = jnp.arange(RW_WIDTH) // RW_HEAD
    p['seg'] = (hid[:, None] == hid[None, :]).astype(BF16)
    ti = jnp.arange(CHUNK)[:, None]
    tj = jnp.arange(CHUNK)[None, :]
    p['sc_tri'] = jnp.stack([tj <= ti, tj >= ti]).astype(BF16)
    ps = (jnp.arange(HGW) % CHUNK)[None, :]
    p['sc_m64'] = jnp.stack([ps < ti, ps <= ti, ps > ti, ps >= ti, ps == ti]).astype(F32)
    blk = jnp.arange(HGW) // RW_HEAD
    same = blk[:, None] == blk[None, :]
    p['sc_bd16'] = same.astype(BF16)
    p['sc_bd32'] = jnp.stack([same, jnp.eye(HGW, dtype=bool)]).astype(F32)
    p['ln_g'] = ln_x_g[None, :]
    p['ln_b'] = ln_x_b[None, :]
    p['w_out'] = w_out.astype(BF16)
    p['g2'] = norm2_g[None, :]
    p['w_r'] = jnp.concatenate([w_router_expert, w_router_group,
                                zcol(d, LANE - N_EXPERTS - N_GROUPS)], axis=1).astype(BF16)
    p['b_r'] = jnp.concatenate([b_router_expert, b_router_group,
                                jnp.zeros((LANE - N_EXPERTS - N_GROUPS,), F32)])[None, :]
    p['wg'] = w_expert_gate.astype(BF16)
    p['wu'] = w_expert_up.astype(BF16)
    p['wd'] = w_expert_down.astype(BF16)
    return p


TM_PROJ = 512
PROJ_SPLIT = 2
TQ = 1024
TK = 8192
TKS = 2048
TQ_MAX = 1024
TK_MAX = 2048
TKS_MAX = 1024
TM_POST = 512
TM_MOE = 512
TM_ROWS = 1024


def _layer(x, p):
    bsz, seq, d = x.shape
    n = bsz * seq
    x2 = x.reshape(n, d)
    q, k, v, r, k2, vv, kkn, kka, lwf, lwb, g, bonus = _proj(x2, seq, p, TM_PROJ)
    oa = _attn(q, k, v, bsz, seq, p['fast_softmax'])
    yf, yb = _rwscan(r, k2, vv, kkn, kka, lwf, lwb, bsz, seq, p)
    x1, hp, gid = _post(x2, oa, yf, yb, bonus, g, p, TM_POST)
    tm = min(TM_MOE, n)
    npt = n // tm + N_GROUPS
    pos, meta = _route_plan(gid, tm, npt)
    xs = _scatter_rows(pos, hp, npt * tm, TM_ROWS)
    ys = _moe_grouped(xs, meta, p, tm, npt)
    out = _gather_add(pos, x1, ys, TM_ROWS)
    return out.reshape(bsz, seq, d)


def kernel(x_prompt, x_sample, norm1_g, w_in, q_a_norm_g, w_uq, kv_a_norm_g, w_ukv, q_norm_g, k_norm_g, mu_shift, w0_f, w_up_f, w0_b, w_up_b, a0, a_up, g_up, k_k, k_a, r_k, ln_x_g, ln_x_b, w_out, norm2_g, w_router_group, b_router_group, w_router_expert, b_router_expert, w_expert_gate, w_expert_up, w_expert_down):
    layer_params = (norm1_g, w_in, q_a_norm_g, w_uq, kv_a_norm_g, w_ukv, q_norm_g, k_norm_g, mu_shift,
                    w0_f, w_up_f, w0_b, w_up_b, a0, a_up, g_up, k_k, k_a, r_k, ln_x_g, ln_x_b, w_out,
                    norm2_g, w_router_group, b_router_group, w_router_expert, b_router_expert,
                    w_expert_gate, w_expert_up, w_expert_down)
    y_prompt, y_sample = x_prompt, x_sample
    max_seq = max(x_prompt.shape[1], x_sample.shape[1])
    for layer in range(norm1_g.shape[0]):
        p = _pack_params([w[layer] for w in layer_params], max_seq)
        y_prompt = _layer(y_prompt, p)
        y_sample = _layer(y_sample, p)
    return (y_prompt, y_sample)
```

```python
import functools
import math

import jax
import jax.numpy as jnp
from jax import lax
from jax.experimental import pallas as pl
from jax.experimental.pallas import tpu as pltpu

D_MODEL = 1024
MLA_HEADS = 8
QK_NOPE = 64
QK_ROPE = 32
QK_HEAD = QK_NOPE + QK_ROPE
V_HEAD = 64
Q_LORA = 384
KV_LORA = 256
ROPE_THETA = 10000.0
RW_HEADS = 8
RW_HEAD = 64
RW_WIDTH = RW_HEADS * RW_HEAD
DECAY_LORA = 64
A_LORA = 64
GATE_LORA = 128
LN_X_EPS = 6.4e-4
MLA_WIDTH = MLA_HEADS * V_HEAD
N_GROUPS = 4
EXPERTS_PER_GROUP = 8
N_EXPERTS = N_GROUPS * EXPERTS_PER_GROUP
D_EXPERT = 256
MOE_EPC = 2
RMS_EPS = 1e-6

LANE = 128
HEAD_PAD = LANE
MLA_PAD = MLA_HEADS * HEAD_PAD
RW_COLS_PAD = 3 * RW_WIDTH + 3 * LANE
MLA_COLS_PAD = Q_LORA + KV_LORA + 2 * LANE
D_IN_PAD = MLA_COLS_PAD + RW_COLS_PAD
CHUNK = 64
HG = 4
HGW = HG * RW_HEAD
VMEM_LIMIT = 56 * 1024 * 1024
LOG2E = 1.4426950408889634
NCH = 8
FAST_SOFTMAX_MAX_SHIFT = 100.0

F32 = jnp.float32
BF16 = jnp.bfloat16


def _dot(a, b):
    return jnp.dot(a, b, preferred_element_type=F32)


def _split_dot(a, b_exact):
    hi = a.astype(BF16)
    lo = (a - hi.astype(F32)).astype(BF16)
    return _dot(hi, b_exact) + _dot(lo, b_exact)


def _cparams(sem):
    return pltpu.CompilerParams(dimension_semantics=sem, vmem_limit_bytes=VMEM_LIMIT)


def _proj_kernel(x_ref, xp_ref, xn_ref, g1_ref, win_ref, qag_ref, wuq_ref, kvag_ref, wukv_ref, gq_ref, gk_ref,
                 ct_ref, st_ref, *rw_refs, tm, seq):
    q_out, k_out, v_out = rw_refs[10:13]
    i = pl.program_id(0)
    x = jnp.concatenate([xp_ref[0], x_ref[...], xn_ref[0]], axis=0)
    h = x * lax.rsqrt(jnp.mean(x * x, axis=-1, keepdims=True) + RMS_EPS) * g1_ref[...]
    ze = _dot(h.astype(BF16), win_ref[...])
    first = (i * tm) % seq == 0
    last = ((i + 1) * tm) % seq == 0

    gq = gq_ref[...]
    gk = gk_ref[...]
    scale = QK_HEAD ** -0.5 * LOG2E
    lane = lax.broadcasted_iota(jnp.int32, (1, LANE), 1)
    ones_col = jnp.where(lane == V_HEAD, 1.0, 0.0).astype(F32)
    shift_col = jnp.where(lane == QK_HEAD, 1.0, 0.0).astype(F32)
    q_shift = shift_col * gq[2:3, 0:1]

    def rows_gen(a, b):
        rs = slice(a, b)
        z = ze[8 + a:8 + b]
        prev_row = ze[7 + a:8 + a, MLA_COLS_PAD:]
        next_row = ze[8 + b:9 + b, MLA_COLS_PAD:]
        if a == 0:
            prev_row = jnp.where(first, 0.0, prev_row)
        if b == tm:
            next_row = jnp.where(last, 0.0, next_row)
        yield from _rw_token_maps(z[:, MLA_COLS_PAD:], prev_row, next_row, rs, *rw_refs[:10], *rw_refs[13:])

        cq = z[:, :Q_LORA]
        cqn = cq * lax.rsqrt(jnp.mean(cq * cq, axis=-1, keepdims=True) + RMS_EPS) * qag_ref[...]
        q2 = _dot(cqn.astype(BF16), wuq_ref[...])
        ckv = z[:, Q_LORA:Q_LORA + KV_LORA]
        ckvn = ckv * lax.rsqrt(jnp.mean(ckv * ckv, axis=-1, keepdims=True) + RMS_EPS) * kvag_ref[...]
        kv2 = _dot(ckvn.astype(BF16), wukv_ref[...])
        yield
        kr_main = z[:, Q_LORA + KV_LORA:Q_LORA + KV_LORA + LANE]
        kr_swap = z[:, Q_LORA + KV_LORA + LANE:MLA_COLS_PAD]
        ct = ct_ref[rs, :]
        st = st_ref[rs, :]
        kr_ssq = jnp.sum(kr_main * kr_main, axis=-1, keepdims=True)
        q_main_f = gq[0:1] * ct
        q_swap_f = gq[1:2] * st
        k_rot = kr_main * (gk[0:1] * ct) + kr_swap * (gk[1:2] * st)
        for hd in range(MLA_HEADS):
            sl = slice(hd * HEAD_PAD, (hd + 1) * HEAD_PAD)
            qm = q2[:, sl]
            qs = q2[:, MLA_PAD + hd * HEAD_PAD:MLA_PAD + (hd + 1) * HEAD_PAD]
            rinv = lax.rsqrt(jnp.sum(qm * qm, axis=-1, keepdims=True) * (1.0 / QK_HEAD) + RMS_EPS)
            qh = (qm * q_main_f + qs * q_swap_f) * (rinv * scale)
            q_out[rs, sl] = (qh - q_shift).astype(BF16)
            kn = kv2[:, sl]
            rinv_k = lax.rsqrt((jnp.sum(kn * kn, axis=-1, keepdims=True) + kr_ssq) * (1.0 / QK_HEAD) + RMS_EPS)
            kh = (kn * gk[0:1] + k_rot) * rinv_k
            k_out[rs, sl] = (kh + shift_col).astype(BF16)
            v_out[rs, sl] = (kv2[:, MLA_PAD + hd * HEAD_PAD:MLA_PAD + (hd + 1) * HEAD_PAD] + ones_col).astype(BF16)

    nsplit = PROJ_SPLIT if tm % (8 * PROJ_SPLIT) == 0 else 1
    rows_per = tm // nsplit
    _interleave([rows_gen(j * rows_per, (j + 1) * rows_per) for j in range(nsplit)])


def _proj(x2, seq, p, tm):
    n = x2.shape[0]
    tm = min(tm, seq)
    nseq_t = seq // tm
    row = lambda i: (i, 0)
    fixed = lambda i: (0, 0)
    pos = lambda i: (i % nseq_t, 0)
    full = lambda a: pl.BlockSpec(a.shape, fixed)
    g8 = tm // 8
    ngrp = n // 8
    x3 = x2.reshape(ngrp, 8, D_MODEL)
    rw_names = ('mu', 'w_dec', 'w0', 'a_up', 'a0', 'g_up', 'k_k', 'k_a', 'r_k', 'seg')
    mla_sds = jax.ShapeDtypeStruct((n, MLA_PAD), BF16)
    rw_sds = jax.ShapeDtypeStruct((n, RW_WIDTH), F32)
    return pl.pallas_call(
        functools.partial(_proj_kernel, tm=tm, seq=seq),
        grid=(n // tm,),
        in_specs=[pl.BlockSpec((tm, D_MODEL), row),
                  pl.BlockSpec((1, 8, D_MODEL), lambda i: (jnp.maximum(i * g8 - 1, 0), 0, 0)),
                  pl.BlockSpec((1, 8, D_MODEL), lambda i: (jnp.minimum((i + 1) * g8, ngrp - 1), 0, 0)),
                  full(p['g1']), full(p['w_in']), full(p['qag']),
                  full(p['w_uq']), full(p['kvag']), full(p['w_ukv']), full(p['gq']), full(p['gk']),
                  pl.BlockSpec((tm, LANE), pos), pl.BlockSpec((tm, LANE), pos)]
                 + [full(p[k]) for k in rw_names],
        out_specs=[pl.BlockSpec((tm, MLA_PAD), row)] * 3 + [pl.BlockSpec((tm, RW_WIDTH), row)] * 9,
        out_shape=[mla_sds] * 3 + [rw_sds] * 9,
        compiler_params=_cparams(("parallel",)),
        name="proj",
    )(x2, x3, x3, p['g1'], p['w_in'], p['qag'], p['w_uq'], p['kvag'], p['w_ukv'], p['gq'], p['gk'],
      p['ct'][:seq], p['st'][:seq], *[p[k] for k in rw_names])


def _attn_kernel(q_ref, k_ref, v_ref, o_ref, m_ref, acc_ref, *, running_max, tks):
    j = pl.program_id(3)

    @pl.when(j == 0)
    def _():
        if running_max:
            m_ref[...] = jnp.full(m_ref.shape, -jnp.inf, F32)
        acc_ref[...] = jnp.zeros(acc_ref.shape, F32)

    heads = [slice(hh * HEAD_PAD, (hh + 1) * HEAD_PAD) for hh in range(2)]
    nsub = k_ref.shape[0] // tks

    def score_pair(sb):
        rows = slice(sb * tks, (sb + 1) * tks)
        return [lax.dot_general(k_ref[rows, sl], q_ref[:, sl], (((1,), (1,)), ((), ())),
                                preferred_element_type=F32) for sl in heads]

    accs = [acc_ref[hh] for hh in range(2)]
    maxes = [m_ref[hh] for hh in range(2)] if running_max else None
    scores = score_pair(0)
    for sb in range(nsub):
        nxt = score_pair(sb + 1) if sb + 1 < nsub else None
        rows = slice(sb * tks, (sb + 1) * tks)
        for hh, sl in enumerate(heads):
            s = scores[hh]
            if running_max:
                m_new = jnp.maximum(maxes[hh], jnp.max(s, axis=0, keepdims=True))
                p = jnp.exp2(s - m_new)
            else:
                p = jnp.exp2(s)
            pv = lax.dot_general(v_ref[rows, sl], p.astype(BF16), (((0,), (0,)), ((), ())),
                                 preferred_element_type=F32)
            if running_max:
                accs[hh] = jnp.exp2(maxes[hh] - m_new) * accs[hh] + pv
                maxes[hh] = m_new
            else:
                accs[hh] = accs[hh] + pv
        scores = nxt
    for hh in range(2):
        acc_ref[hh] = accs[hh]
        if running_max:
            m_ref[hh] = maxes[hh]

    @pl.when(j == pl.num_programs(3) - 1)
    def _():
        outs = []
        for hh in range(2):
            a = acc_ref[hh]
            outs.append(a[:V_HEAD] / a[V_HEAD:V_HEAD + 1])
        o_ref[...] = jnp.concatenate(outs, axis=0).T.astype(BF16)


def _attn(q, k, v, bsz, seq, fast_softmax):
    return lax.cond(fast_softmax,
                    lambda q, k, v: _attn_call(q, k, v, bsz, seq, TQ, TK, TKS, False),
                    lambda q, k, v: _attn_call(q, k, v, bsz, seq, TQ_MAX, TK_MAX, TKS_MAX, True),
                    q, k, v)


def _attn_call(q, k, v, bsz, seq, tq, tk, tks, running_max):
    tq = min(tq, seq)
    tk = min(tk, seq)
    tks = min(tks, tk)
    nq, nk = seq // tq, seq // tk
    n = q.shape[0]
    return pl.pallas_call(
        functools.partial(_attn_kernel, running_max=running_max, tks=tks),
        grid=(bsz, MLA_HEADS // 2, nq, nk),
        in_specs=[pl.BlockSpec((tq, 2 * HEAD_PAD), lambda b, h, i, j: (b * nq + i, h)),
                  pl.BlockSpec((tk, 2 * HEAD_PAD), lambda b, h, i, j: (b * nk + j, h)),
                  pl.BlockSpec((tk, 2 * HEAD_PAD), lambda b, h, i, j: (b * nk + j, h))],
        out_specs=pl.BlockSpec((tq, 2 * V_HEAD), lambda b, h, i, j: (b * nq + i, h)),
        out_shape=jax.ShapeDtypeStruct((n, MLA_WIDTH), BF16),
        scratch_shapes=[pltpu.VMEM((2, 1, tq), F32), pltpu.VMEM((2, HEAD_PAD, tq), F32)],
        compiler_params=_cparams(("parallel", "parallel", "parallel", "arbitrary")),
        name="attn_max" if running_max else "attn",
    )(q, k, v)


def _rw_token_maps(u, prev_row, next_row, rs, mu_ref, wdec_ref, w0_ref, aup_ref, a0_ref, gup_ref,
                   kk_ref, ka_ref, rk_ref, seg_ref,
                   r_out, k_out, v_out, kkn_out, kka_out, lwf_out, lwb_out, g_out, bonus_out):
    tm = u.shape[0]
    sub = lax.broadcasted_iota(jnp.int32, (8, 1), 0)
    prev = pltpu.roll(u, 1, 0)
    prev = jnp.concatenate([jnp.where(sub == 0, prev_row, prev[:8]), prev[8:]], axis=0)
    nxt = pltpu.roll(u, tm - 1, 0)
    nxt = jnp.concatenate([nxt[:tm - 8], jnp.where(sub == 7, next_row, nxt[tm - 8:])], axis=0)
    us = u + (0.5 * (prev + nxt) - u) * mu_ref[...]

    r = us[:, :RW_WIDTH]
    k = us[:, RW_WIDTH:2 * RW_WIDTH]
    v = us[:, 2 * RW_WIDTH:3 * RW_WIDTH]
    xw = us[:, 3 * RW_WIDTH:3 * RW_WIDTH + LANE]
    xa = us[:, 3 * RW_WIDTH + LANE:3 * RW_WIDTH + 2 * LANE]
    xg = us[:, 3 * RW_WIDTH + 2 * LANE:]
    r_out[rs, :] = r
    v_out[rs, :] = v

    wdot = _dot(jnp.tanh(xw).astype(BF16), wdec_ref[...])
    adot = _dot(xa.astype(BF16), aup_ref[...])
    gdot = _dot(jax.nn.sigmoid(xg).astype(BF16), gup_ref[...])
    seg = seg_ref[...]
    kk = k * kk_ref[...]
    kss = _split_dot(kk * kk, seg)
    yield
    lw = jax.nn.sigmoid(w0_ref[...] + wdot) * (-math.exp(-0.5))
    lwf_out[rs, :] = lw[:, :RW_WIDTH]
    lwb_out[rs, :] = lw[:, RW_WIDTH:]
    a = jax.nn.sigmoid(a0_ref[...] + adot)
    g_out[rs, :] = gdot
    kkn = kk * lax.rsqrt(jnp.maximum(kss, 1e-24))
    k2 = k * (1.0 + (a - 1.0) * ka_ref[...])
    k_out[rs, :] = k2
    kkn_out[rs, :] = kkn
    kka_out[rs, :] = kkn * a
    bsum = _split_dot(r * k2 * rk_ref[...], seg)
    yield
    bonus_out[rs, :] = bsum * v


def _bd(x, bd16):
    return jnp.concatenate([x.astype(BF16)] * HG, axis=0) * bd16


def _chunk_intra(r, k, v, kkn, kka, lw, reverse, consts):
    tri, strict_m, incl_m, eye_p, bd16, bd32, eye32 = consts
    lg = _split_dot_left(tri, lw)
    yield
    lgx = lg - lw
    tot = lg[0:1] if reverse else lg[CHUNK - 1:CHUNK]
    gi = jnp.exp(lg)
    ginv = jnp.exp(-lg)
    gend = jnp.exp(tot - lg)
    at = -kkn * jnp.exp(lgx)
    rt = r * gi
    bt = kka * ginv
    kt = k * ginv
    bh = kka * gend
    kh = k * gend

    lhs = jnp.concatenate([at, rt], axis=0).astype(BF16)
    rhs = jnp.concatenate([_bd(bt, bd16), _bd(kt, bd16)], axis=0)
    a_all = lax.dot_general(lhs, rhs, (((1,), (1,)), ((), ())), preferred_element_type=F32)
    yield
    n_ab = a_all[:CHUNK, :HGW] * strict_m
    a_ak = a_all[:CHUNK, HGW:] * strict_m
    a_rb = a_all[CHUNK:, :HGW] * incl_m
    a_rk = a_all[CHUNK:, HGW:] * incl_m

    t = eye_p + n_ab
    nk = _dot(n_ab.astype(BF16), _bd(n_ab, bd16))
    yield
    for _ in range(4):
        both = _dot(jnp.concatenate([nk, t], axis=0).astype(BF16), _bd(nk, bd16))
        yield
        nk = both[:CHUNK]
        t = t + both[CHUNK:]
    tn = _dot(t.astype(BF16), _bd(nk, bd16))
    yield
    t = t + tn

    tb = t.astype(BF16)
    vbd = _bd(v, bd16)
    w = _dot(tb, _bd(at, bd16))
    akv = _dot(a_ak.astype(BF16), vbd)
    yield
    uv = _dot(tb, _bd(akv, bd16))
    arb = a_rb.astype(BF16)
    qpd = _dot(arb, _bd(w, bd16))
    yield
    y_in = _dot(jnp.concatenate([arb, a_rk.astype(BF16)], axis=1),
                jnp.concatenate([_bd(uv, bd16), vbd], axis=0))
    lhs_t = jnp.concatenate([bh, kh], axis=0).astype(BF16)
    rhs_t = jnp.concatenate([jnp.concatenate([w, uv], axis=1),
                             jnp.concatenate([jnp.zeros_like(v), v], axis=1)], axis=0).astype(BF16)
    mc = lax.dot_general(lhs_t, rhs_t, (((0,), (0,)), ((), ())), preferred_element_type=F32)
    yield
    m_bd = mc[:, :HGW] * bd32 + eye32 * jnp.exp(tot)
    c_bd = mc[:, HGW:] * bd32
    return y_in, (rt + qpd).astype(BF16), m_bd.astype(BF16), c_bd


def _interleave(gens):
    results = [None] * len(gens)
    active = list(range(len(gens)))
    while active:
        still = []
        for i in active:
            try:
                next(gens[i])
                still.append(i)
            except StopIteration as stop:
                results[i] = stop.value
        active = still
    return results


def _split_dot_left(b_exact, a):
    hi = a.astype(BF16)
    lo = (a - hi.astype(F32)).astype(BF16)
    return _dot(b_exact, hi) + _dot(b_exact, lo)


def _rwscan_kernel(rf, kf, vf, nf, af, lf, rb, kb, vb, nb, ab, lb, tri_ref, m64_ref, bd16_ref, bd32_ref,
                   yf_out, yb_out, s_ref):
    c = pl.program_id(1)

    @pl.when(c == 0)
    def _():
        s_ref[...] = jnp.zeros(s_ref.shape, F32)

    nch = rf.shape[0] // CHUNK
    bd16 = bd16_ref[...]
    bd32 = bd32_ref[0]
    eye32 = bd32_ref[1]
    eye_p = m64_ref[4]
    ngrp = RW_HEADS // HG
    chains = []
    gens = []
    for d, refs, y_out in ((0, (rf, kf, vf, nf, af, lf), yf_out), (1, (rb, kb, vb, nb, ab, lb), yb_out)):
        reverse = d == 1
        consts = (tri_ref[d], m64_ref[2 * d], m64_ref[2 * d + 1], eye_p, bd16, bd32, eye32)
        order = list(range(nch - 1, -1, -1) if reverse else range(nch))
        for g in range(ngrp):
            chains.append((d, g, y_out, order))
            for j in order:
                ops = [x[j * CHUNK:(j + 1) * CHUNK, g * HGW:(g + 1) * HGW] for x in refs]
                gens.append(_chunk_intra(*ops, reverse, consts))
    parts = _interleave(gens)

    states = [s_ref[d, g] for d, g, _, _ in chains]
    for step in range(nch):
        for ci, (d, g, y_out, order) in enumerate(chains):
            j = order[step]
            y_in, qp, m_bd, c_bd = parts[ci * nch + step]
            sb = states[ci].astype(BF16)
            y_out[j * CHUNK:(j + 1) * CHUNK, g * HGW:(g + 1) * HGW] = y_in + _dot(qp, sb)
            states[ci] = _dot(m_bd, sb) + c_bd
    for ci, (d, g, _, _) in enumerate(chains):
        s_ref[d, g] = states[ci]


def _rwscan(r, k, v, kkn, kka, lwf, lwb, bsz, seq, p):
    n = r.shape[0]
    rows = min(NCH * CHUNK, seq)
    nb = seq // rows
    fwd = pl.BlockSpec((rows, RW_WIDTH), lambda b, c: (b * nb + c, 0))
    bwd = pl.BlockSpec((rows, RW_WIDTH), lambda b, c: (b * nb + nb - 1 - c, 0))
    full = lambda a: pl.BlockSpec(a.shape, lambda b, c: (0,) * a.ndim)
    sds = jax.ShapeDtypeStruct((n, RW_WIDTH), F32)
    return pl.pallas_call(
        _rwscan_kernel,
        grid=(bsz, nb),
        in_specs=[fwd] * 6 + [bwd] * 6 + [full(p['sc_tri']), full(p['sc_m64']), full(p['sc_bd16']),
                                           full(p['sc_bd32'])],
        out_specs=[fwd, bwd],
        out_shape=[sds, sds],
        scratch_shapes=[pltpu.VMEM((2, RW_HEADS // HG, HGW, HGW), F32)],
        compiler_params=_cparams(("parallel", "arbitrary")),
        name="rwscan",
    )(r, k, v, kkn, kka, lwf, r, k, v, kkn, kka, lwb, p['sc_tri'], p['sc_m64'], p['sc_bd16'], p['sc_bd32'])


def _post_kernel(x_ref, oa_ref, yf_ref, yb_ref, bonus_ref, g_ref, lng_ref, lnb_ref, seg_ref, wout_ref,
                 g2_ref, wr_ref, br_ref, x1_out, hp_out, gid_out):
    tm = x_ref.shape[0]
    nsplit = 2 if tm % 16 == 0 else 1
    rows_per = tm // nsplit

    def rows_gen(rs):
        seg = seg_ref[...]
        oa_part = _dot(oa_ref[rs, :], wout_ref[:MLA_WIDTH, :])
        y = yf_ref[rs, :] + yb_ref[rs, :]
        mean = _split_dot(y, seg) * (1.0 / RW_HEAD)
        yield
        dlt = y - mean
        var = _split_dot(dlt * dlt, seg) * (1.0 / RW_HEAD)
        yield
        yn = dlt * lax.rsqrt(var + LN_X_EPS) * lng_ref[...] + lnb_ref[...]
        ob = (yn + bonus_ref[rs, :]) * g_ref[rs, :]
        x1 = x_ref[rs, :] + oa_part + _dot(ob.astype(BF16), wout_ref[MLA_WIDTH:, :])
        yield
        x1_out[rs, :] = x1
        h2 = x1 * lax.rsqrt(jnp.mean(x1 * x1, axis=-1, keepdims=True) + RMS_EPS) * g2_ref[...]
        h2b = h2.astype(BF16)
        bits = lax.bitcast_convert_type(h2b.astype(F32), jnp.uint32)
        hp_out[rs, :] = (bits[:, :D_MODEL // 2] >> 16) | (bits[:, D_MODEL // 2:] & jnp.uint32(0xFFFF0000))
        logits = _dot(h2b, wr_ref[...]) + br_ref[...]
        yield
        _, gidx, _ = _group_choice(logits)
        gid_out[rs, :] = jnp.broadcast_to(gidx, (rows_per, LANE))

    _interleave([rows_gen(slice(j * rows_per, (j + 1) * rows_per)) for j in range(nsplit)])


def _group_choice(logits):
    lane_i = lax.broadcasted_iota(jnp.int32, logits.shape, 1)
    lane = lane_i.astype(F32)
    is_g = (lane_i >= N_EXPERTS) & (lane_i < N_EXPERTS + N_GROUPS)
    gl = jnp.where(is_g, logits, -jnp.inf)
    gmax = jnp.max(gl, axis=-1, keepdims=True)
    gidx = jnp.min(jnp.where(gl == gmax, lane, 1e9), axis=-1, keepdims=True) - N_EXPERTS
    p_g = 1.0 / jnp.sum(jnp.where(is_g, jnp.exp(gl - gmax), 0.0), axis=-1, keepdims=True)
    return p_g, gidx, lane


def _post(x2, oa, yf, yb, bonus, g, p, tm):
    n = x2.shape[0]
    tm = min(tm, n)
    row = lambda i: (i, 0)
    fixed = lambda i: (0, 0)
    full = lambda a: pl.BlockSpec(a.shape, fixed)
    rw = pl.BlockSpec((tm, RW_WIDTH), row)
    return pl.pallas_call(
        _post_kernel,
        grid=(n // tm,),
        in_specs=[pl.BlockSpec((tm, D_MODEL), row), pl.BlockSpec((tm, MLA_WIDTH), row), rw, rw, rw, rw,
                  full(p['ln_g']), full(p['ln_b']), full(p['seg']), full(p['w_out']), full(p['g2']),
                  full(p['w_r']), full(p['b_r'])],
        out_specs=[pl.BlockSpec((tm, D_MODEL), row), pl.BlockSpec((tm, D_MODEL // 2), row),
                   pl.BlockSpec((tm, LANE), row)],
        out_shape=[jax.ShapeDtypeStruct((n, D_MODEL), F32), jax.ShapeDtypeStruct((n, D_MODEL // 2), jnp.uint32),
                   jax.ShapeDtypeStruct((n, LANE), F32)],
        compiler_params=_cparams(("parallel",)),
        name="post",
    )(x2, oa, yf, yb, bonus, g, p['ln_g'], p['ln_b'], p['seg'], p['w_out'], p['g2'], p['w_r'], p['b_r'])


def _route_plan(gid, tm, npt):
    g = gid[:, 0].astype(jnp.int32)
    onehot = (g[:, None] == jnp.arange(N_GROUPS, dtype=jnp.int32)[None, :]).astype(jnp.int32)
    csum = jnp.cumsum(onehot, axis=0)
    rank = jnp.sum(csum * onehot, axis=1) - 1
    ntile = (csum[-1] + tm - 1) // tm
    tile_end = jnp.cumsum(ntile)
    off = (tile_end - ntile) * tm
    pos = jnp.sum(onehot * off[None, :], axis=1) + rank
    n_active = tile_end[-1]
    t = jnp.minimum(jnp.arange(npt, dtype=jnp.int32), n_active - 1)
    tile_gid = jnp.sum((t[:, None] >= tile_end[None, :]).astype(jnp.int32), axis=1)
    meta = jnp.concatenate([n_active[None], tile_gid]).astype(jnp.int32)
    return pos.astype(jnp.int32), meta


def _row_dma_loops(n_rows, make_copy, all_rows_copy):
    def issue(h, carry):
        make_copy(2 * h).start(priority=0)
        make_copy(2 * h + 1).start(priority=1)
        return carry
    lax.fori_loop(0, n_rows // 2, issue, 0, unroll=4)
    all_rows_copy.wait()


def _scatter_kernel(pos_ref, src_ref, init_ref, dst_ref, sem):
    n_rows = src_ref.shape[0]
    _row_dma_loops(n_rows, lambda r: pltpu.make_async_copy(
        src_ref.at[pl.ds(r, 1)], dst_ref.at[pl.ds(pos_ref[0, 0, r], 1)], sem),
        pltpu.make_async_copy(src_ref, dst_ref.at[pl.ds(0, n_rows)], sem))


def _scatter_rows(pos, src, n_dst, tms):
    n, w = src.shape
    tms = min(tms, n)
    return pl.pallas_call(
        _scatter_kernel,
        grid=(n // tms,),
        in_specs=[pl.BlockSpec((1, 1, tms), lambda i: (i, 0, 0), memory_space=pltpu.SMEM),
                  pl.BlockSpec((tms, w), lambda i: (i, 0)), pl.BlockSpec(memory_space=pl.ANY)],
        out_specs=pl.BlockSpec(memory_space=pl.ANY),
        out_shape=jax.ShapeDtypeStruct((n_dst, w), src.dtype),
        scratch_shapes=[pltpu.SemaphoreType.DMA(())],
        input_output_aliases={2: 0},
        compiler_params=_cparams(("arbitrary",)),
        name="moe_scatter",
    )(pos.reshape(n // tms, 1, tms), src, jnp.zeros((n_dst, w), src.dtype))


def _moe_group_kernel(meta_ref, xs_ref, wr_ref, br_ref, wg_ref, wu_ref, wd_ref, ys_ref):
    i = pl.program_id(0)

    @pl.when(i >= meta_ref[0])
    def _():
        ys_ref[...] = jnp.zeros(ys_ref.shape, F32)

    @pl.when(i < meta_ref[0])
    def _():
        tm = xs_ref.shape[0]
        word = xs_ref[...]
        h2 = jnp.concatenate([lax.bitcast_convert_type(word << 16, F32),
                              lax.bitcast_convert_type(word & jnp.uint32(0xFFFF0000), F32)],
                             axis=1).astype(BF16)
        def up(c):
            ks = range(c * MOE_EPC, (c + 1) * MOE_EPC)
            return (jnp.concatenate([_dot(h2, wg_ref[k]) for k in ks], axis=1),
                    jnp.concatenate([_dot(h2, wu_ref[k]) for k in ks], axis=1))

        pend = up(0)
        logits = _dot(h2, wr_ref[...]) + br_ref[...]
        p_g, _, lane = _group_choice(logits)
        first = (meta_ref[1 + i] * EXPERTS_PER_GROUP).astype(F32)
        in_grp = (lane >= first) & (lane < first + EXPERTS_PER_GROUP)
        el = jnp.where(in_grp, logits, -jnp.inf)
        v1 = jnp.max(el, axis=-1, keepdims=True)
        i1 = jnp.min(jnp.where(el == v1, lane, 1e9), axis=-1, keepdims=True)
        el2 = jnp.where(lane == i1, -jnp.inf, el)
        v2 = jnp.max(el2, axis=-1, keepdims=True)
        i2 = jnp.min(jnp.where(el2 == v2, lane, 1e9), axis=-1, keepdims=True)
        e2 = jnp.exp(v2 - v1)
        den = 1.0 + e2
        w1 = (1.0 / den) * p_g
        w2 = (e2 / den) * p_g

        acc = None
        nchunk = EXPERTS_PER_GROUP // MOE_EPC
        for c in range(nchunk):
            nxt = up(c + 1) if c + 1 < nchunk else None
            hg, hu = pend
            gate = jnp.concatenate(
                [jnp.broadcast_to(jnp.where(i1 == first + k, w1, 0.0) + jnp.where(i2 == first + k, w2, 0.0),
                                  (tm, D_EXPERT)) for k in range(c * MOE_EPC, (c + 1) * MOE_EPC)], axis=1)
            act = hg * jax.nn.sigmoid(hg) * hu * gate
            wd = wd_ref[c * MOE_EPC:(c + 1) * MOE_EPC].reshape(MOE_EPC * D_EXPERT, D_MODEL)
            part = _dot(act.astype(BF16), wd)
            acc = part if acc is None else acc + part
            pend = nxt
        ys_ref[...] = acc


def _moe_grouped(xs, meta, p, tm, npt):
    def tile(i, m):
        return (jnp.minimum(i, m[0] - 1), 0)

    def group(i, m):
        return (m[1 + i], 0, 0)

    grid_spec = pltpu.PrefetchScalarGridSpec(
        num_scalar_prefetch=1,
        grid=(npt,),
        in_specs=[pl.BlockSpec((tm, D_MODEL // 2), tile),
                  pl.BlockSpec(p['w_r'].shape, lambda i, m: (0, 0)),
                  pl.BlockSpec(p['b_r'].shape, lambda i, m: (0, 0)),
                  pl.BlockSpec((EXPERTS_PER_GROUP, D_MODEL, D_EXPERT), group),
                  pl.BlockSpec((EXPERTS_PER_GROUP, D_MODEL, D_EXPERT), group),
                  pl.BlockSpec((EXPERTS_PER_GROUP, D_EXPERT, D_MODEL), group)],
        out_specs=pl.BlockSpec((tm, D_MODEL), lambda i, m: (i, 0)),
    )
    return pl.pallas_call(
        _moe_group_kernel,
        grid_spec=grid_spec,
        out_shape=jax.ShapeDtypeStruct((npt * tm, D_MODEL), F32),
        compiler_params=_cparams(("arbitrary",)),
        name="moe_group",
    )(meta, xs, p['w_r'], p['b_r'], p['wg'], p['wu'], p['wd'])


def _gather_kernel(pos_ref, pos_next_ref, x1_ref, ys_ref, o_ref, buf, sems):
    i = pl.program_id(0)
    n_rows = buf.shape[1]
    slot = lax.rem(i, 2)

    def start_rows(idx_ref, s):
        def issue(h, carry):
            for q in range(2):
                r = 2 * h + q
                pltpu.make_async_copy(ys_ref.at[pl.ds(idx_ref[0, 0, r], 1)], buf.at[s, pl.ds(r, 1)],
                                      sems.at[s]).start(priority=q)
            return carry
        lax.fori_loop(0, n_rows // 2, issue, 0, unroll=4)

    @pl.when(i == 0)
    def _():
        start_rows(pos_ref, 0)

    @pl.when(i + 1 < pl.num_programs(0))
    def _():
        start_rows(pos_next_ref, 1 - slot)

    pltpu.make_async_copy(ys_ref.at[pl.ds(0, n_rows)], buf.at[slot], sems.at[slot]).wait()
    o_ref[...] = x1_ref[...] + buf[slot]


def _gather_add(pos, x1, ys, tms):
    n, d = x1.shape
    tms = min(tms, n)
    nsteps = n // tms
    pos3 = pos.reshape(nsteps, 1, tms)
    return pl.pallas_call(
        _gather_kernel,
        grid=(nsteps,),
        in_specs=[pl.BlockSpec((1, 1, tms), lambda i: (i, 0, 0), memory_space=pltpu.SMEM),
                  pl.BlockSpec((1, 1, tms), lambda i: (jnp.minimum(i + 1, nsteps - 1), 0, 0),
                               memory_space=pltpu.SMEM),
                  pl.BlockSpec((tms, d), lambda i: (i, 0)),
                  pl.BlockSpec(memory_space=pl.ANY)],
        out_specs=pl.BlockSpec((tms, d), lambda i: (i, 0)),
        out_shape=jax.ShapeDtypeStruct((n, d), F32),
        scratch_shapes=[pltpu.VMEM((2, tms, d), F32), pltpu.SemaphoreType.DMA((2,))],
        compiler_params=_cparams(("arbitrary",)),
        name="moe_gather",
    )(pos3, pos3, x1, ys)


def _head_pad_cols(w, head_w, lo, hi, dst, width=HEAD_PAD, perm=None):
    rows = w.shape[0]
    wh = w.reshape(rows, -1, head_w)[:, :, lo:hi]
    if perm is not None:
        wh = wh[:, :, perm]
    out = jnp.zeros((rows, wh.shape[1], width), w.dtype)
    return out.at[:, :, dst:dst + (hi - lo)].set(wh).reshape(rows, -1)


def _pack_params(lp, max_seq):
    (norm1_g, w_in, q_a_norm_g, w_uq, kv_a_norm_g, w_ukv, q_norm_g, k_norm_g, mu_shift, w0_f, w_up_f,
     w0_b, w_up_b, a0, a_up, g_up, k_k, k_a, r_k, ln_x_g, ln_x_b, w_out, norm2_g, w_router_group,
     b_router_group, w_router_expert, b_router_expert, w_expert_gate, w_expert_up, w_expert_down) = lp
    half = QK_ROPE // 2
    swap = jnp.concatenate([jnp.arange(half, QK_ROPE), jnp.arange(half)])
    zcol = lambda rows, n: jnp.zeros((rows, n), F32)
    p = {}
    c0 = Q_LORA + KV_LORA
    w_kr = w_in[:, c0:c0 + QK_ROPE]
    u0 = c0 + QK_ROPE
    wu_ = w_in[:, u0:]
    d = D_MODEL
    w_in_p = jnp.concatenate([
        w_in[:, :c0],
        zcol(d, QK_NOPE), w_kr, zcol(d, LANE - QK_HEAD),
        zcol(d, QK_NOPE), w_kr[:, swap], zcol(d, LANE - QK_HEAD),
        wu_[:, :3 * RW_WIDTH + 2 * DECAY_LORA + A_LORA], zcol(d, LANE - A_LORA),
        wu_[:, 3 * RW_WIDTH + 2 * DECAY_LORA + A_LORA:]], axis=1)
    p['w_in'] = w_in_p.astype(BF16)
    mu = mu_shift[None, :]
    p['mu'] = jnp.concatenate([mu[:, :3 * RW_WIDTH + 2 * DECAY_LORA + A_LORA], zcol(1, LANE - A_LORA),
                               mu[:, 3 * RW_WIDTH + 2 * DECAY_LORA + A_LORA:]], axis=1)
    p['g1'] = norm1_g[None, :]
    p['qag'] = q_a_norm_g[None, :]
    p['kvag'] = kv_a_norm_g[None, :]
    p['w_uq'] = jnp.concatenate([
        _head_pad_cols(w_uq, QK_HEAD, 0, QK_HEAD, 0),
        _head_pad_cols(w_uq, QK_HEAD, QK_NOPE, QK_HEAD, QK_NOPE, perm=swap)], axis=1).astype(BF16)
    p['w_ukv'] = jnp.concatenate([
        _head_pad_cols(w_ukv, QK_NOPE + V_HEAD, 0, QK_NOPE, 0),
        _head_pad_cols(w_ukv, QK_NOPE + V_HEAD, QK_NOPE, QK_NOPE + V_HEAD, 0)], axis=1).astype(BF16)

    def gain_rows(g):
        main = jnp.concatenate([g, jnp.zeros((LANE - QK_HEAD,), F32)])
        swp = jnp.concatenate([jnp.zeros((QK_NOPE,), F32), g[QK_NOPE:][swap], jnp.zeros((LANE - QK_HEAD,), F32)])
        return jnp.stack([main, swp])
    kbound = math.sqrt(QK_HEAD) * jnp.max(jnp.abs(k_norm_g))
    qbound = jnp.max(jnp.abs(q_norm_g)) * LOG2E
    p['gq'] = jnp.concatenate([gain_rows(q_norm_g), jnp.full((1, LANE), qbound * kbound, F32)], axis=0)
    p['gk'] = gain_rows(k_norm_g)
    p['fast_softmax'] = 2.0 * qbound * kbound <= FAST_SOFTMAX_MAX_SHIFT
    inv_freq = 1.0 / (ROPE_THETA ** (jnp.arange(half, dtype=F32) / half))
    ang = jnp.arange(max_seq, dtype=F32)[:, None] * inv_freq[None, :]
    cos, sin = jnp.cos(ang), jnp.sin(ang)
    zpad = jnp.zeros((max_seq, LANE - QK_HEAD), F32)
    p['ct'] = jnp.concatenate([jnp.ones((max_seq, QK_NOPE), F32), cos, cos, zpad], axis=1)
    p['st'] = jnp.concatenate([jnp.zeros((max_seq, QK_NOPE), F32), -sin, sin, zpad], axis=1)

    zl = jnp.zeros((DECAY_LORA, RW_WIDTH), F32)
    p['w_dec'] = jnp.concatenate([jnp.concatenate([w_up_f, zl], axis=1),
                                  jnp.concatenate([zl, w_up_b], axis=1)], axis=0).astype(BF16)
    p['w0'] = jnp.concatenate([w0_f, w0_b])[None, :]
    p['a_up'] = jnp.concatenate([a_up, jnp.zeros((LANE - A_LORA, RW_WIDTH), F32)], axis=0).astype(BF16)
    p['a0'] = a0[None, :]
    p['g_up'] = g_up.astype(BF16)
    p['k_k'] = k_k[None, :]
    p['k_a'] = k_a[None, :]
    p['r_k'] = r_k.reshape(1, RW_WIDTH)
    hid = jnp.arange(RW_WIDTH) // RW_HEAD
    p['seg'] = (hid[:, None] == hid[None, :]).astype(BF16)
    ti = jnp.arange(CHUNK)[:, None]
    tj = jnp.arange(CHUNK)[None, :]
    p['sc_tri'] = jnp.stack([tj <= ti, tj >= ti]).astype(BF16)
    ps = (jnp.arange(HGW) % CHUNK)[None, :]
    p['sc_m64'] = jnp.stack([ps < ti, ps <= ti, ps > ti, ps >= ti, ps == ti]).astype(F32)
    blk = jnp.arange(HGW) // RW_HEAD
    same = blk[:, None] == blk[None, :]
    p['sc_bd16'] = same.astype(BF16)
    p['sc_bd32'] = jnp.stack([same, jnp.eye(HGW, dtype=bool)]).astype(F32)
    p['ln_g'] = ln_x_g[None, :]
    p['ln_b'] = ln_x_b[None, :]
    p['w_out'] = w_out.astype(BF16)
    p['g2'] = norm2_g[None, :]
    p['w_r'] = jnp.concatenate([w_router_expert, w_router_group,
                                zcol(d, LANE - N_EXPERTS - N_GROUPS)], axis=1).astype(BF16)
    p['b_r'] = jnp.concatenate([b_router_expert, b_router_group,
                                jnp.zeros((LANE - N_EXPERTS - N_GROUPS,), F32)])[None, :]
    p['wg'] = w_expert_gate.astype(BF16)
    p['wu'] = w_expert_up.astype(BF16)
    p['wd'] = w_expert_down.astype(BF16)
    return p


TM_PROJ = 512
PROJ_SPLIT = 2
TQ = 1024
TK = 8192
TKS = 2048
TQ_MAX = 1024
TK_MAX = 2048
TKS_MAX = 1024
TM_POST = 512
TM_MOE = 512
TM_ROWS = 1024


def _layer(x, p):
    bsz, seq, d = x.shape
    n = bsz * seq
    x2 = x.reshape(n, d)
    q, k, v, r, k2, vv, kkn, kka, lwf, lwb, g, bonus = _proj(x2, seq, p, TM_PROJ)
    oa = _attn(q, k, v, bsz, seq, p['fast_softmax'])
    yf, yb = _rwscan(r, k2, vv, kkn, kka, lwf, lwb, bsz, seq, p)
    x1, hp, gid = _post(x2, oa, yf, yb, bonus, g, p, TM_POST)
    tm = min(TM_MOE, n)
    npt = n // tm + N_GROUPS
    pos, meta = _route_plan(gid, tm, npt)
    xs = _scatter_rows(pos, hp, npt * tm, TM_ROWS)
    ys = _moe_grouped(xs, meta, p, tm, npt)
    out = _gather_add(pos, x1, ys, TM_ROWS)
    return out.reshape(bsz, seq, d)


def kernel(x_prompt, x_sample, norm1_g, w_in, q_a_norm_g, w_uq, kv_a_norm_g, w_ukv, q_norm_g, k_norm_g, mu_shift, w0_f, w_up_f, w0_b, w_up_b, a0, a_up, g_up, k_k, k_a, r_k, ln_x_g, ln_x_b, w_out, norm2_g, w_router_group, b_router_group, w_router_expert, b_router_expert, w_expert_gate, w_expert_up, w_expert_down):
    layer_params = (norm1_g, w_in, q_a_norm_g, w_uq, kv_a_norm_g, w_ukv, q_norm_g, k_norm_g, mu_shift,
                    w0_f, w_up_f, w0_b, w_up_b, a0, a_up, g_up, k_k, k_a, r_k, ln_x_g, ln_x_b, w_out,
                    norm2_g, w_router_group, b_router_group, w_router_expert, b_router_expert,
                    w_expert_gate, w_expert_up, w_expert_down)
    y_prompt, y_sample = x_prompt, x_sample
    max_seq = max(x_prompt.shape[1], x_sample.shape[1])
    for layer in range(norm1_g.shape[0]):
        p = _pack_params([w[layer] for w in layer_params], max_seq)
        y_prompt = _layer(y_prompt, p)
        y_sample = _layer(y_sample, p)
    return (y_prompt, y_sample)
```
